```python
import math
import jax, jax.numpy as jnp
from jax import lax
import numpy as np

D_MODEL = 1024
BATCH = 8
SEQ = 8192
DEPTH = 1

MEM_LEN = 256
SSD_HEAD_DIM = 64
SSD_WIDTH = D_MODEL // 2
SSD_HEADS = SSD_WIDTH // SSD_HEAD_DIM
SSD_GROUPS = 2
SSD_HEADS_PER_GROUP = SSD_HEADS // SSD_GROUPS
SSD_STATE = 64
SSD_CONV = 4
SSD_CHUNK = 128
SC_WIDTH = D_MODEL - SSD_WIDTH
SC_CONV = 3
MIX_WIDTH = SSD_WIDTH + SC_WIDTH
XBC_WIDTH = SSD_WIDTH + 2 * SSD_GROUPS * SSD_STATE
IN_COLS = SSD_WIDTH + XBC_WIDTH + SSD_HEADS + 3 * SC_WIDTH
XA_HEADS = 4
XA_HEAD_DIM = D_MODEL // XA_HEADS
N_GROUPS_MOE = 4
EXPERTS_PER_GROUP = 8
N_EXPERTS = N_GROUPS_MOE * EXPERTS_PER_GROUP
TOP_K = 2
D_EXPERT = D_MODEL // 2
MOE_BLOCK = 128
EPS = 1e-6

kernel_name = 'hybrid_ssd_shortconv_memxattn_hmoe'


def rms_norm(x, g):
    xf = x.astype(jnp.float32)
    y = xf * lax.rsqrt(jnp.mean(xf * xf, axis=-1, keepdims=True) + EPS)
    return (y * g.astype(jnp.float32)).astype(x.dtype)


def causal_dwconv(u, w):
    k = w.shape[0]
    s = u.shape[1]
    up = jnp.pad(u, ((0, 0), (k - 1, 0), (0, 0)))
    out = up[:, 0:s] * w[0]
    for i in range(1, k):
        out = out + up[:, i:i + s] * w[i]
    return out


def ssd_chunked(x, dt, a, bm, cm):
    f32 = jnp.float32
    bsz, s, _ = x.shape
    nc = s // SSD_CHUNK
    G, R, P, N, Q = SSD_GROUPS, SSD_HEADS_PER_GROUP, SSD_HEAD_DIM, SSD_STATE, SSD_CHUNK
    xc = x.astype(f32).reshape(bsz, nc, Q, G, R, P)
    dtc = dt.astype(f32).reshape(bsz, nc, Q, G, R)
    bc = bm.astype(f32).reshape(bsz, nc, Q, G, N)
    cc = cm.astype(f32).reshape(bsz, nc, Q, G, N)
    a_dt = (dtc * a.astype(f32).reshape(G, R)).transpose(0, 3, 4, 1, 2)
    cs = jnp.cumsum(a_dt, axis=-1)
    xdt = xc * dtc[..., None]
    causal = jnp.tril(jnp.ones((Q, Q), dtype=bool))
    decay = jnp.exp(jnp.where(causal, cs[..., :, None] - cs[..., None, :], -jnp.inf))
    cb = jnp.einsum('bclgn,bcsgn->bgcls', cc, bc)
    y_diag = jnp.einsum('bgcls,bgrcls,bcsgrp->bclgrp', cb, decay, xdt)
    decay_in = jnp.exp(cs[..., -1:] - cs)
    states = jnp.einsum('bclgn,bgrcl,bclgrp->bcgrpn', bc, decay_in, xdt)
    chunk_decay = jnp.exp(cs[..., -1])

    def step(carry, inp):
        st, dec = inp
        return carry * dec[..., None, None] + st, carry

    init = jnp.zeros((bsz, G, R, P, N), states.dtype)
    _, prev = lax.scan(step, init, (states.transpose(1, 0, 2, 3, 4, 5), chunk_decay.transpose(3, 0, 1, 2)))
    prev = prev.transpose(1, 0, 2, 3, 4, 5)
    y_off = jnp.einsum('bclgn,bcgrpn,bgrcl->bclgrp', cc, prev, jnp.exp(cs))
    return (y_diag + y_off).reshape(bsz, s, G * R * P).astype(x.dtype)


def hybrid_mixer(n, w_in, conv_ssd_w, conv_ssd_b, dt_bias, a_log, d_skip, norm_ssd_gate, conv_short_w, w_out):
    proj = n @ w_in
    o1 = SSD_WIDTH
    o2 = o1 + XBC_WIDTH
    o3 = o2 + SSD_HEADS
    o4 = o3 + SC_WIDTH
    o5 = o4 + SC_WIDTH
    z, xbc, dt_raw, g_b, g_c, v = jnp.split(proj, [o1, o2, o3, o4, o5], axis=-1)
    xbc = jax.nn.silu(causal_dwconv(xbc, conv_ssd_w) + conv_ssd_b)
    xs, bm, cm = jnp.split(xbc, [SSD_WIDTH, SSD_WIDTH + SSD_GROUPS * SSD_STATE], axis=-1)
    dt = jax.nn.softplus(dt_raw + dt_bias)
    a = -jnp.exp(a_log)
    y = ssd_chunked(xs, dt, a, bm, cm) + xs * jnp.repeat(d_skip, SSD_HEAD_DIM)
    gated = (y * jax.nn.silu(z)).reshape(y.shape[:-1] + (SSD_GROUPS, SSD_WIDTH // SSD_GROUPS))
    y_ssd = rms_norm(gated, norm_ssd_gate.reshape(SSD_GROUPS, SSD_WIDTH // SSD_GROUPS)).reshape(y.shape)
    y_sc = g_b * causal_dwconv(g_c * v, conv_short_w)
    return jnp.concatenate([y_ssd, y_sc], axis=-1) @ w_out


def memory_cross_attention(n, mem_n, w_q, w_kv, w_o):
    bsz, s, d = n.shape
    m = mem_n.shape[1]
    q = (n @ w_q).reshape(bsz, s, XA_HEADS, XA_HEAD_DIM)
    k, v = jnp.split(mem_n @ w_kv, 2, axis=-1)
    k = k.reshape(bsz, m, XA_HEADS, XA_HEAD_DIM)
    v = v.reshape(bsz, m, XA_HEADS, XA_HEAD_DIM)
    scores = jnp.einsum('bshd,bmhd->bhsm', q, k).astype(jnp.float32) * (XA_HEAD_DIM ** -0.5)
    p = jax.nn.softmax(scores, axis=-1).astype(v.dtype)
    o = jnp.einsum('bhsm,bmhd->bshd', p, v).reshape(bsz, s, d)
    return o @ w_o


def hierarchical_moe(n, w_rg, b_rg, w_re, b_re, w_gate, w_up, w_down):
    bsz, s, d = n.shape
    t = n.reshape(-1, d)
    T = t.shape[0]
    f32 = jnp.float32
    g_prob = jax.nn.softmax((t @ w_rg).astype(f32) + b_rg.astype(f32), axis=-1)
    g_idx = jnp.argmax(g_prob, axis=-1).astype(jnp.int32)
    g_w = jnp.take_along_axis(g_prob, g_idx[:, None], axis=-1)
    e_logits = ((t @ w_re).astype(f32) + b_re.astype(f32)).reshape(T, N_GROUPS_MOE, EXPERTS_PER_GROUP)
    e_logits = jnp.take_along_axis(e_logits, g_idx[:, None, None], axis=1)[:, 0]
    e_prob = jax.nn.softmax(e_logits, axis=-1)
    top_w, top_i = lax.top_k(e_prob, TOP_K)
    top_w = top_w / jnp.sum(top_w, axis=-1, keepdims=True)
    comb = (g_w * top_w).reshape(-1)
    flat_e = (g_idx[:, None] * EXPERTS_PER_GROUP + top_i.astype(jnp.int32)).reshape(-1)
    n_assign = T * TOP_K
    n_blocks = (n_assign + MOE_BLOCK - 1) // MOE_BLOCK + N_EXPERTS
    order = jnp.argsort(flat_e)
    tok_of = order // TOP_K
    sorted_e = flat_e[order]
    sizes = jnp.bincount(flat_e, length=N_EXPERTS)
    padded = ((sizes + MOE_BLOCK - 1) // MOE_BLOCK) * MOE_BLOCK
    pad_end = jnp.cumsum(padded)
    pad_start = pad_end - padded
    seg_start = jnp.cumsum(sizes) - sizes
    dest = pad_start[sorted_e] + (jnp.arange(n_assign) - seg_start[sorted_e])
    buf = jnp.zeros((n_blocks * MOE_BLOCK, d), t.dtype).at[dest].set(t[tok_of])
    block_expert = jnp.clip(jnp.searchsorted(pad_end, jnp.arange(n_blocks) * MOE_BLOCK, side='right'), 0, N_EXPERTS - 1)

    def expert_block(args):
        xb, e = args
        return (jax.nn.silu(xb @ w_gate[e]) * (xb @ w_up[e])) @ w_down[e]

    y_buf = lax.map(expert_block, (buf.reshape(n_blocks, MOE_BLOCK, d), block_expert))
    y_sorted = y_buf.reshape(-1, d)[dest] * comb[order][:, None].astype(t.dtype)
    out = jnp.zeros_like(t).at[tok_of].add(y_sorted)
    return out.reshape(bsz, s, d)


def setup_inputs(seed: int = 0) -> dict:
    key = jax.random.key(seed)
    ks = jax.random.split(key, 26)
    L, D = DEPTH, D_MODEL

    def nrm(k, shape, scale):
        return jax.random.normal(k, shape, jnp.float32) * scale

    def gain(k, shape):
        return 1.0 + 0.02 * jax.random.normal(k, shape, jnp.float32)

    dt0 = jnp.exp(jax.random.uniform(ks[7], (L, SSD_HEADS), jnp.float32, minval=math.log(1e-3), maxval=math.log(1e-1)))
    return {
        'x': nrm(ks[0], (BATCH, SEQ, D), 1.0),
        'mem': nrm(ks[1], (BATCH, MEM_LEN, D), 1.0),
        'norm_mem': gain(ks[2], (D,)),
        'norm_mix': gain(ks[3], (L, D)),
        'w_in': nrm(ks[4], (L, D, IN_COLS), D ** -0.5),
        'conv_ssd_w': nrm(ks[5], (L, SSD_CONV, XBC_WIDTH), SSD_CONV ** -0.5),
        'conv_ssd_b': nrm(ks[6], (L, XBC_WIDTH), 0.02),
        'dt_bias': dt0 + jnp.log(-jnp.expm1(-dt0)),
        'a_log': jnp.log(jax.random.uniform(ks[8], (L, SSD_HEADS), jnp.float32, minval=1.0, maxval=16.0)),
        'd_skip': 1.0 + 0.1 * jax.random.normal(ks[9], (L, SSD_HEADS), jnp.float32),
        'norm_ssd_gate': gain(ks[10], (L, SSD_WIDTH)),
        'conv_short_w': nrm(ks[11], (L, SC_CONV, SC_WIDTH), SC_CONV ** -0.5),
        'w_out': nrm(ks[12], (L, MIX_WIDTH, D), MIX_WIDTH ** -0.5),
        'norm_xattn': gain(ks[13], (L, D)),
        'w_q': nrm(ks[14], (L, D, D), D ** -0.5),
        'w_kv': nrm(ks[15], (L, D, 2 * D), D ** -0.5),
        'w_o': nrm(ks[16], (L, D, D), D ** -0.5),
        'norm_moe': gain(ks[17], (L, D)),
        'w_router_group': nrm(ks[18], (L, D, N_GROUPS_MOE), D ** -0.5),
        'b_router_group': nrm(ks[19], (L, N_GROUPS_MOE), 0.01),
        'w_router_expert': nrm(ks[20], (L, D, N_EXPERTS), D ** -0.5),
        'b_router_expert': nrm(ks[21], (L, N_EXPERTS), 0.01),
        'w_gate': nrm(ks[22], (L, N_EXPERTS, D, D_EXPERT), D ** -0.5),
        'w_up': nrm(ks[23], (L, N_EXPERTS, D, D_EXPERT), D ** -0.5),
        'w_down': nrm(ks[24], (L, N_EXPERTS, D_EXPERT, D), D_EXPERT ** -0.5),
        'norm_final': gain(ks[25], (D,)),
    }


def reference(x, mem, norm_mem, norm_mix, w_in, conv_ssd_w, conv_ssd_b, dt_bias, a_log, d_skip,
              norm_ssd_gate, conv_short_w, w_out, norm_xattn, w_q, w_kv, w_o, norm_moe,
              w_router_group, b_router_group, w_router_expert, b_router_expert,
              w_gate, w_up, w_down, norm_final):
    mem_n = rms_norm(mem, norm_mem)
    h = x
    for l in range(DEPTH):
        h = h + hybrid_mixer(rms_norm(h, norm_mix[l]), w_in[l], conv_ssd_w[l], conv_ssd_b[l],
                             dt_bias[l], a_log[l], d_skip[l], norm_ssd_gate[l], conv_short_w[l], w_out[l])
        h = h + memory_cross_attention(rms_norm(h, norm_xattn[l]), mem_n, w_q[l], w_kv[l], w_o[l])
        h = h + hierarchical_moe(rms_norm(h, norm_moe[l]), w_router_group[l], b_router_group[l],
                                 w_router_expert[l], b_router_expert[l], w_gate[l], w_up[l], w_down[l])
    return rms_norm(h, norm_final)
```

```python
import functools

import jax
import jax.numpy as jnp
from jax import lax
from jax.experimental import pallas as pl
from jax.experimental.pallas import tpu as pltpu

F32 = jnp.float32
BF16 = jnp.bfloat16
EPS = 1e-6

LANES = 128
SUBLANES = 8
VMEM_LIMIT_BYTES = 56 * 1024 * 1024

D_MODEL = 1024
SSD_WIDTH = 512
SSD_HEAD_DIM = 64
SSD_HEADS = 8
SSD_GROUPS = 2
SSD_HEADS_PER_GROUP = 4
SSD_STATE = 64
SSD_CHUNK = 128
XBC_WIDTH = SSD_WIDTH + 2 * SSD_GROUPS * SSD_STATE
SC_WIDTH = 512
XA_HEADS = 4
XA_HEAD_DIM = 256
N_GROUPS_MOE = 4
EXPERTS_PER_GROUP = 8
N_EXPERTS = 32
D_EXPERT = 512

COL_Z = 0
COL_XBC = COL_Z + SSD_WIDTH
COL_GB = COL_XBC + XBC_WIDTH
COL_GC = COL_GB + SC_WIDTH
COL_V = COL_GC + SC_WIDTH
COL_DT = COL_V + SC_WIDTH
PROJ_WIDTH = COL_DT + LANES

SEQ_TILE = 512
ROW_TILE = 512
EXPERT_BLOCK = 256
HALO = SUBLANES

ROUTER_GROUP_ROW = 0
ROUTER_EXPERT_ROW = SUBLANES


def _rms(x, g):
    return x * lax.rsqrt(jnp.mean(x * x, axis=-1, keepdims=True) + EPS) * g


def _silu(x):
    return x * jax.nn.sigmoid(x)


def _dot(a, b):
    return jnp.dot(a, b, preferred_element_type=F32)


def _mixer_kernel(x_ref, gmix_ref, win_ref, cw_ref, cb_ref, dtb_ref, a_ref, dskip_ref, gate_ref,
                  scw_ref, wout_ref, tri_ref, h_ref,
                  conv_ref, cv_ref, act_ref, dt_ref, y_ref, state_ref):
    ts = x_ref.shape[1]
    s = pl.program_id(1)
    x = x_ref[0]
    nb = _rms(x, gmix_ref[...]).astype(BF16)

    @pl.when(s == 0)
    def _():
        conv_ref[0:HALO, :] = jnp.zeros((HALO, XBC_WIDTH), F32)
        cv_ref[0:HALO, :] = jnp.zeros((HALO, SC_WIDTH), F32)
        state_ref[...] = jnp.zeros(state_ref.shape, F32)

    @pl.when(s > 0)
    def _():
        conv_ref[0:HALO, :] = conv_ref[ts:ts + HALO, :]
        cv_ref[0:HALO, :] = cv_ref[ts:ts + HALO, :]

    xbc = _dot(nb, win_ref[:, COL_XBC:COL_XBC + XBC_WIDTH])
    conv_ref[HALO:HALO + ts, :] = xbc
    acc = cb_ref[...] + cw_ref[3:4, :] * xbc
    for i in range(3):
        acc = acc + cw_ref[i:i + 1, :] * conv_ref[HALO - 3 + i:HALO - 3 + i + ts, :]
    act_ref[...] = _silu(acc)

    dtr = _dot(nb, win_ref[:, COL_DT:COL_DT + LANES]) + dtb_ref[...]
    dt_ref[...] = jnp.maximum(dtr, 0.0) + jnp.log1p(jnp.exp(-jnp.abs(dtr)))

    row = lax.broadcasted_iota(jnp.int32, (SSD_CHUNK, SSD_CHUNK), 0)
    col = lax.broadcasted_iota(jnp.int32, (SSD_CHUNK, SSD_CHUNK), 1)
    causal = row >= col

    def chunk_body(c, carry):
        r0 = pl.multiple_of(c * SSD_CHUNK, SSD_CHUNK)
        rows = pl.ds(r0, SSD_CHUNK)
        dt = dt_ref[rows, :]
        adt = dt * a_ref[...]
        cs = jnp.dot(tri_ref[...], adt, preferred_element_type=F32,
                     precision=lax.Precision.HIGHEST)
        cs_t = cs.T
        dt_t = dt.T
        ecs = jnp.exp(cs)
        cs_last = cs[SSD_CHUNK - 1:SSD_CHUNK, :]
        e_last = jnp.exp(cs_last)
        bm = act_ref[rows, SSD_WIDTH:SSD_WIDTH + SSD_GROUPS * SSD_STATE]
        cm = act_ref[rows, SSD_WIDTH + SSD_GROUPS * SSD_STATE:XBC_WIDTH]
        bm_t = bm.T
        for g in range(SSD_GROUPS):
            cg = cm[:, g * SSD_STATE:(g + 1) * SSD_STATE]
            bg = bm[:, g * SSD_STATE:(g + 1) * SSD_STATE]
            bg_t = bm_t[g * SSD_STATE:(g + 1) * SSD_STATE, :]
            cb = lax.dot_general(cg.astype(BF16), bg.astype(BF16), (((1,), (1,)), ((), ())),
                                 preferred_element_type=F32)
            ys = []
            for r in range(SSD_HEADS_PER_GROUP):
                h = g * SSD_HEADS_PER_GROUP + r
                cs_col = cs[:, h:h + 1]
                cs_row = cs_t[h:h + 1, :]
                decay = jnp.exp(jnp.where(causal, cs_col - cs_row, -jnp.inf))
                xh = act_ref[rows, h * SSD_HEAD_DIM:(h + 1) * SSD_HEAD_DIM]
                xdt = (xh * dt[:, h:h + 1]).astype(BF16)
                y = _dot((cb * decay).astype(BF16), xdt)
                st = state_ref[h]
                y = y + _dot((cg * ecs[:, h:h + 1]).astype(BF16), st.astype(BF16))
                cl = cs_last[:, h:h + 1]
                w_row = jnp.exp(cl - cs_row) * dt_t[h:h + 1, :]
                state_ref[h] = e_last[:, h:h + 1] * st + _dot((bg_t * w_row).astype(BF16),
                                                              xh.astype(BF16))
                ys.append(y)
            y_ref[rows, g * 256:(g + 1) * 256] = jnp.concatenate(ys, axis=-1)
        return carry

    lax.fori_loop(0, ts // SSD_CHUNK, chunk_body, 0)

    z = _dot(nb, win_ref[:, COL_Z:COL_Z + SSD_WIDTH])
    y = y_ref[...] + act_ref[:, 0:SSD_WIDTH] * dskip_ref[...]
    gated = y * _silu(z)
    gw = SSD_WIDTH // SSD_GROUPS
    y_ssd = jnp.concatenate(
        [_rms(gated[:, g * gw:(g + 1) * gw], gate_ref[:, g * gw:(g + 1) * gw]) for g in range(SSD_GROUPS)],
        axis=-1).astype(BF16)

    g_b = _dot(nb, win_ref[:, COL_GB:COL_GB + SC_WIDTH])
    g_c = _dot(nb, win_ref[:, COL_GC:COL_GC + SC_WIDTH])
    v = _dot(nb, win_ref[:, COL_V:COL_V + SC_WIDTH])
    u = g_c * v
    cv_ref[HALO:HALO + ts, :] = u
    conv3 = scw_ref[2:3, :] * u
    for i in range(2):
        conv3 = conv3 + scw_ref[i:i + 1, :] * cv_ref[HALO - 2 + i:HALO - 2 + i + ts, :]
    y_sc = (g_b * conv3).astype(BF16)

    out = _dot(y_ssd, wout_ref[0:SSD_WIDTH, :]) + _dot(y_sc, wout_ref[SSD_WIDTH:, :])
    h_ref[0] = x + out


def _const_spec(shape):
    nd = len(shape)
    return pl.BlockSpec(shape, lambda *_: (0,) * nd)


def _mixer(x, gmix, win, cw, cb, dtb, a_row, dskip, gate, scw, wout, tri):
    bsz, seq, d = x.shape
    ts = SEQ_TILE
    consts = (gmix, win, cw, cb, dtb, a_row, dskip, gate, scw, wout, tri)
    return pl.pallas_call(
        _mixer_kernel,
        grid=(bsz, seq // ts),
        in_specs=[pl.BlockSpec((1, ts, d), lambda b, s: (b, s, 0))] + [_const_spec(c.shape) for c in consts],
        out_specs=pl.BlockSpec((1, ts, d), lambda b, s: (b, s, 0)),
        out_shape=jax.ShapeDtypeStruct((bsz, seq, d), F32),
        scratch_shapes=[
            pltpu.VMEM((ts + HALO, XBC_WIDTH), F32),
            pltpu.VMEM((ts + HALO, SC_WIDTH), F32),
            pltpu.VMEM((ts, XBC_WIDTH), F32),
            pltpu.VMEM((ts, LANES), F32),
            pltpu.VMEM((ts, SSD_WIDTH), F32),
            pltpu.VMEM((SSD_HEADS, SSD_STATE, SSD_HEAD_DIM), F32),
        ],
        compiler_params=pltpu.CompilerParams(
            dimension_semantics=("arbitrary", "arbitrary"), vmem_limit_bytes=VMEM_LIMIT_BYTES),
        name="mixer",
    )(x, *consts)


def _kv_kernel(mem_ref, g_ref, wkv_ref, kt_ref, v_ref):
    d = mem_ref.shape[2]
    mn = _rms(mem_ref[0], g_ref[...]).astype(BF16)
    kv = _dot(mn, wkv_ref[...])
    kt_ref[0] = (kv[:, :d] * (XA_HEAD_DIM ** -0.5)).T.astype(BF16)
    v_ref[0] = kv[:, d:].astype(BF16)


def _memory_kv(mem, g, wkv):
    bsz, m, d = mem.shape
    return pl.pallas_call(
        _kv_kernel,
        grid=(bsz,),
        in_specs=[pl.BlockSpec((1, m, d), lambda b: (b, 0, 0)), _const_spec(g.shape), _const_spec(wkv.shape)],
        out_specs=[pl.BlockSpec((1, d, m), lambda b: (b, 0, 0)), pl.BlockSpec((1, m, d), lambda b: (b, 0, 0))],
        out_shape=[jax.ShapeDtypeStruct((bsz, d, m), BF16), jax.ShapeDtypeStruct((bsz, m, d), BF16)],
        compiler_params=pltpu.CompilerParams(
            dimension_semantics=("arbitrary",), vmem_limit_bytes=VMEM_LIMIT_BYTES),
        name="memory_kv",
    )(mem, g, wkv)


def _xattn_router_kernel(h1_ref, gx_ref, wq_ref, kt_ref, v_ref, wo_ref, gm_ref, wr_ref, br_ref, upper_ref,
                         h2_ref, xp_ref, rinfo_ref, counts_ref, carry_ref):
    ts = h1_ref.shape[1]
    first = jnp.logical_and(pl.program_id(0) == 0, pl.program_id(1) == 0)

    @pl.when(first)
    def _():
        carry_ref[...] = jnp.zeros(carry_ref.shape, F32)

    h1 = h1_ref[0]
    q = _dot(_rms(h1, gx_ref[...]).astype(BF16), wq_ref[...]).astype(BF16)
    outs = []
    for h in range(XA_HEADS):
        hs = slice(h * XA_HEAD_DIM, (h + 1) * XA_HEAD_DIM)
        sc = _dot(q[:, hs], kt_ref[0, hs, :])
        p = jnp.exp(sc - jnp.max(sc, axis=-1, keepdims=True))
        o = _dot(p.astype(BF16), v_ref[0, :, hs]) / jnp.sum(p, axis=-1, keepdims=True)
        outs.append(o.astype(BF16))
    h2 = h1 + _dot(jnp.concatenate(outs, axis=-1), wo_ref[...])
    h2_ref[0] = h2

    n3 = _rms(h2, gm_ref[...])
    n3b = n3.astype(BF16)
    half = n3.shape[1] // 2
    n3r = n3b.astype(F32)
    lo = lax.shift_right_logical(pltpu.bitcast(n3r[:, :half], jnp.uint32), jnp.uint32(16))
    hi = pltpu.bitcast(n3r[:, half:], jnp.uint32) & jnp.uint32(0xFFFF0000)
    xp_ref[...] = lo | hi

    n3_lo = (n3 - n3r).astype(BF16)
    lg2 = _dot(n3b, wr_ref[...])
    logits = lg2[:, :LANES] + lg2[:, LANES:] + _dot(n3_lo, wr_ref[:, :LANES]) + br_ref[...]
    lt = logits.T

    l0, l1, l2, l3 = (lt[ROUTER_GROUP_ROW + i:ROUTER_GROUP_ROW + i + 1, :] for i in range(N_GROUPS_MOE))
    m = jnp.maximum(jnp.maximum(l0, l1), jnp.maximum(l2, l3))
    gidx = jnp.where(l0 == m, 0, jnp.where(l1 == m, 1, jnp.where(l2 == m, 2, 3)))
    g_w = 1.0 / (jnp.exp(l0 - m) + jnp.exp(l1 - m) + jnp.exp(l2 - m) + jnp.exp(l3 - m))

    def group_rows(g):
        r = ROUTER_EXPERT_ROW + g * EXPERTS_PER_GROUP
        return lt[r:r + EXPERTS_PER_GROUP, :]

    el = jnp.where(gidx == 0, group_rows(0),
                   jnp.where(gidx == 1, group_rows(1), jnp.where(gidx == 2, group_rows(2), group_rows(3))))
    sub = lax.broadcasted_iota(jnp.int32, el.shape, 0)
    m1 = jnp.max(el, axis=0, keepdims=True)
    i1 = jnp.min(jnp.where(el == m1, sub, EXPERTS_PER_GROUP), axis=0, keepdims=True)
    el2 = jnp.where(sub == i1, -jnp.inf, el)
    m2 = jnp.max(el2, axis=0, keepdims=True)
    i2 = jnp.min(jnp.where(el2 == m2, sub, EXPERTS_PER_GROUP), axis=0, keepdims=True)
    e2 = jnp.exp(m2 - m1)
    w1 = 1.0 / (1.0 + e2)
    w2 = e2 / (1.0 + e2)
    e_a = gidx * EXPERTS_PER_GROUP + i1
    e_b = gidx * EXPERTS_PER_GROUP + i2

    eio = lax.broadcasted_iota(jnp.int32, (N_EXPERTS, ts), 0)
    oh_a = (eio == e_a).astype(F32)
    oh_b = (eio == e_b).astype(F32)
    prefix = _dot(jnp.concatenate([oh_a, oh_b], axis=0).astype(BF16), upper_ref[...])
    tot_a = jnp.sum(oh_a, axis=1, keepdims=True)
    tot_b = jnp.sum(oh_b, axis=1, keepdims=True)
    carry = carry_ref[:, 0:1]
    rank_a = jnp.sum(oh_a * (carry + prefix[:N_EXPERTS]), axis=0, keepdims=True)
    rank_b = jnp.sum(oh_b * (carry + tot_a + prefix[N_EXPERTS:]), axis=0, keepdims=True)
    new_carry = jnp.broadcast_to(carry + tot_a + tot_b, carry_ref.shape)
    carry_ref[...] = new_carry
    counts_ref[...] = new_carry

    res = jnp.concatenate(
        [e_a.astype(F32), e_b.astype(F32), rank_a, rank_b, g_w * w1, g_w * w2,
         jnp.zeros((LANES - 6, ts), F32)], axis=0)
    rinfo_ref[...] = res.T


def _xattn_router(h1, gx, wq, kt, v, wo, gm, wr, br, upper):
    bsz, seq, d = h1.shape
    ts = SEQ_TILE
    m = v.shape[1]
    nt = seq // ts
    tok_spec = pl.BlockSpec((1, ts, d), lambda b, s: (b, s, 0))
    return pl.pallas_call(
        _xattn_router_kernel,
        grid=(bsz, nt),
        in_specs=[tok_spec, _const_spec(gx.shape), _const_spec(wq.shape),
                  pl.BlockSpec((1, d, m), lambda b, s: (b, 0, 0)),
                  pl.BlockSpec((1, m, d), lambda b, s: (b, 0, 0)),
                  _const_spec(wo.shape), _const_spec(gm.shape), _const_spec(wr.shape), _const_spec(br.shape),
                  _const_spec(upper.shape)],
        out_specs=[tok_spec,
                   pl.BlockSpec((ts, d // 2), lambda b, s: (b * nt + s, 0)),
                   pl.BlockSpec((ts, LANES), lambda b, s: (b * nt + s, 0)),
                   pl.BlockSpec((N_EXPERTS, LANES), lambda b, s: (0, 0))],
        out_shape=[jax.ShapeDtypeStruct((bsz, seq, d), F32),
                   jax.ShapeDtypeStruct((bsz * seq, d // 2), jnp.uint32),
                   jax.ShapeDtypeStruct((bsz * seq, LANES), F32),
                   jax.ShapeDtypeStruct((N_EXPERTS, LANES), F32)],
        scratch_shapes=[pltpu.VMEM((N_EXPERTS, LANES), F32)],
        compiler_params=pltpu.CompilerParams(
            dimension_semantics=("arbitrary", "arbitrary"), vmem_limit_bytes=VMEM_LIMIT_BYTES),
        name="xattn_router",
    )(h1, gx, wq, kt, v, wo, gm, wr, br, upper)


def _dispatch_kernel(dest_ref, xp_ref, buf_in_ref, buf_ref, sem):
    del buf_in_ref
    tr = xp_ref.shape[0]

    def row_copy(r, k):
        d = dest_ref[0, 0, 2 * r + k]
        return pltpu.make_async_copy(xp_ref.at[pl.ds(r, 1)], buf_ref.at[pl.ds(d, 1)], sem)

    def start(r, c):
        row_copy(r, 0).start()
        row_copy(r, 1).start()
        return c

    def wait(r, c):
        row_copy(r, 0).wait()
        row_copy(r, 1).wait()
        return c

    lax.fori_loop(0, tr, start, 0, unroll=8)
    lax.fori_loop(0, tr, wait, 0, unroll=8)


def _dispatch(dest, xp, n_rows):
    t, w = xp.shape
    tr = ROW_TILE
    buf0 = jnp.zeros((n_rows, w), jnp.uint32)
    return pl.pallas_call(
        _dispatch_kernel,
        grid=(t // tr,),
        in_specs=[pl.BlockSpec((1, 1, 2 * tr), lambda i: (i, 0, 0), memory_space=pltpu.SMEM),
                  pl.BlockSpec((tr, w), lambda i: (i, 0)),
                  pl.BlockSpec(memory_space=pl.ANY)],
        out_specs=pl.BlockSpec(memory_space=pl.ANY),
        out_shape=jax.ShapeDtypeStruct((n_rows, w), jnp.uint32),
        scratch_shapes=[pltpu.SemaphoreType.DMA],
        input_output_aliases={2: 0},
        compiler_params=pltpu.CompilerParams(
            dimension_semantics=("arbitrary",), vmem_limit_bytes=VMEM_LIMIT_BYTES),
        name="dispatch",
    )(dest.reshape(t // tr, 1, 2 * tr), xp, buf0)


def _expert_kernel(be_ref, nused_ref, xb_ref, wgu_ref, wd_ref, y_ref):
    del be_ref
    i = pl.program_id(0)

    @pl.when(i < nused_ref[0])
    def _():
        w = xb_ref[...]
        x_lo = pltpu.bitcast(lax.shift_left(w, jnp.uint32(16)), F32).astype(BF16)
        x_hi = pltpu.bitcast(w & jnp.uint32(0xFFFF0000), F32).astype(BF16)
        half = w.shape[1]
        gu = _dot(x_lo, wgu_ref[0, :half, :]) + _dot(x_hi, wgu_ref[0, half:, :])
        mid = (_silu(gu[:, :D_EXPERT]) * gu[:, D_EXPERT:]).astype(BF16)
        y_ref[...] = _dot(mid, wd_ref[0])

    @pl.when(i >= nused_ref[0])
    def _():
        y_ref[...] = jnp.zeros(y_ref.shape, F32)


def _experts(block_expert, n_used, xbuf, wgu, wd):
    n_rows, w = xbuf.shape
    bm = EXPERT_BLOCK
    d = wd.shape[2]
    grid_spec = pltpu.PrefetchScalarGridSpec(
        num_scalar_prefetch=2,
        grid=(n_rows // bm,),
        in_specs=[pl.BlockSpec((bm, w), lambda i, be, nu: (i, 0)),
                  pl.BlockSpec((1,) + wgu.shape[1:], lambda i, be, nu: (be[i], 0, 0)),
                  pl.BlockSpec((1,) + wd.shape[1:], lambda i, be, nu: (be[i], 0, 0))],
        out_specs=pl.BlockSpec((bm, d), lambda i, be, nu: (i, 0)),
    )
    return pl.pallas_call(
        _expert_kernel,
        grid_spec=grid_spec,
        out_shape=jax.ShapeDtypeStruct((n_rows, d), F32),
        compiler_params=pltpu.CompilerParams(
            dimension_semantics=("arbitrary",), vmem_limit_bytes=VMEM_LIMIT_BYTES),
        name="experts",
    )(block_expert, n_used, xbuf, wgu, wd)


def _combine_kernel(dest_ref, h2_ref, rinfo_ref, gf_ref, ybuf_ref, out_ref, ya_ref, yb_ref, sem):
    tr = h2_ref.shape[0]

    def row_copy(r, k):
        d = dest_ref[0, 0, 2 * r + k]
        dst = ya_ref if k == 0 else yb_ref
        return pltpu.make_async_copy(ybuf_ref.at[pl.ds(d, 1)], dst.at[pl.ds(r, 1)], sem)

    def start(r, c):
        row_copy(r, 0).start()
        row_copy(r, 1).start()
        return c

    def wait(r, c):
        row_copy(r, 0).wait()
        row_copy(r, 1).wait()
        return c

    lax.fori_loop(0, tr, start, 0, unroll=8)
    lax.fori_loop(0, tr, wait, 0, unroll=8)
    info = rinfo_ref[...]
    h = h2_ref[...] + info[:, 4:5] * ya_ref[...] + info[:, 5:6] * yb_ref[...]
    out_ref[...] = _rms(h, gf_ref[...])


def _combine(dest, h2, rinfo, gf, ybuf):
    t, d = h2.shape
    tr = ROW_TILE
    return pl.pallas_call(
        _combine_kernel,
        grid=(t // tr,),
        in_specs=[pl.BlockSpec((1, 1, 2 * tr), lambda i: (i, 0, 0), memory_space=pltpu.SMEM),
                  pl.BlockSpec((tr, d), lambda i: (i, 0)),
                  pl.BlockSpec((tr, LANES), lambda i: (i, 0)),
                  _const_spec(gf.shape),
                  pl.BlockSpec(memory_space=pl.ANY)],
        out_specs=pl.BlockSpec((tr, d), lambda i: (i, 0)),
        out_shape=jax.ShapeDtypeStruct((t, d), F32),
        scratch_shapes=[pltpu.VMEM((tr, d), F32), pltpu.VMEM((tr, d), F32), pltpu.SemaphoreType.DMA],
        compiler_params=pltpu.CompilerParams(
            dimension_semantics=("arbitrary",), vmem_limit_bytes=VMEM_LIMIT_BYTES),
        name="combine",
    )(dest.reshape(t // tr, 1, 2 * tr), h2, rinfo, gf, ybuf)


def _pad_lanes(v, fill=0.0):
    return jnp.pad(v, (0, LANES - v.shape[0]), constant_values=fill)[None, :]


def kernel(x, mem, norm_mem, norm_mix, w_in, conv_ssd_w, conv_ssd_b, dt_bias, a_log, d_skip, norm_ssd_gate,
           conv_short_w, w_out, norm_xattn, w_q, w_kv, w_o, norm_moe, w_router_group, b_router_group,
           w_router_expert, b_router_expert, w_gate, w_up, w_down, norm_final):
    bsz, seq, d = x.shape
    t = bsz * seq
    assert w_in.shape[0] == 1, "single-layer model"
    assert d == D_MODEL and seq % SEQ_TILE == 0 and t % ROW_TILE == 0

    tri = jnp.tril(jnp.ones((SSD_CHUNK, SSD_CHUNK), F32))
    upper = jnp.triu(jnp.ones((SEQ_TILE, SEQ_TILE), F32), k=1).astype(BF16)

    o2 = SSD_WIDTH + XBC_WIDTH
    o3 = o2 + SSD_HEADS
    wl = w_in[0]
    win = jnp.concatenate(
        [wl[:, :o2], wl[:, o3:], wl[:, o2:o3], jnp.zeros((d, LANES - SSD_HEADS), F32)], axis=1).astype(BF16)
    h1 = _mixer(x, norm_mix, win, conv_ssd_w[0], conv_ssd_b, _pad_lanes(dt_bias[0]),
                _pad_lanes(-jnp.exp(a_log[0])), jnp.repeat(d_skip[0], SSD_HEAD_DIM)[None, :], norm_ssd_gate,
                conv_short_w[0], w_out[0].astype(BF16), tri)

    kt, v = _memory_kv(mem, norm_mem[None, :], w_kv[0].astype(BF16))
    wr = jnp.zeros((d, LANES), F32)
    wr = wr.at[:, ROUTER_GROUP_ROW:ROUTER_GROUP_ROW + N_GROUPS_MOE].set(w_router_group[0])
    wr = wr.at[:, ROUTER_EXPERT_ROW:ROUTER_EXPERT_ROW + N_EXPERTS].set(w_router_expert[0])
    wr_hi = wr.astype(BF16)
    wr_lo = (wr - wr_hi.astype(F32)).astype(BF16)
    br = jnp.zeros((LANES,), F32)
    br = br.at[ROUTER_GROUP_ROW:ROUTER_GROUP_ROW + N_GROUPS_MOE].set(b_router_group[0])
    br = br.at[ROUTER_EXPERT_ROW:ROUTER_EXPERT_ROW + N_EXPERTS].set(b_router_expert[0])
    h2, xp, rinfo, counts = _xattn_router(
        h1, norm_xattn, w_q[0].astype(BF16), kt, v, w_o[0].astype(BF16), norm_moe,
        jnp.concatenate([wr_hi, wr_lo], axis=1), br[None, :], upper)

    bm = EXPERT_BLOCK
    n_blocks = (2 * t) // bm + N_EXPERTS
    sizes = counts[:, 0].astype(jnp.int32)
    padded = ((sizes + bm - 1) // bm) * bm
    pad_end = jnp.cumsum(padded)
    pad_start = pad_end - padded
    e_ab = rinfo[:, 0:2].astype(jnp.int32)
    dest = (pad_start[e_ab] + rinfo[:, 2:4].astype(jnp.int32)).reshape(-1)
    block_expert = jnp.clip(jnp.searchsorted(pad_end, jnp.arange(n_blocks) * bm, side='right'),
                            0, N_EXPERTS - 1).astype(jnp.int32)
    n_used = (pad_end[-1:] // bm).astype(jnp.int32)

    xbuf = _dispatch(dest, xp, n_blocks * bm)
    wgu = jnp.concatenate([w_gate[0], w_up[0]], axis=-1).astype(BF16)
    ybuf = _experts(block_expert, n_used, xbuf, wgu, w_down[0].astype(BF16))
    out = _combine(dest, h2.reshape(t, d), rinfo, norm_final[None, :], ybuf)
    return out.reshape(bsz, seq, d)
```

```python
import functools

import jax
import jax.numpy as jnp
from jax import lax
from jax.experimental import pallas as pl
from jax.experimental.pallas import tpu as pltpu

F32 = jnp.float32
BF16 = jnp.bfloat16
EPS = 1e-6

LANES = 128
SUBLANES = 8
VMEM_LIMIT_BYTES = 56 * 1024 * 1024

D_MODEL = 1024
SSD_WIDTH = 512
SSD_HEAD_DIM = 64
SSD_HEADS = 8
SSD_GROUPS = 2
SSD_HEADS_PER_GROUP = 4
SSD_STATE = 64
SSD_CHUNK = 128
XBC_WIDTH = SSD_WIDTH + 2 * SSD_GROUPS * SSD_STATE
SC_WIDTH = 512
XA_HEADS = 4
XA_HEAD_DIM = 256
N_GROUPS_MOE = 4
EXPERTS_PER_GROUP = 8
N_EXPERTS = 32
D_EXPERT = 512

COL_Z = 0
COL_XBC = COL_Z + SSD_WIDTH
COL_GB = COL_XBC + XBC_WIDTH
COL_GC = COL_GB + SC_WIDTH
COL_V = COL_GC + SC_WIDTH
COL_DT = COL_V + SC_WIDTH
PROJ_WIDTH = COL_DT + LANES

SEQ_TILE = 512
ROW_TILE = 512
EXPERT_BLOCK = 512
HALO = SUBLANES

ROUTER_GROUP_ROW = 0
ROUTER_EXPERT_ROW = SUBLANES


def _rms(x, g):
    return x * lax.rsqrt(jnp.mean(x * x, axis=-1, keepdims=True) + EPS) * g


def _silu(x):
    return x * jax.nn.sigmoid(x)


def _dot(a, b):
    return jnp.dot(a, b, preferred_element_type=F32)


def _mixer_kernel(x_ref, gmix_ref, win_ref, cw_ref, cb_ref, dtb_ref, a_ref, dskip_ref, gate_ref,
                  scw_ref, wout_ref, tri_ref, h_ref,
                  conv_ref, cv_ref, act_ref, dt_ref, y_ref, state_ref, sinc_ref, loff_ref, elast_ref):
    ts = x_ref.shape[1]
    s = pl.program_id(1)
    x = x_ref[0]
    nb = _rms(x, gmix_ref[...]).astype(BF16)

    @pl.when(s == 0)
    def _():
        conv_ref[0:HALO, :] = jnp.zeros((HALO, XBC_WIDTH), F32)
        cv_ref[0:HALO, :] = jnp.zeros((HALO, SC_WIDTH), F32)
        state_ref[...] = jnp.zeros(state_ref.shape, F32)

    @pl.when(s > 0)
    def _():
        conv_ref[0:HALO, :] = conv_ref[ts:ts + HALO, :]
        cv_ref[0:HALO, :] = cv_ref[ts:ts + HALO, :]

    xbc = _dot(nb, win_ref[:, COL_XBC:COL_XBC + XBC_WIDTH])
    conv_ref[HALO:HALO + ts, :] = xbc
    acc = cb_ref[...] + cw_ref[3:4, :] * xbc
    for i in range(3):
        acc = acc + cw_ref[i:i + 1, :] * conv_ref[HALO - 3 + i:HALO - 3 + i + ts, :]
    act_ref[...] = _silu(acc)

    dtr = _dot(nb, win_ref[:, COL_DT:COL_DT + LANES]) + dtb_ref[...]
    dt_ref[...] = jnp.maximum(dtr, 0.0) + jnp.log1p(jnp.exp(-jnp.abs(dtr)))

    row = lax.broadcasted_iota(jnp.int32, (SSD_CHUNK, SSD_CHUNK), 0)
    col = lax.broadcasted_iota(jnp.int32, (SSD_CHUNK, SSD_CHUNK), 1)
    causal = row >= col
    lo_q = col < SSD_HEAD_DIM
    lo_n = lax.broadcasted_iota(jnp.int32, (SSD_STATE, SSD_CHUNK), 1) < SSD_HEAD_DIM
    lo_e = lax.broadcasted_iota(jnp.int32, (SUBLANES, SSD_CHUNK), 1) < SSD_HEAD_DIM

    def chunk_body(c, carry):
        rows = slice(c * SSD_CHUNK, (c + 1) * SSD_CHUNK)
        dt = dt_ref[rows, :]
        adt = dt * a_ref[...]
        cs = jnp.dot(tri_ref[...], adt, preferred_element_type=F32,
                     precision=lax.Precision.HIGHEST)
        cs_t = cs.T
        dt_t = dt.T
        bm = act_ref[rows, SSD_WIDTH:SSD_WIDTH + SSD_GROUPS * SSD_STATE]
        cm = act_ref[rows, SSD_WIDTH + SSD_GROUPS * SSD_STATE:XBC_WIDTH]
        bm_t = bm.T
        cm_sw = pltpu.roll(cm, SSD_STATE, axis=1)
        for g in range(SSD_GROUPS):
            cg = cm[:, g * SSD_STATE:(g + 1) * SSD_STATE]
            bg = bm[:, g * SSD_STATE:(g + 1) * SSD_STATE]
            bg_t = bm_t[g * SSD_STATE:(g + 1) * SSD_STATE, :]
            cb = lax.dot_general(cg.astype(BF16), bg.astype(BF16), (((1,), (1,)), ((), ())),
                                 preferred_element_type=F32)
            c2 = jnp.where(lo_q, cm, cm_sw) if g == 0 else jnp.where(lo_q, cm_sw, cm)
            for jj in range(SSD_HEADS_PER_GROUP // 2):
                j = g * (SSD_HEADS_PER_GROUP // 2) + jj
                idx = c * (SSD_HEADS // 2) + j
                xp = act_ref[rows, j * LANES:(j + 1) * LANES]
                rhs = jnp.concatenate(
                    [jnp.where(lo_q, xp, 0.0), jnp.where(lo_q, 0.0, xp)], axis=0).astype(BF16)
                diag, off, w_rows, e_last = [], [], [], []
                for h in (2 * j, 2 * j + 1):
                    cs_col = jnp.broadcast_to(cs[:, h:h + 1], (SSD_CHUNK, SSD_CHUNK))
                    cs_row = cs_t[h:h + 1, :]
                    dt_row = dt_t[h:h + 1, :]
                    decay = jnp.exp(jnp.where(causal, cs_col - cs_row, -jnp.inf))
                    diag.append((cb * dt_row) * decay)
                    off.append(jnp.exp(cs_col))
                    cl = cs_t[h:h + 1, SSD_CHUNK - 1:SSD_CHUNK]
                    w_rows.append(jnp.exp(cl - cs_row) * dt_row)
                    e_last.append(jnp.exp(cl))
                y_ref[rows, j * LANES:(j + 1) * LANES] = _dot(jnp.concatenate(diag, axis=1).astype(BF16), rhs)
                lhs2 = jnp.concatenate([bg_t * w_rows[0], bg_t * w_rows[1]], axis=1).astype(BF16)
                sinc_ref[idx] = _dot(lhs2, rhs)
                loff_ref[idx] = (c2 * jnp.where(lo_q, off[0], off[1])).astype(BF16)
                elast_ref[idx] = jnp.where(lo_e, e_last[0], e_last[1])
        return carry

    n_chunks = ts // SSD_CHUNK
    for c in range(n_chunks):
        chunk_body(c, 0)

    for c in range(n_chunks):
        for j in range(SSD_HEADS // 2):
            idx = c * (SSD_HEADS // 2) + j
            stp = state_ref[j]
            rhs_st = jnp.concatenate(
                [jnp.where(lo_n, stp, 0.0), jnp.where(lo_n, 0.0, stp)], axis=0).astype(BF16)
            y_ref[c * SSD_CHUNK:(c + 1) * SSD_CHUNK, j * LANES:(j + 1) * LANES] += _dot(loff_ref[idx], rhs_st)
            state_ref[j] = elast_ref[idx][0:1, :] * stp + sinc_ref[idx]

    z = _dot(nb, win_ref[:, COL_Z:COL_Z + SSD_WIDTH])
    y = y_ref[...] + act_ref[:, 0:SSD_WIDTH] * dskip_ref[...]
    gated = y * _silu(z)
    gw = SSD_WIDTH // SSD_GROUPS
    y_ssd = jnp.concatenate(
        [_rms(gated[:, g * gw:(g + 1) * gw], gate_ref[:, g * gw:(g + 1) * gw]) for g in range(SSD_GROUPS)],
        axis=-1).astype(BF16)

    g_b = _dot(nb, win_ref[:, COL_GB:COL_GB + SC_WIDTH])
    g_c = _dot(nb, win_ref[:, COL_GC:COL_GC + SC_WIDTH])
    v = _dot(nb, win_ref[:, COL_V:COL_V + SC_WIDTH])
    u = g_c * v
    cv_ref[HALO:HALO + ts, :] = u
    conv3 = scw_ref[2:3, :] * u
    for i in range(2):
        conv3 = conv3 + scw_ref[i:i + 1, :] * cv_ref[HALO - 2 + i:HALO - 2 + i + ts, :]
    y_sc = (g_b * conv3).astype(BF16)

    out = _dot(y_ssd, wout_ref[0:SSD_WIDTH, :]) + _dot(y_sc, wout_ref[SSD_WIDTH:, :])
    h_ref[0] = x + out


def _const_spec(shape):
    nd = len(shape)
    return pl.BlockSpec(shape, lambda *_: (0,) * nd)


def _mixer(x, gmix, win, cw, cb, dtb, a_row, dskip, gate, scw, wout, tri):
    bsz, seq, d = x.shape
    ts = SEQ_TILE
    consts = (gmix, win, cw, cb, dtb, a_row, dskip, gate, scw, wout, tri)
    return pl.pallas_call(
        _mixer_kernel,
        grid=(bsz, seq // ts),
        in_specs=[pl.BlockSpec((1, ts, d), lambda b, s: (b, s, 0))] + [_const_spec(c.shape) for c in consts],
        out_specs=pl.BlockSpec((1, ts, d), lambda b, s: (b, s, 0)),
        out_shape=jax.ShapeDtypeStruct((bsz, seq, d), F32),
        scratch_shapes=[
            pltpu.VMEM((ts + HALO, XBC_WIDTH), F32),
            pltpu.VMEM((ts + HALO, SC_WIDTH), F32),
            pltpu.VMEM((ts, XBC_WIDTH), F32),
            pltpu.VMEM((ts, LANES), F32),
            pltpu.VMEM((ts, SSD_WIDTH), F32),
            pltpu.VMEM((SSD_HEADS // 2, SSD_STATE, 2 * SSD_HEAD_DIM), F32),
            pltpu.VMEM((ts // SSD_CHUNK * SSD_HEADS // 2, SSD_STATE, 2 * SSD_HEAD_DIM), F32),
            pltpu.VMEM((ts // SSD_CHUNK * SSD_HEADS // 2, SSD_CHUNK, 2 * SSD_STATE), BF16),
            pltpu.VMEM((ts // SSD_CHUNK * SSD_HEADS // 2, SUBLANES, 2 * SSD_HEAD_DIM), F32),
        ],
        compiler_params=pltpu.CompilerParams(
            dimension_semantics=("arbitrary", "arbitrary"), vmem_limit_bytes=VMEM_LIMIT_BYTES),
        name="mixer",
    )(x, *consts)


def _kv_kernel(mem_ref, g_ref, wkv_ref, kt_ref, v_ref):
    d = mem_ref.shape[2]
    mn = _rms(mem_ref[0], g_ref[...]).astype(BF16)
    kv = _dot(mn, wkv_ref[...])
    kt_ref[0] = (kv[:, :d] * (XA_HEAD_DIM ** -0.5)).T.astype(BF16)
    v_ref[0] = kv[:, d:].astype(BF16)


def _memory_kv(mem, g, wkv):
    bsz, m, d = mem.shape
    return pl.pallas_call(
        _kv_kernel,
        grid=(bsz,),
        in_specs=[pl.BlockSpec((1, m, d), lambda b: (b, 0, 0)), _const_spec(g.shape), _const_spec(wkv.shape)],
        out_specs=[pl.BlockSpec((1, d, m), lambda b: (b, 0, 0)), pl.BlockSpec((1, m, d), lambda b: (b, 0, 0))],
        out_shape=[jax.ShapeDtypeStruct((bsz, d, m), BF16), jax.ShapeDtypeStruct((bsz, m, d), BF16)],
        compiler_params=pltpu.CompilerParams(
            dimension_semantics=("arbitrary",), vmem_limit_bytes=VMEM_LIMIT_BYTES),
        name="memory_kv",
    )(mem, g, wkv)


def _xattn_router_kernel(h1_ref, gx_ref, wq_ref, kt_ref, v_ref, wo_ref, gm_ref, wr_ref, br_ref, upper_ref,
                         h2_ref, xp_ref, rinfo_ref, counts_ref, zero_ref, carry_ref):
    ts = h1_ref.shape[1]
    first = jnp.logical_and(pl.program_id(0) == 0, pl.program_id(1) == 0)

    @pl.when(first)
    def _():
        carry_ref[...] = jnp.zeros(carry_ref.shape, F32)

    h1 = h1_ref[0]
    q = _dot(_rms(h1, gx_ref[...]).astype(BF16), wq_ref[...]).astype(BF16)
    outs = []
    for h in range(XA_HEADS):
        hs = slice(h * XA_HEAD_DIM, (h + 1) * XA_HEAD_DIM)
        sc = _dot(q[:, hs], kt_ref[0, hs, :])
        p = jnp.exp(sc - jnp.max(sc, axis=-1, keepdims=True))
        o = _dot(p.astype(BF16), v_ref[0, :, hs]) / jnp.sum(p, axis=-1, keepdims=True)
        outs.append(o.astype(BF16))
    h2 = h1 + _dot(jnp.concatenate(outs, axis=-1), wo_ref[...])
    h2_ref[0] = h2

    n3 = _rms(h2, gm_ref[...])
    n3b = n3.astype(BF16)
    half = n3.shape[1] // 2
    n3r = n3b.astype(F32)
    lo = lax.shift_right_logical(pltpu.bitcast(n3r[:, :half], jnp.uint32), jnp.uint32(16))
    hi = pltpu.bitcast(n3r[:, half:], jnp.uint32) & jnp.uint32(0xFFFF0000)
    xp_ref[...] = lo | hi
    zero_ref[...] = jnp.zeros(zero_ref.shape, zero_ref.dtype)

    n3_lo = (n3 - n3r).astype(BF16)
    lg2 = _dot(n3b, wr_ref[...])
    logits = lg2[:, :LANES] + lg2[:, LANES:] + _dot(n3_lo, wr_ref[:, :LANES]) + br_ref[...]
    lt = logits.T

    l0, l1, l2, l3 = (lt[ROUTER_GROUP_ROW + i:ROUTER_GROUP_ROW + i + 1, :] for i in range(N_GROUPS_MOE))
    m = jnp.maximum(jnp.maximum(l0, l1), jnp.maximum(l2, l3))
    gidx = jnp.where(l0 == m, 0, jnp.where(l1 == m, 1, jnp.where(l2 == m, 2, 3)))
    g_w = 1.0 / (jnp.exp(l0 - m) + jnp.exp(l1 - m) + jnp.exp(l2 - m) + jnp.exp(l3 - m))

    def group_rows(g):
        r = ROUTER_EXPERT_ROW + g * EXPERTS_PER_GROUP
        return lt[r:r + EXPERTS_PER_GROUP, :]

    el = jnp.where(gidx == 0, group_rows(0),
                   jnp.where(gidx == 1, group_rows(1), jnp.where(gidx == 2, group_rows(2), group_rows(3))))
    sub = lax.broadcasted_iota(jnp.int32, el.shape, 0)
    m1 = jnp.max(el, axis=0, keepdims=True)
    i1 = jnp.min(jnp.where(el == m1, sub, EXPERTS_PER_GROUP), axis=0, keepdims=True)
    el2 = jnp.where(sub == i1, -jnp.inf, el)
    m2 = jnp.max(el2, axis=0, keepdims=True)
    i2 = jnp.min(jnp.where(el2 == m2, sub, EXPERTS_PER_GROUP), axis=0, keepdims=True)
    e2 = jnp.exp(m2 - m1)
    w1 = 1.0 / (1.0 + e2)
    w2 = e2 / (1.0 + e2)
    e_a = gidx * EXPERTS_PER_GROUP + i1
    e_b = gidx * EXPERTS_PER_GROUP + i2

    eio = lax.broadcasted_iota(jnp.int32, (N_EXPERTS, ts), 0)
    oh_a = (eio == e_a).astype(F32)
    oh_b = (eio == e_b).astype(F32)
    prefix = _dot(jnp.concatenate([oh_a, oh_b], axis=0).astype(BF16), upper_ref[...])
    tot_a = jnp.sum(oh_a, axis=1, keepdims=True)
    tot_b = jnp.sum(oh_b, axis=1, keepdims=True)
    carry = carry_ref[:, 0:1]
    rank_a = jnp.sum(oh_a * (carry + prefix[:N_EXPERTS]), axis=0, keepdims=True)
    rank_b = jnp.sum(oh_b * (carry + tot_a + prefix[N_EXPERTS:]), axis=0, keepdims=True)
    new_carry = jnp.broadcast_to(carry + tot_a + tot_b, carry_ref.shape)
    carry_ref[...] = new_carry
    counts_ref[...] = new_carry

    res = jnp.concatenate(
        [e_a.astype(F32), e_b.astype(F32), rank_a, rank_b, g_w * w1, g_w * w2,
         jnp.zeros((LANES - 6, ts), F32)], axis=0)
    rinfo_ref[...] = res.T


def _xattn_router(h1, gx, wq, kt, v, wo, gm, wr, br, upper, n_rows):
    bsz, seq, d = h1.shape
    ts = SEQ_TILE
    m = v.shape[1]
    nt = seq // ts
    zero_rows = n_rows // (bsz * nt)
    assert zero_rows * bsz * nt == n_rows and zero_rows % SUBLANES == 0
    tok_spec = pl.BlockSpec((1, ts, d), lambda b, s: (b, s, 0))
    return pl.pallas_call(
        _xattn_router_kernel,
        grid=(bsz, nt),
        in_specs=[tok_spec, _const_spec(gx.shape), _const_spec(wq.shape),
                  pl.BlockSpec((1, d, m), lambda b, s: (b, 0, 0)),
                  pl.BlockSpec((1, m, d), lambda b, s: (b, 0, 0)),
                  _const_spec(wo.shape), _const_spec(gm.shape), _const_spec(wr.shape), _const_spec(br.shape),
                  _const_spec(upper.shape)],
        out_specs=[tok_spec,
                   pl.BlockSpec((ts, d // 2), lambda b, s: (b * nt + s, 0)),
                   pl.BlockSpec((ts, LANES), lambda b, s: (b * nt + s, 0)),
                   pl.BlockSpec((N_EXPERTS, LANES), lambda b, s: (0, 0)),
                   pl.BlockSpec((zero_rows, d // 2), lambda b, s: (b * nt + s, 0))],
        out_shape=[jax.ShapeDtypeStruct((bsz, seq, d), F32),
                   jax.ShapeDtypeStruct((bsz * seq, d // 2), jnp.uint32),
                   jax.ShapeDtypeStruct((bsz * seq, LANES), F32),
                   jax.ShapeDtypeStruct((N_EXPERTS, LANES), F32),
                   jax.ShapeDtypeStruct((n_rows, d // 2), jnp.uint32)],
        scratch_shapes=[pltpu.VMEM((N_EXPERTS, LANES), F32)],
        compiler_params=pltpu.CompilerParams(
            dimension_semantics=("arbitrary", "arbitrary"), vmem_limit_bytes=VMEM_LIMIT_BYTES),
        name="xattn_router",
    )(h1, gx, wq, kt, v, wo, gm, wr, br, upper)


def _dispatch_kernel(dest_ref, xp_ref, buf_in_ref, buf_ref, sem):
    del buf_in_ref
    tr = xp_ref.shape[0]

    def row_copy(r, k):
        d = dest_ref[0, 0, 2 * r + k]
        return pltpu.make_async_copy(xp_ref.at[pl.ds(r, 1)], buf_ref.at[pl.ds(d, 1)], sem)

    def start(r, c):
        row_copy(r, 0).start(priority=0)
        row_copy(r, 1).start(priority=1)
        return c

    def wait(r, c):
        row_copy(r, 0).wait()
        row_copy(r, 1).wait()
        return c

    lax.fori_loop(0, tr, start, 0, unroll=8)
    lax.fori_loop(0, tr, wait, 0, unroll=8)


def _dispatch(dest, xp, buf0):
    t, w = xp.shape
    tr = ROW_TILE
    return pl.pallas_call(
        _dispatch_kernel,
        grid=(t // tr,),
        in_specs=[pl.BlockSpec((1, 1, 2 * tr), lambda i: (i, 0, 0), memory_space=pltpu.SMEM),
                  pl.BlockSpec((tr, w), lambda i: (i, 0)),
                  pl.BlockSpec(memory_space=pl.ANY)],
        out_specs=pl.BlockSpec(memory_space=pl.ANY),
        out_shape=jax.ShapeDtypeStruct(buf0.shape, buf0.dtype),
        scratch_shapes=[pltpu.SemaphoreType.DMA],
        input_output_aliases={2: 0},
        compiler_params=pltpu.CompilerParams(
            dimension_semantics=("arbitrary",), vmem_limit_bytes=VMEM_LIMIT_BYTES),
        name="dispatch",
    )(dest.reshape(t // tr, 1, 2 * tr), xp, buf0)


def _expert_kernel(be_ref, nused_ref, xb_ref, wgu_ref, wd_ref, y_ref):
    del be_ref
    i = pl.program_id(0)

    @pl.when(i < nused_ref[0])
    def _():
        w = xb_ref[...]
        x_lo = pltpu.bitcast(lax.shift_left(w, jnp.uint32(16)), F32).astype(BF16)
        x_hi = pltpu.bitcast(w & jnp.uint32(0xFFFF0000), F32).astype(BF16)
        half = w.shape[1]
        gu = _dot(x_lo, wgu_ref[0, :half, :]) + _dot(x_hi, wgu_ref[0, half:, :])
        mid = (_silu(gu[:, :D_EXPERT]) * gu[:, D_EXPERT:]).astype(BF16)
        y_ref[...] = _dot(mid, wd_ref[0])

    @pl.when(i >= nused_ref[0])
    def _():
        y_ref[...] = jnp.zeros(y_ref.shape, F32)


def _experts(block_expert, n_used, xbuf, wgu, wd):
    w = xbuf.shape[1]
    bm = EXPERT_BLOCK
    n_blocks = block_expert.shape[0]
    n_rows = n_blocks * bm
    d = wd.shape[2]
    grid_spec = pltpu.PrefetchScalarGridSpec(
        num_scalar_prefetch=2,
        grid=(n_blocks,),
        in_specs=[pl.BlockSpec((bm, w), lambda i, be, nu: (i, 0)),
                  pl.BlockSpec((1,) + wgu.shape[1:], lambda i, be, nu: (be[i], 0, 0)),
                  pl.BlockSpec((1,) + wd.shape[1:], lambda i, be, nu: (be[i], 0, 0))],
        out_specs=pl.BlockSpec((bm, d), lambda i, be, nu: (i, 0)),
    )
    return pl.pallas_call(
        _expert_kernel,
        grid_spec=grid_spec,
        out_shape=jax.ShapeDtypeStruct((n_rows, d), F32),
        compiler_params=pltpu.CompilerParams(
            dimension_semantics=("arbitrary",), vmem_limit_bytes=VMEM_LIMIT_BYTES),
        name="experts",
    )(block_expert, n_used, xbuf, wgu, wd)


def _combine_kernel(dest_ref, dest_next_ref, h2_ref, rinfo_ref, gf_ref, ybuf_ref, out_ref, ya_ref, yb_ref, sem):
    tr = h2_ref.shape[0]
    i = pl.program_id(0)
    slot = i % 2

    def row_copy(dref, s, r, k):
        d = dref[0, 0, 2 * r + k]
        dst = ya_ref if k == 0 else yb_ref
        return pltpu.make_async_copy(ybuf_ref.at[pl.ds(d, 1)], dst.at[s, pl.ds(r, 1)], sem.at[s])

    def gather(dref, s):
        def start(r, c):
            row_copy(dref, s, r, 0).start(priority=0)
            row_copy(dref, s, r, 1).start(priority=1)
            return c
        lax.fori_loop(0, tr, start, 0, unroll=8)

    @pl.when(i == 0)
    def _():
        gather(dest_ref, 0)

    @pl.when(i + 1 < pl.num_programs(0))
    def _():
        gather(dest_next_ref, 1 - slot)

    def wait(r, c):
        row_copy(dest_ref, slot, r, 0).wait()
        row_copy(dest_ref, slot, r, 1).wait()
        return c

    lax.fori_loop(0, tr, wait, 0, unroll=8)
    info = rinfo_ref[...]
    h = h2_ref[...] + info[:, 4:5] * ya_ref[slot] + info[:, 5:6] * yb_ref[slot]
    out_ref[...] = _rms(h, gf_ref[...])


def _combine(dest, h2, rinfo, gf, ybuf):
    t, d = h2.shape
    tr = ROW_TILE
    n_tiles = t // tr
    dest_tiles = dest.reshape(n_tiles, 1, 2 * tr)
    return pl.pallas_call(
        _combine_kernel,
        grid=(n_tiles,),
        in_specs=[pl.BlockSpec((1, 1, 2 * tr), lambda i: (i, 0, 0), memory_space=pltpu.SMEM),
                  pl.BlockSpec((1, 1, 2 * tr), lambda i: (jnp.minimum(i + 1, n_tiles - 1), 0, 0),
                               memory_space=pltpu.SMEM),
                  pl.BlockSpec((tr, d), lambda i: (i, 0)),
                  pl.BlockSpec((tr, LANES), lambda i: (i, 0)),
                  _const_spec(gf.shape),
                  pl.BlockSpec(memory_space=pl.ANY)],
        out_specs=pl.BlockSpec((tr, d), lambda i: (i, 0)),
        out_shape=jax.ShapeDtypeStruct((t, d), F32),
        scratch_shapes=[pltpu.VMEM((2, tr, d), F32), pltpu.VMEM((2, tr, d), F32),
                        pltpu.SemaphoreType.DMA((2,))],
        compiler_params=pltpu.CompilerParams(
            dimension_semantics=("arbitrary",), vmem_limit_bytes=VMEM_LIMIT_BYTES),
        name="combine",
    )(dest_tiles, dest_tiles, h2, rinfo, gf, ybuf)


def _pad_lanes(v, fill=0.0):
    return jnp.pad(v, (0, LANES - v.shape[0]), constant_values=fill)[None, :]


def kernel(x, mem, norm_mem, norm_mix, w_in, conv_ssd_w, conv_ssd_b, dt_bias, a_log, d_skip, norm_ssd_gate,
           conv_short_w, w_out, norm_xattn, w_q, w_kv, w_o, norm_moe, w_router_group, b_router_group,
           w_router_expert, b_router_expert, w_gate, w_up, w_down, norm_final):
    bsz, seq, d = x.shape
    t = bsz * seq
    assert w_in.shape[0] == 1, "single-layer model"
    assert d == D_MODEL and seq % SEQ_TILE == 0 and t % ROW_TILE == 0

    tri = jnp.tril(jnp.ones((SSD_CHUNK, SSD_CHUNK), F32))
    upper = jnp.triu(jnp.ones((SEQ_TILE, SEQ_TILE), F32), k=1).astype(BF16)

    o2 = SSD_WIDTH + XBC_WIDTH
    o3 = o2 + SSD_HEADS
    wl = w_in[0]
    win = jnp.concatenate(
        [wl[:, :o2], wl[:, o3:], wl[:, o2:o3], jnp.zeros((d, LANES - SSD_HEADS), F32)], axis=1).astype(BF16)
    h1 = _mixer(x, norm_mix, win, conv_ssd_w[0], conv_ssd_b, _pad_lanes(dt_bias[0]),
                _pad_lanes(-jnp.exp(a_log[0])), jnp.repeat(d_skip[0], SSD_HEAD_DIM)[None, :], norm_ssd_gate,
                conv_short_w[0], w_out[0].astype(BF16), tri)

    kt, v = _memory_kv(mem, norm_mem[None, :], w_kv[0].astype(BF16))
    wr = jnp.zeros((d, LANES), F32)
    wr = wr.at[:, ROUTER_GROUP_ROW:ROUTER_GROUP_ROW + N_GROUPS_MOE].set(w_router_group[0])
    wr = wr.at[:, ROUTER_EXPERT_ROW:ROUTER_EXPERT_ROW + N_EXPERTS].set(w_router_expert[0])
    wr_hi = wr.astype(BF16)
    wr_lo = (wr - wr_hi.astype(F32)).astype(BF16)
    br = jnp.zeros((LANES,), F32)
    br = br.at[ROUTER_GROUP_ROW:ROUTER_GROUP_ROW + N_GROUPS_MOE].set(b_router_group[0])
    br = br.at[ROUTER_EXPERT_ROW:ROUTER_EXPERT_ROW + N_EXPERTS].set(b_router_expert[0])
    bm = EXPERT_BLOCK
    n_blocks = (2 * t) // bm + N_EXPERTS
    h2, xp, rinfo, counts, xbuf0 = _xattn_router(
        h1, norm_xattn, w_q[0].astype(BF16), kt, v, w_o[0].astype(BF16), norm_moe,
        jnp.concatenate([wr_hi, wr_lo], axis=1), br[None, :], upper, n_blocks * bm)

    sizes = counts[:, 0].astype(jnp.int32)
    padded = ((sizes + bm - 1) // bm) * bm
    pad_end = jnp.cumsum(padded)
    pad_start = pad_end - padded
    experts = jnp.arange(N_EXPERTS, dtype=jnp.int32)
    e_ab = rinfo[:, 0:2].astype(jnp.int32)
    base = jnp.sum(jnp.where(e_ab[..., None] == experts, pad_start, 0), axis=-1)
    dest = (base + rinfo[:, 2:4].astype(jnp.int32)).reshape(-1)
    block_row = jnp.arange(n_blocks, dtype=jnp.int32) * bm
    block_expert = jnp.minimum(jnp.sum((pad_end[None, :] <= block_row[:, None]).astype(jnp.int32), axis=-1),
                               N_EXPERTS - 1)
    n_used = (pad_end[-1:] // bm).astype(jnp.int32)

    xbuf = _dispatch(dest, xp, xbuf0)
    wgu = jnp.concatenate([w_gate[0], w_up[0]], axis=-1).astype(BF16)
    ybuf = _experts(block_expert, n_used, xbuf, wgu, w_down[0].astype(BF16))
    out = _combine(dest, h2.reshape(t, d), rinfo, norm_final[None, :], ybuf)
    return out.reshape(bsz, seq, d)
```

```python
import functools

import jax
import jax.numpy as jnp
from jax import lax
from jax.experimental import pallas as pl
from jax.experimental.pallas import tpu as pltpu

F32 = jnp.float32
BF16 = jnp.bfloat16
EPS = 1e-6

LANES = 128
SUBLANES = 8
VMEM_LIMIT_BYTES = 56 * 1024 * 1024

D_MODEL = 1024
SSD_WIDTH = 512
SSD_HEAD_DIM = 64
SSD_HEADS = 8
SSD_GROUPS = 2
SSD_HEADS_PER_GROUP = 4
SSD_STATE = 64
SSD_CHUNK = 128
XBC_WIDTH = SSD_WIDTH + 2 * SSD_GROUPS * SSD_STATE
SC_WIDTH = 512
XA_HEADS = 4
XA_HEAD_DIM = 256
N_GROUPS_MOE = 4
EXPERTS_PER_GROUP = 8
N_EXPERTS = 32
D_EXPERT = 512

COL_Z = 0
COL_XBC = COL_Z + SSD_WIDTH
COL_GB = COL_XBC + XBC_WIDTH
COL_GC = COL_GB + SC_WIDTH
COL_V = COL_GC + SC_WIDTH
COL_DT = COL_V + SC_WIDTH
PROJ_WIDTH = COL_DT + LANES

SEQ_TILE = 512
ROW_TILE = 512
EXPERT_BLOCK = 512
HALO = SUBLANES

ROUTER_GROUP_ROW = 0
ROUTER_EXPERT_ROW = SUBLANES


def _rms(x, g):
    return x * lax.rsqrt(jnp.mean(x * x, axis=-1, keepdims=True) + EPS) * g


def _silu(x):
    return x * jax.nn.sigmoid(x)


def _dot(a, b):
    return jnp.dot(a, b, preferred_element_type=F32)


def _mixer_kernel(x_ref, gmix_ref, win_ref, cw_ref, cb_ref, dtb_ref, a_ref, dskip_ref, gate_ref,
                  scw_ref, wout_ref, tri_ref, h_ref,
                  conv_ref, cv_ref, act_ref, dt_ref, y_ref, state_ref, sinc_ref, loff_ref, elast_ref):
    ts = x_ref.shape[1]
    s = pl.program_id(1)
    x = x_ref[0]
    nb = _rms(x, gmix_ref[...]).astype(BF16)

    @pl.when(s == 0)
    def _():
        conv_ref[0:HALO, :] = jnp.zeros((HALO, XBC_WIDTH), F32)
        cv_ref[0:HALO, :] = jnp.zeros((HALO, SC_WIDTH), F32)
        state_ref[...] = jnp.zeros(state_ref.shape, F32)

    @pl.when(s > 0)
    def _():
        conv_ref[0:HALO, :] = conv_ref[ts:ts + HALO, :]
        cv_ref[0:HALO, :] = cv_ref[ts:ts + HALO, :]

    xbc = _dot(nb, win_ref[:, COL_XBC:COL_XBC + XBC_WIDTH])
    conv_ref[HALO:HALO + ts, :] = xbc
    acc = cb_ref[...] + cw_ref[3:4, :] * xbc
    for i in range(3):
        acc = acc + cw_ref[i:i + 1, :] * conv_ref[HALO - 3 + i:HALO - 3 + i + ts, :]
    act_ref[...] = _silu(acc)

    dtr = _dot(nb, win_ref[:, COL_DT:COL_DT + LANES]) + dtb_ref[...]
    dt_ref[...] = jnp.maximum(dtr, 0.0) + jnp.log1p(jnp.exp(-jnp.abs(dtr)))

    row = lax.broadcasted_iota(jnp.int32, (SSD_CHUNK, SSD_CHUNK), 0)
    col = lax.broadcasted_iota(jnp.int32, (SSD_CHUNK, SSD_CHUNK), 1)
    causal = row >= col
    lo_q = col < SSD_HEAD_DIM
    lo_n = lax.broadcasted_iota(jnp.int32, (SSD_STATE, SSD_CHUNK), 1) < SSD_HEAD_DIM
    lo_e = lax.broadcasted_iota(jnp.int32, (SUBLANES, SSD_CHUNK), 1) < SSD_HEAD_DIM

    def chunk_prologue(c):
        rows = slice(c * SSD_CHUNK, (c + 1) * SSD_CHUNK)
        dt = dt_ref[rows, :]
        adt = dt * a_ref[...]
        cs = jnp.dot(tri_ref[...], adt, preferred_element_type=F32,
                     precision=lax.Precision.HIGHEST)
        bm = act_ref[rows, SSD_WIDTH:SSD_WIDTH + SSD_GROUPS * SSD_STATE]
        cm = act_ref[rows, SSD_WIDTH + SSD_GROUPS * SSD_STATE:XBC_WIDTH]
        cbs = []
        for g in range(SSD_GROUPS):
            cg = cm[:, g * SSD_STATE:(g + 1) * SSD_STATE]
            bg = bm[:, g * SSD_STATE:(g + 1) * SSD_STATE]
            cbs.append(lax.dot_general(cg.astype(BF16), bg.astype(BF16), (((1,), (1,)), ((), ())),
                                       preferred_element_type=F32))
        return cs, cs.T, dt.T, bm.T, cm, pltpu.roll(cm, SSD_STATE, axis=1), cbs

    def chunk_body(c, prologue):
        rows = slice(c * SSD_CHUNK, (c + 1) * SSD_CHUNK)
        cs, cs_t, dt_t, bm_t, cm, cm_sw, cbs = prologue
        for g in range(SSD_GROUPS):
            bg_t = bm_t[g * SSD_STATE:(g + 1) * SSD_STATE, :]
            cb = cbs[g]
            c2 = jnp.where(lo_q, cm, cm_sw) if g == 0 else jnp.where(lo_q, cm_sw, cm)
            for jj in range(SSD_HEADS_PER_GROUP // 2):
                j = g * (SSD_HEADS_PER_GROUP // 2) + jj
                idx = c * (SSD_HEADS // 2) + j
                xp = act_ref[rows, j * LANES:(j + 1) * LANES]
                rhs = jnp.concatenate(
                    [jnp.where(lo_q, xp, 0.0), jnp.where(lo_q, 0.0, xp)], axis=0).astype(BF16)
                diag, off, w_rows, e_last = [], [], [], []
                for h in (2 * j, 2 * j + 1):
                    cs_col = jnp.broadcast_to(cs[:, h:h + 1], (SSD_CHUNK, SSD_CHUNK))
                    cs_row = cs_t[h:h + 1, :]
                    dt_row = dt_t[h:h + 1, :]
                    decay = jnp.exp(jnp.where(causal, cs_col - cs_row, -jnp.inf))
                    diag.append((cb * dt_row) * decay)
                    off.append(jnp.exp(cs_col))
                    cl = cs_t[h:h + 1, SSD_CHUNK - 1:SSD_CHUNK]
                    w_rows.append(jnp.exp(cl - cs_row) * dt_row)
                    e_last.append(jnp.exp(cl))
                y_ref[rows, j * LANES:(j + 1) * LANES] = _dot(jnp.concatenate(diag, axis=1).astype(BF16), rhs)
                lhs2 = jnp.concatenate([bg_t * w_rows[0], bg_t * w_rows[1]], axis=1).astype(BF16)
                sinc_ref[idx] = _dot(lhs2, rhs)
                loff_ref[idx] = (c2 * jnp.where(lo_q, off[0], off[1])).astype(BF16)
                elast_ref[idx] = jnp.where(lo_e, e_last[0], e_last[1])

    n_chunks = ts // SSD_CHUNK
    prologues = [chunk_prologue(c) for c in range(n_chunks)]

    g_c = _dot(nb, win_ref[:, COL_GC:COL_GC + SC_WIDTH])
    v = _dot(nb, win_ref[:, COL_V:COL_V + SC_WIDTH])
    u = g_c * v
    cv_ref[HALO:HALO + ts, :] = u
    conv3 = scw_ref[2:3, :] * u
    for i in range(2):
        conv3 = conv3 + scw_ref[i:i + 1, :] * cv_ref[HALO - 2 + i:HALO - 2 + i + ts, :]
    g_b = _dot(nb, win_ref[:, COL_GB:COL_GB + SC_WIDTH])
    out_sc = _dot((g_b * conv3).astype(BF16), wout_ref[SSD_WIDTH:, :])
    gate_z = _silu(_dot(nb, win_ref[:, COL_Z:COL_Z + SSD_WIDTH]))

    for c in range(n_chunks):
        chunk_body(c, prologues[c])

    for c in range(n_chunks):
        for j in range(SSD_HEADS // 2):
            idx = c * (SSD_HEADS // 2) + j
            stp = state_ref[j]
            rhs_st = jnp.concatenate(
                [jnp.where(lo_n, stp, 0.0), jnp.where(lo_n, 0.0, stp)], axis=0).astype(BF16)
            y_ref[c * SSD_CHUNK:(c + 1) * SSD_CHUNK, j * LANES:(j + 1) * LANES] += _dot(loff_ref[idx], rhs_st)
            state_ref[j] = elast_ref[idx][0:1, :] * stp + sinc_ref[idx]

    y = y_ref[...] + act_ref[:, 0:SSD_WIDTH] * dskip_ref[...]
    gated = y * gate_z
    gw = SSD_WIDTH // SSD_GROUPS
    y_ssd = jnp.concatenate(
        [_rms(gated[:, g * gw:(g + 1) * gw], gate_ref[:, g * gw:(g + 1) * gw]) for g in range(SSD_GROUPS)],
        axis=-1).astype(BF16)
    h_ref[0] = x + out_sc + _dot(y_ssd, wout_ref[0:SSD_WIDTH, :])


def _const_spec(shape):
    nd = len(shape)
    return pl.BlockSpec(shape, lambda *_: (0,) * nd)


def _mixer(x, gmix, win, cw, cb, dtb, a_row, dskip, gate, scw, wout, tri):
    bsz, seq, d = x.shape
    ts = SEQ_TILE
    consts = (gmix, win, cw, cb, dtb, a_row, dskip, gate, scw, wout, tri)
    return pl.pallas_call(
        _mixer_kernel,
        grid=(bsz, seq // ts),
        in_specs=[pl.BlockSpec((1, ts, d), lambda b, s: (b, s, 0))] + [_const_spec(c.shape) for c in consts],
        out_specs=pl.BlockSpec((1, ts, d), lambda b, s: (b, s, 0)),
        out_shape=jax.ShapeDtypeStruct((bsz, seq, d), F32),
        scratch_shapes=[
            pltpu.VMEM((ts + HALO, XBC_WIDTH), F32),
            pltpu.VMEM((ts + HALO, SC_WIDTH), F32),
            pltpu.VMEM((ts, XBC_WIDTH), F32),
            pltpu.VMEM((ts, LANES), F32),
            pltpu.VMEM((ts, SSD_WIDTH), F32),
            pltpu.VMEM((SSD_HEADS // 2, SSD_STATE, 2 * SSD_HEAD_DIM), F32),
            pltpu.VMEM((ts // SSD_CHUNK * SSD_HEADS // 2, SSD_STATE, 2 * SSD_HEAD_DIM), F32),
            pltpu.VMEM((ts // SSD_CHUNK * SSD_HEADS // 2, SSD_CHUNK, 2 * SSD_STATE), BF16),
            pltpu.VMEM((ts // SSD_CHUNK * SSD_HEADS // 2, SUBLANES, 2 * SSD_HEAD_DIM), F32),
        ],
        compiler_params=pltpu.CompilerParams(
            dimension_semantics=("arbitrary", "arbitrary"), vmem_limit_bytes=VMEM_LIMIT_BYTES),
        name="mixer",
    )(x, *consts)


def _kv_kernel(mem_ref, g_ref, wkv_ref, kt_ref, v_ref):
    d = mem_ref.shape[2]
    mn = _rms(mem_ref[0], g_ref[...]).astype(BF16)
    kv = _dot(mn, wkv_ref[...])
    kt_ref[0] = (kv[:, :d] * (XA_HEAD_DIM ** -0.5)).T.astype(BF16)
    v_ref[0] = kv[:, d:].astype(BF16)


def _memory_kv(mem, g, wkv):
    bsz, m, d = mem.shape
    return pl.pallas_call(
        _kv_kernel,
        grid=(bsz,),
        in_specs=[pl.BlockSpec((1, m, d), lambda b: (b, 0, 0)), _const_spec(g.shape), _const_spec(wkv.shape)],
        out_specs=[pl.BlockSpec((1, d, m), lambda b: (b, 0, 0)), pl.BlockSpec((1, m, d), lambda b: (b, 0, 0))],
        out_shape=[jax.ShapeDtypeStruct((bsz, d, m), BF16), jax.ShapeDtypeStruct((bsz, m, d), BF16)],
        compiler_params=pltpu.CompilerParams(
            dimension_semantics=("arbitrary",), vmem_limit_bytes=VMEM_LIMIT_BYTES),
        name="memory_kv",
    )(mem, g, wkv)


def _xattn_router_kernel(h1_ref, gx_ref, wq_ref, kt_ref, v_ref, wo_ref, gm_ref, wr_ref, br_ref, upper_ref,
                         h2_ref, xp_ref, rinfo_ref, counts_ref, zero_ref, carry_ref):
    ts = h1_ref.shape[1]
    first = jnp.logical_and(pl.program_id(0) == 0, pl.program_id(1) == 0)

    @pl.when(first)
    def _():
        carry_ref[...] = jnp.zeros(carry_ref.shape, F32)

    h1 = h1_ref[0]
    q = _dot(_rms(h1, gx_ref[...]).astype(BF16), wq_ref[...]).astype(BF16)
    outs = []
    for h in range(XA_HEADS):
        hs = slice(h * XA_HEAD_DIM, (h + 1) * XA_HEAD_DIM)
        sc = _dot(q[:, hs], kt_ref[0, hs, :])
        p = jnp.exp(sc - jnp.max(sc, axis=-1, keepdims=True))
        o = _dot(p.astype(BF16), v_ref[0, :, hs]) / jnp.sum(p, axis=-1, keepdims=True)
        outs.append(o.astype(BF16))
    h2 = h1 + _dot(jnp.concatenate(outs, axis=-1), wo_ref[...])
    h2_ref[0] = h2

    n3 = _rms(h2, gm_ref[...])
    n3b = n3.astype(BF16)
    half = n3.shape[1] // 2
    n3r = n3b.astype(F32)
    lo = lax.shift_right_logical(pltpu.bitcast(n3r[:, :half], jnp.uint32), jnp.uint32(16))
    hi = pltpu.bitcast(n3r[:, half:], jnp.uint32) & jnp.uint32(0xFFFF0000)
    xp_ref[...] = lo | hi
    zero_ref[...] = jnp.zeros(zero_ref.shape, zero_ref.dtype)

    n3_lo = (n3 - n3r).astype(BF16)
    lg2 = _dot(n3b, wr_ref[...])
    logits = lg2[:, :LANES] + lg2[:, LANES:] + _dot(n3_lo, wr_ref[:, :LANES]) + br_ref[...]
    lt = logits.T

    l0, l1, l2, l3 = (lt[ROUTER_GROUP_ROW + i:ROUTER_GROUP_ROW + i + 1, :] for i in range(N_GROUPS_MOE))
    m = jnp.maximum(jnp.maximum(l0, l1), jnp.maximum(l2, l3))
    gidx = jnp.where(l0 == m, 0, jnp.where(l1 == m, 1, jnp.where(l2 == m, 2, 3)))
    g_w = 1.0 / (jnp.exp(l0 - m) + jnp.exp(l1 - m) + jnp.exp(l2 - m) + jnp.exp(l3 - m))

    def group_rows(g):
        r = ROUTER_EXPERT_ROW + g * EXPERTS_PER_GROUP
        return lt[r:r + EXPERTS_PER_GROUP, :]

    el = jnp.where(gidx == 0, group_rows(0),
                   jnp.where(gidx == 1, group_rows(1), jnp.where(gidx == 2, group_rows(2), group_rows(3))))
    sub = lax.broadcasted_iota(jnp.int32, el.shape, 0)
    m1 = jnp.max(el, axis=0, keepdims=True)
    i1 = jnp.min(jnp.where(el == m1, sub, EXPERTS_PER_GROUP), axis=0, keepdims=True)
    el2 = jnp.where(sub == i1, -jnp.inf, el)
    m2 = jnp.max(el2, axis=0, keepdims=True)
    i2 = jnp.min(jnp.where(el2 == m2, sub, EXPERTS_PER_GROUP), axis=0, keepdims=True)
    e2 = jnp.exp(m2 - m1)
    w1 = 1.0 / (1.0 + e2)
    w2 = e2 / (1.0 + e2)
    e_a = gidx * EXPERTS_PER_GROUP + i1
    e_b = gidx * EXPERTS_PER_GROUP + i2

    eio = lax.broadcasted_iota(jnp.int32, (N_EXPERTS, ts), 0)
    oh_a = (eio == e_a).astype(F32)
    oh_b = (eio == e_b).astype(F32)
    prefix = _dot(jnp.concatenate([oh_a, oh_b], axis=0).astype(BF16), upper_ref[...])
    tot_a = jnp.sum(oh_a, axis=1, keepdims=True)
    tot_b = jnp.sum(oh_b, axis=1, keepdims=True)
    carry = carry_ref[:, 0:1]
    rank_a = jnp.sum(oh_a * (carry + prefix[:N_EXPERTS]), axis=0, keepdims=True)
    rank_b = jnp.sum(oh_b * (carry + tot_a + prefix[N_EXPERTS:]), axis=0, keepdims=True)
    new_carry = jnp.broadcast_to(carry + tot_a + tot_b, carry_ref.shape)
    carry_ref[...] = new_carry
    counts_ref[...] = new_carry

    res = jnp.concatenate(
        [e_a.astype(F32), e_b.astype(F32), rank_a, rank_b, g_w * w1, g_w * w2,
         jnp.zeros((LANES - 6, ts), F32)], axis=0)
    rinfo_ref[...] = res.T


def _xattn_router(h1, gx, wq, kt, v, wo, gm, wr, br, upper, n_rows):
    bsz, seq, d = h1.shape
    ts = SEQ_TILE
    m = v.shape[1]
    nt = seq // ts
    zero_rows = n_rows // (bsz * nt)
    assert zero_rows * bsz * nt == n_rows and zero_rows % SUBLANES == 0
    tok_spec = pl.BlockSpec((1, ts, d), lambda b, s: (b, s, 0))
    return pl.pallas_call(
        _xattn_router_kernel,
        grid=(bsz, nt),
        in_specs=[tok_spec, _const_spec(gx.shape), _const_spec(wq.shape),
                  pl.BlockSpec((1, d, m), lambda b, s: (b, 0, 0)),
                  pl.BlockSpec((1, m, d), lambda b, s: (b, 0, 0)),
                  _const_spec(wo.shape), _const_spec(gm.shape), _const_spec(wr.shape), _const_spec(br.shape),
                  _const_spec(upper.shape)],
        out_specs=[tok_spec,
                   pl.BlockSpec((ts, d // 2), lambda b, s: (b * nt + s, 0)),
                   pl.BlockSpec((ts, LANES), lambda b, s: (b * nt + s, 0)),
                   pl.BlockSpec((N_EXPERTS, LANES), lambda b, s: (0, 0)),
                   pl.BlockSpec((zero_rows, d // 2), lambda b, s: (b * nt + s, 0))],
        out_shape=[jax.ShapeDtypeStruct((bsz, seq, d), F32),
                   jax.ShapeDtypeStruct((bsz * seq, d // 2), jnp.uint32),
                   jax.ShapeDtypeStruct((bsz * seq, LANES), F32),
                   jax.ShapeDtypeStruct((N_EXPERTS, LANES), F32),
                   jax.ShapeDtypeStruct((n_rows, d // 2), jnp.uint32)],
        scratch_shapes=[pltpu.VMEM((N_EXPERTS, LANES), F32)],
        compiler_params=pltpu.CompilerParams(
            dimension_semantics=("arbitrary", "arbitrary"), vmem_limit_bytes=VMEM_LIMIT_BYTES),
        name="xattn_router",
    )(h1, gx, wq, kt, v, wo, gm, wr, br, upper)


def _dispatch_kernel(dest_ref, xp_ref, buf_in_ref, buf_ref, sem):
    del buf_in_ref
    tr = xp_ref.shape[0]

    def row_copy(r, k):
        d = dest_ref[0, 0, 2 * r + k]
        return pltpu.make_async_copy(xp_ref.at[pl.ds(r, 1)], buf_ref.at[pl.ds(d, 1)], sem)

    def start(r, c):
        row_copy(r, 0).start(priority=0)
        row_copy(r, 1).start(priority=1)
        return c

    def wait(r, c):
        row_copy(r, 0).wait()
        row_copy(r, 1).wait()
        return c

    lax.fori_loop(0, tr, start, 0, unroll=8)
    lax.fori_loop(0, tr, wait, 0, unroll=8)


def _dispatch(dest, xp, buf0):
    t, w = xp.shape
    tr = ROW_TILE
    return pl.pallas_call(
        _dispatch_kernel,
        grid=(t // tr,),
        in_specs=[pl.BlockSpec((1, 1, 2 * tr), lambda i: (i, 0, 0), memory_space=pltpu.SMEM),
                  pl.BlockSpec((tr, w), lambda i: (i, 0)),
                  pl.BlockSpec(memory_space=pl.ANY)],
        out_specs=pl.BlockSpec(memory_space=pl.ANY),
        out_shape=jax.ShapeDtypeStruct(buf0.shape, buf0.dtype),
        scratch_shapes=[pltpu.SemaphoreType.DMA],
        input_output_aliases={2: 0},
        compiler_params=pltpu.CompilerParams(
            dimension_semantics=("arbitrary",), vmem_limit_bytes=VMEM_LIMIT_BYTES),
        name="dispatch",
    )(dest.reshape(t // tr, 1, 2 * tr), xp, buf0)


def _expert_kernel(be_ref, nused_ref, xb_ref, wg_ref, wu_ref, wd_ref, y_ref, wgu_s, wd_s):
    i = pl.program_id(0)
    used = i < nused_ref[0]

    @pl.when(jnp.logical_and(used, jnp.logical_or(i == 0, be_ref[i] != be_ref[jnp.maximum(i - 1, 0)])))
    def _():
        wgu_s[:, :D_EXPERT] = wg_ref[0].astype(BF16)
        wgu_s[:, D_EXPERT:] = wu_ref[0].astype(BF16)
        wd_s[...] = wd_ref[0].astype(BF16)

    @pl.when(used)
    def _():
        w = xb_ref[...]
        x_lo = pltpu.bitcast(lax.shift_left(w, jnp.uint32(16)), F32).astype(BF16)
        x_hi = pltpu.bitcast(w & jnp.uint32(0xFFFF0000), F32).astype(BF16)
        half = w.shape[1]
        gu = _dot(x_lo, wgu_s[:half, :]) + _dot(x_hi, wgu_s[half:, :])
        mid = (_silu(gu[:, :D_EXPERT]) * gu[:, D_EXPERT:]).astype(BF16)
        y_ref[...] = _dot(mid, wd_s[...])

    @pl.when(jnp.logical_not(used))
    def _():
        y_ref[...] = jnp.zeros(y_ref.shape, F32)


def _experts(block_expert, n_used, xbuf, w_gate, w_up, w_down):
    w = xbuf.shape[1]
    bm = EXPERT_BLOCK
    n_blocks = block_expert.shape[0]
    n_rows = n_blocks * bm
    d, de = w_gate.shape[1], w_gate.shape[2]
    grid_spec = pltpu.PrefetchScalarGridSpec(
        num_scalar_prefetch=2,
        grid=(n_blocks,),
        in_specs=[pl.BlockSpec((bm, w), lambda i, be, nu: (i, 0)),
                  pl.BlockSpec((1, d, de), lambda i, be, nu: (be[i], 0, 0)),
                  pl.BlockSpec((1, d, de), lambda i, be, nu: (be[i], 0, 0)),
                  pl.BlockSpec((1, de, d), lambda i, be, nu: (be[i], 0, 0))],
        out_specs=pl.BlockSpec((bm, d), lambda i, be, nu: (i, 0)),
        scratch_shapes=[pltpu.VMEM((d, 2 * de), BF16), pltpu.VMEM((de, d), BF16)],
    )
    return pl.pallas_call(
        _expert_kernel,
        grid_spec=grid_spec,
        out_shape=jax.ShapeDtypeStruct((n_rows, d), F32),
        compiler_params=pltpu.CompilerParams(
            dimension_semantics=("arbitrary",), vmem_limit_bytes=VMEM_LIMIT_BYTES),
        name="experts",
    )(block_expert, n_used, xbuf, w_gate, w_up, w_down)


def _combine_kernel(dest_ref, dest_next_ref, h2_ref, rinfo_ref, gf_ref, ybuf_ref, out_ref, ya_ref, yb_ref, sem):
    tr = h2_ref.shape[0]
    i = pl.program_id(0)
    slot = i % 2

    def row_copy(dref, s, r, k):
        d = dref[0, 0, 2 * r + k]
        dst = ya_ref if k == 0 else yb_ref
        return pltpu.make_async_copy(ybuf_ref.at[pl.ds(d, 1)], dst.at[s, pl.ds(r, 1)], sem.at[s])

    def gather(dref, s):
        def start(r, c):
            row_copy(dref, s, r, 0).start(priority=0)
            row_copy(dref, s, r, 1).start(priority=1)
            return c
        lax.fori_loop(0, tr, start, 0, unroll=8)

    @pl.when(i == 0)
    def _():
        gather(dest_ref, 0)

    @pl.when(i + 1 < pl.num_programs(0))
    def _():
        gather(dest_next_ref, 1 - slot)

    def wait(r, c):
        row_copy(dest_ref, slot, r, 0).wait()
        row_copy(dest_ref, slot, r, 1).wait()
        return c

    lax.fori_loop(0, tr, wait, 0, unroll=8)
    info = rinfo_ref[...]
    h = h2_ref[...] + info[:, 4:5] * ya_ref[slot] + info[:, 5:6] * yb_ref[slot]
    out_ref[...] = _rms(h, gf_ref[...])


def _combine(dest, h2, rinfo, gf, ybuf):
    t, d = h2.shape
    tr = ROW_TILE
    n_tiles = t // tr
    dest_tiles = dest.reshape(n_tiles, 1, 2 * tr)
    return pl.pallas_call(
        _combine_kernel,
        grid=(n_tiles,),
        in_specs=[pl.BlockSpec((1, 1, 2 * tr), lambda i: (i, 0, 0), memory_space=pltpu.SMEM),
                  pl.BlockSpec((1, 1, 2 * tr), lambda i: (jnp.minimum(i + 1, n_tiles - 1), 0, 0),
                               memory_space=pltpu.SMEM),
                  pl.BlockSpec((tr, d), lambda i: (i, 0)),
                  pl.BlockSpec((tr, LANES), lambda i: (i, 0)),
                  _const_spec(gf.shape),
                  pl.BlockSpec(memory_space=pl.ANY)],
        out_specs=pl.BlockSpec((tr, d), lambda i: (i, 0)),
        out_shape=jax.ShapeDtypeStruct((t, d), F32),
        scratch_shapes=[pltpu.VMEM((2, tr, d), F32), pltpu.VMEM((2, tr, d), F32),
                        pltpu.SemaphoreType.DMA((2,))],
        compiler_params=pltpu.CompilerParams(
            dimension_semantics=("arbitrary",), vmem_limit_bytes=VMEM_LIMIT_BYTES),
        name="combine",
    )(dest_tiles, dest_tiles, h2, rinfo, gf, ybuf)


def _pad_lanes(v, fill=0.0):
    return jnp.pad(v, (0, LANES - v.shape[0]), constant_values=fill)[None, :]


def kernel(x, mem, norm_mem, norm_mix, w_in, conv_ssd_w, conv_ssd_b, dt_bias, a_log, d_skip, norm_ssd_gate,
           conv_short_w, w_out, norm_xattn, w_q, w_kv, w_o, norm_moe, w_router_group, b_router_group,
           w_router_expert, b_router_expert, w_gate, w_up, w_down, norm_final):
    bsz, seq, d = x.shape
    t = bsz * seq
    assert w_in.shape[0] == 1, "single-layer model"
    assert d == D_MODEL and seq % SEQ_TILE == 0 and t % ROW_TILE == 0

    tri = jnp.tril(jnp.ones((SSD_CHUNK, SSD_CHUNK), F32))
    upper = jnp.triu(jnp.ones((SEQ_TILE, SEQ_TILE), F32), k=1).astype(BF16)

    o2 = SSD_WIDTH + XBC_WIDTH
    o3 = o2 + SSD_HEADS
    wl = w_in[0]
    win = jnp.concatenate(
        [wl[:, :o2], wl[:, o3:], wl[:, o2:o3], jnp.zeros((d, LANES - SSD_HEADS), F32)], axis=1).astype(BF16)
    h1 = _mixer(x, norm_mix, win, conv_ssd_w[0], conv_ssd_b, _pad_lanes(dt_bias[0]),
                _pad_lanes(-jnp.exp(a_log[0])), jnp.repeat(d_skip[0], SSD_HEAD_DIM)[None, :], norm_ssd_gate,
                conv_short_w[0], w_out[0].astype(BF16), tri)

    kt, v = _memory_kv(mem, norm_mem[None, :], w_kv[0].astype(BF16))
    wr = jnp.zeros((d, LANES), F32)
    wr = wr.at[:, ROUTER_GROUP_ROW:ROUTER_GROUP_ROW + N_GROUPS_MOE].set(w_router_group[0])
    wr = wr.at[:, ROUTER_EXPERT_ROW:ROUTER_EXPERT_ROW + N_EXPERTS].set(w_router_expert[0])
    wr_hi = wr.astype(BF16)
    wr_lo = (wr - wr_hi.astype(F32)).astype(BF16)
    br = jnp.zeros((LANES,), F32)
    br = br.at[ROUTER_GROUP_ROW:ROUTER_GROUP_ROW + N_GROUPS_MOE].set(b_router_group[0])
    br = br.at[ROUTER_EXPERT_ROW:ROUTER_EXPERT_ROW + N_EXPERTS].set(b_router_expert[0])
    bm = EXPERT_BLOCK
    n_blocks = (2 * t) // bm + N_EXPERTS
    h2, xp, rinfo, counts, xbuf0 = _xattn_router(
        h1, norm_xattn, w_q[0].astype(BF16), kt, v, w_o[0].astype(BF16), norm_moe,
        jnp.concatenate([wr_hi, wr_lo], axis=1), br[None, :], upper, n_blocks * bm)

    sizes = counts[:, 0].astype(jnp.int32)
    padded = ((sizes + bm - 1) // bm) * bm
    pad_end = jnp.cumsum(padded)
    pad_start = pad_end - padded
    experts = jnp.arange(N_EXPERTS, dtype=jnp.int32)
    e_ab = rinfo[:, 0:2].astype(jnp.int32)
    base = jnp.sum(jnp.where(e_ab[..., None] == experts, pad_start, 0), axis=-1)
    dest = (base + rinfo[:, 2:4].astype(jnp.int32)).reshape(-1)
    block_row = jnp.arange(n_blocks, dtype=jnp.int32) * bm
    block_expert = jnp.minimum(jnp.sum((pad_end[None, :] <= block_row[:, None]).astype(jnp.int32), axis=-1),
                               N_EXPERTS - 1)
    n_used = (pad_end[-1:] // bm).astype(jnp.int32)

    xbuf = _dispatch(dest, xp, xbuf0)
    ybuf = _experts(block_expert, n_used, xbuf, w_gate[0], w_up[0], w_down[0])
    out = _combine(dest, h2.reshape(t, d), rinfo, norm_final[None, :], ybuf)
    return out.reshape(bsz, seq, d)
```

```python
import math

import jax
import jax.numpy as jnp
from jax import lax
from jax.experimental import pallas as pl
from jax.experimental.pallas import tpu as pltpu

F32 = jnp.float32
BF16 = jnp.bfloat16
EPS = 1e-6

LANES = 128
SUBLANES = 8
VMEM_LIMIT_BYTES = 56 * 1024 * 1024

D_MODEL = 1024
SSD_WIDTH = 512
SSD_HEAD_DIM = 64
SSD_HEADS = 8
SSD_GROUPS = 2
SSD_HEADS_PER_GROUP = 4
SSD_STATE = 64
SSD_CHUNK = 128
XBC_WIDTH = SSD_WIDTH + 2 * SSD_GROUPS * SSD_STATE
SC_WIDTH = 512
XA_HEADS = 4
XA_HEAD_DIM = 256
N_GROUPS_MOE = 4
EXPERTS_PER_GROUP = 8
N_EXPERTS = 32
D_EXPERT = 512

COL_Z = 0
COL_XBC = COL_Z + SSD_WIDTH
COL_GB = COL_XBC + XBC_WIDTH
COL_GC = COL_GB + SC_WIDTH
COL_V = COL_GC + SC_WIDTH
COL_DT = COL_V + SC_WIDTH
PROJ_WIDTH = COL_DT + LANES

SEQ_TILE = 512
ROW_TILE = 512
EXPERT_BLOCK = 512
HALO = SUBLANES
ROW_GROUP = SUBLANES
LOCAL_ROWS = 2 * SEQ_TILE + N_EXPERTS * ROW_GROUP
ROW_WORDS = D_MODEL // 2 + LANES

ROUTER_GROUP_ROW = 0
ROUTER_EXPERT_ROW = SUBLANES


def _rms(x, g):
    return x * lax.rsqrt(jnp.mean(x * x, axis=-1, keepdims=True) + EPS) * g


def _silu(x):
    return x * jax.nn.sigmoid(x)


def _dot(a, b):
    return jnp.dot(a, b, preferred_element_type=F32)


def _mixer_kernel(x_ref, gmix_ref, win_ref, cw_ref, cb_ref, dtb_ref, a_ref, dskip_ref, gate_ref,
                  scw_ref, wout_ref, tri_ref, h_ref,
                  conv_ref, cv_ref, act_ref, dt_ref, y_ref, state_ref, sinc_ref, loff_ref, elast_ref):
    ts = x_ref.shape[1]
    s = pl.program_id(1)
    x = x_ref[0]
    nb = _rms(x, gmix_ref[...]).astype(BF16)

    @pl.when(s == 0)
    def _():
        conv_ref[0:HALO, :] = jnp.zeros((HALO, XBC_WIDTH), F32)
        cv_ref[0:HALO, :] = jnp.zeros((HALO, SC_WIDTH), F32)
        state_ref[...] = jnp.zeros(state_ref.shape, F32)

    @pl.when(s > 0)
    def _():
        conv_ref[0:HALO, :] = conv_ref[ts:ts + HALO, :]
        cv_ref[0:HALO, :] = cv_ref[ts:ts + HALO, :]

    xbc = _dot(nb, win_ref[:, COL_XBC:COL_XBC + XBC_WIDTH])
    conv_ref[HALO:HALO + ts, :] = xbc
    acc = cb_ref[...] + cw_ref[3:4, :] * xbc
    for i in range(3):
        acc = acc + cw_ref[i:i + 1, :] * conv_ref[HALO - 3 + i:HALO - 3 + i + ts, :]
    act_ref[...] = _silu(acc)

    dtr = _dot(nb, win_ref[:, COL_DT:COL_DT + LANES]) + dtb_ref[...]
    dt_ref[...] = jnp.maximum(dtr, 0.0) + jnp.log1p(jnp.exp(-jnp.abs(dtr)))

    row = lax.broadcasted_iota(jnp.int32, (SSD_CHUNK, SSD_CHUNK), 0)
    col = lax.broadcasted_iota(jnp.int32, (SSD_CHUNK, SSD_CHUNK), 1)
    causal = row >= col
    lo_q = col < SSD_HEAD_DIM
    lo_n = lax.broadcasted_iota(jnp.int32, (SSD_STATE, SSD_CHUNK), 1) < SSD_HEAD_DIM
    lo_e = lax.broadcasted_iota(jnp.int32, (SUBLANES, SSD_CHUNK), 1) < SSD_HEAD_DIM

    def chunk_prologue(c):
        rows = slice(c * SSD_CHUNK, (c + 1) * SSD_CHUNK)
        dt = dt_ref[rows, :]
        adt = dt * a_ref[...]
        cs = jnp.dot(tri_ref[...], adt, preferred_element_type=F32,
                     precision=lax.Precision.HIGHEST)
        bm = act_ref[rows, SSD_WIDTH:SSD_WIDTH + SSD_GROUPS * SSD_STATE]
        cm = act_ref[rows, SSD_WIDTH + SSD_GROUPS * SSD_STATE:XBC_WIDTH]
        cbs = []
        for g in range(SSD_GROUPS):
            cg = cm[:, g * SSD_STATE:(g + 1) * SSD_STATE]
            bg = bm[:, g * SSD_STATE:(g + 1) * SSD_STATE]
            cbs.append(lax.dot_general(cg.astype(BF16), bg.astype(BF16), (((1,), (1,)), ((), ())),
                                       preferred_element_type=F32))
        return cs, cs.T, dt.T, bm.T, cm, pltpu.roll(cm, SSD_STATE, axis=1), cbs

    def chunk_body(c, prologue):
        rows = slice(c * SSD_CHUNK, (c + 1) * SSD_CHUNK)
        cs, cs_t, dt_t, bm_t, cm, cm_sw, cbs = prologue
        for g in range(SSD_GROUPS):
            bg_t = bm_t[g * SSD_STATE:(g + 1) * SSD_STATE, :]
            cb = cbs[g]
            c2 = jnp.where(lo_q, cm, cm_sw) if g == 0 else jnp.where(lo_q, cm_sw, cm)
            for jj in range(SSD_HEADS_PER_GROUP // 2):
                j = g * (SSD_HEADS_PER_GROUP // 2) + jj
                idx = c * (SSD_HEADS // 2) + j
                xp = act_ref[rows, j * LANES:(j + 1) * LANES]
                rhs = jnp.concatenate(
                    [jnp.where(lo_q, xp, 0.0), jnp.where(lo_q, 0.0, xp)], axis=0).astype(BF16)
                diag, off, w_rows, e_last = [], [], [], []
                for h in (2 * j, 2 * j + 1):
                    cs_col = jnp.broadcast_to(cs[:, h:h + 1], (SSD_CHUNK, SSD_CHUNK))
                    cs_row = cs_t[h:h + 1, :]
                    dt_row = dt_t[h:h + 1, :]
                    decay = jnp.exp(jnp.where(causal, cs_col - cs_row, -jnp.inf))
                    diag.append((cb * dt_row) * decay)
                    off.append(jnp.exp(cs_col))
                    cl = cs_t[h:h + 1, SSD_CHUNK - 1:SSD_CHUNK]
                    w_rows.append(jnp.exp(cl - cs_row) * dt_row)
                    e_last.append(jnp.exp(cl))
                y_ref[rows, j * LANES:(j + 1) * LANES] = _dot(jnp.concatenate(diag, axis=1).astype(BF16), rhs)
                lhs2 = jnp.concatenate([bg_t * w_rows[0], bg_t * w_rows[1]], axis=1).astype(BF16)
                sinc_ref[idx] = _dot(lhs2, rhs)
                loff_ref[idx] = (c2 * jnp.where(lo_q, off[0], off[1])).astype(BF16)
                elast_ref[idx] = jnp.where(lo_e, e_last[0], e_last[1])

    n_chunks = ts // SSD_CHUNK
    prologues = [chunk_prologue(c) for c in range(n_chunks)]

    g_c = _dot(nb, win_ref[:, COL_GC:COL_GC + SC_WIDTH])
    v = _dot(nb, win_ref[:, COL_V:COL_V + SC_WIDTH])
    u = g_c * v
    cv_ref[HALO:HALO + ts, :] = u
    conv3 = scw_ref[2:3, :] * u
    for i in range(2):
        conv3 = conv3 + scw_ref[i:i + 1, :] * cv_ref[HALO - 2 + i:HALO - 2 + i + ts, :]
    g_b = _dot(nb, win_ref[:, COL_GB:COL_GB + SC_WIDTH])
    out_sc = _dot((g_b * conv3).astype(BF16), wout_ref[SSD_WIDTH:, :])
    gate_z = _silu(_dot(nb, win_ref[:, COL_Z:COL_Z + SSD_WIDTH]))

    for c in range(n_chunks):
        chunk_body(c, prologues[c])

    for c in range(n_chunks):
        for j in range(SSD_HEADS // 2):
            idx = c * (SSD_HEADS // 2) + j
            stp = state_ref[j]
            rhs_st = jnp.concatenate(
                [jnp.where(lo_n, stp, 0.0), jnp.where(lo_n, 0.0, stp)], axis=0).astype(BF16)
            y_ref[c * SSD_CHUNK:(c + 1) * SSD_CHUNK, j * LANES:(j + 1) * LANES] += _dot(loff_ref[idx], rhs_st)
            state_ref[j] = elast_ref[idx][0:1, :] * stp + sinc_ref[idx]

    y = y_ref[...] + act_ref[:, 0:SSD_WIDTH] * dskip_ref[...]
    gated = y * gate_z
    gw = SSD_WIDTH // SSD_GROUPS
    y_ssd = jnp.concatenate(
        [_rms(gated[:, g * gw:(g + 1) * gw], gate_ref[:, g * gw:(g + 1) * gw]) for g in range(SSD_GROUPS)],
        axis=-1).astype(BF16)
    h_ref[0] = x + out_sc + _dot(y_ssd, wout_ref[0:SSD_WIDTH, :])


def _const_spec(shape):
    nd = len(shape)
    return pl.BlockSpec(shape, lambda *_: (0,) * nd)


def _mixer(x, gmix, win, cw, cb, dtb, a_row, dskip, gate, scw, wout, tri):
    bsz, seq, d = x.shape
    ts = SEQ_TILE
    consts = (gmix, win, cw, cb, dtb, a_row, dskip, gate, scw, wout, tri)
    return pl.pallas_call(
        _mixer_kernel,
        grid=(bsz, seq // ts),
        in_specs=[pl.BlockSpec((1, ts, d), lambda b, s: (b, s, 0))] + [_const_spec(c.shape) for c in consts],
        out_specs=pl.BlockSpec((1, ts, d), lambda b, s: (b, s, 0)),
        out_shape=jax.ShapeDtypeStruct((bsz, seq, d), F32),
        scratch_shapes=[
            pltpu.VMEM((ts + HALO, XBC_WIDTH), F32),
            pltpu.VMEM((ts + HALO, SC_WIDTH), F32),
            pltpu.VMEM((ts, XBC_WIDTH), F32),
            pltpu.VMEM((ts, LANES), F32),
            pltpu.VMEM((ts, SSD_WIDTH), F32),
            pltpu.VMEM((SSD_HEADS // 2, SSD_STATE, 2 * SSD_HEAD_DIM), F32),
            pltpu.VMEM((ts // SSD_CHUNK * SSD_HEADS // 2, SSD_STATE, 2 * SSD_HEAD_DIM), F32),
            pltpu.VMEM((ts // SSD_CHUNK * SSD_HEADS // 2, SSD_CHUNK, 2 * SSD_STATE), BF16),
            pltpu.VMEM((ts // SSD_CHUNK * SSD_HEADS // 2, SUBLANES, 2 * SSD_HEAD_DIM), F32),
        ],
        compiler_params=pltpu.CompilerParams(
            dimension_semantics=("arbitrary", "arbitrary"), vmem_limit_bytes=VMEM_LIMIT_BYTES),
        name="mixer",
    )(x, *consts)


def _kv_kernel(mem_ref, g_ref, wkv_ref, kt_ref, v_ref):
    d = mem_ref.shape[2]
    mn = _rms(mem_ref[0], g_ref[...]).astype(BF16)
    kv = _dot(mn, wkv_ref[...])
    kt_ref[0] = (kv[:, :d] * (XA_HEAD_DIM ** -0.5)).T.astype(BF16)
    v_ref[0] = kv[:, d:].astype(BF16)


def _memory_kv(mem, g, wkv):
    bsz, m, d = mem.shape
    return pl.pallas_call(
        _kv_kernel,
        grid=(bsz,),
        in_specs=[pl.BlockSpec((1, m, d), lambda b: (b, 0, 0)), _const_spec(g.shape), _const_spec(wkv.shape)],
        out_specs=[pl.BlockSpec((1, d, m), lambda b: (b, 0, 0)), pl.BlockSpec((1, m, d), lambda b: (b, 0, 0))],
        out_shape=[jax.ShapeDtypeStruct((bsz, d, m), BF16), jax.ShapeDtypeStruct((bsz, m, d), BF16)],
        compiler_params=pltpu.CompilerParams(
            dimension_semantics=("arbitrary",), vmem_limit_bytes=VMEM_LIMIT_BYTES),
        name="memory_kv",
    )(mem, g, wkv)


def _xattn_router_kernel(h1_ref, gx_ref, wq_ref, kt_ref, v_ref, wo_ref, gm_ref, wr_ref, br_ref, upper_ref,
                         lower_ref, h2_ref, xs_ref, rinfo_ref, seg_ref, counts_ref, zero_ref, carry_ref):
    ts = h1_ref.shape[1]
    first = jnp.logical_and(pl.program_id(0) == 0, pl.program_id(1) == 0)

    @pl.when(first)
    def _():
        carry_ref[...] = jnp.zeros(carry_ref.shape, F32)

    h1 = h1_ref[0]
    q = _dot(_rms(h1, gx_ref[...]).astype(BF16), wq_ref[...]).astype(BF16)
    outs = []
    for h in range(XA_HEADS):
        hs = slice(h * XA_HEAD_DIM, (h + 1) * XA_HEAD_DIM)
        sc = _dot(q[:, hs], kt_ref[0, hs, :])
        p = jnp.exp(sc - jnp.max(sc, axis=-1, keepdims=True))
        o = _dot(p.astype(BF16), v_ref[0, :, hs]) / jnp.sum(p, axis=-1, keepdims=True)
        outs.append(o.astype(BF16))
    h2 = h1 + _dot(jnp.concatenate(outs, axis=-1), wo_ref[...])
    h2_ref[0] = h2

    n3 = _rms(h2, gm_ref[...])
    n3b = n3.astype(BF16)
    half = n3.shape[1] // 2
    n3r = n3b.astype(F32)
    zero_ref[...] = jnp.zeros(zero_ref.shape, zero_ref.dtype)

    n3_lo = (n3 - n3r).astype(BF16)
    lg2 = _dot(n3b, wr_ref[...])
    logits = lg2[:, :LANES] + lg2[:, LANES:] + _dot(n3_lo, wr_ref[:, :LANES]) + br_ref[...]
    lt = logits.T

    l0, l1, l2, l3 = (lt[ROUTER_GROUP_ROW + i:ROUTER_GROUP_ROW + i + 1, :] for i in range(N_GROUPS_MOE))
    m = jnp.maximum(jnp.maximum(l0, l1), jnp.maximum(l2, l3))
    gidx = jnp.where(l0 == m, 0, jnp.where(l1 == m, 1, jnp.where(l2 == m, 2, 3)))
    g_w = 1.0 / (jnp.exp(l0 - m) + jnp.exp(l1 - m) + jnp.exp(l2 - m) + jnp.exp(l3 - m))

    def group_rows(g):
        r = ROUTER_EXPERT_ROW + g * EXPERTS_PER_GROUP
        return lt[r:r + EXPERTS_PER_GROUP, :]

    el = jnp.where(gidx == 0, group_rows(0),
                   jnp.where(gidx == 1, group_rows(1), jnp.where(gidx == 2, group_rows(2), group_rows(3))))
    sub = lax.broadcasted_iota(jnp.int32, el.shape, 0)
    m1 = jnp.max(el, axis=0, keepdims=True)
    i1 = jnp.min(jnp.where(el == m1, sub, EXPERTS_PER_GROUP), axis=0, keepdims=True)
    el2 = jnp.where(sub == i1, -jnp.inf, el)
    m2 = jnp.max(el2, axis=0, keepdims=True)
    i2 = jnp.min(jnp.where(el2 == m2, sub, EXPERTS_PER_GROUP), axis=0, keepdims=True)
    e2 = jnp.exp(m2 - m1)
    w1 = 1.0 / (1.0 + e2)
    w2 = e2 / (1.0 + e2)
    e_a = gidx * EXPERTS_PER_GROUP + i1
    e_b = gidx * EXPERTS_PER_GROUP + i2

    eio = lax.broadcasted_iota(jnp.int32, (N_EXPERTS, ts), 0)
    oh_a = (eio == e_a).astype(F32)
    oh_b = (eio == e_b).astype(F32)
    prefix = _dot(jnp.concatenate([oh_a, oh_b], axis=0).astype(BF16), upper_ref[...])
    tot_a = jnp.sum(oh_a, axis=1, keepdims=True)
    tot_b = jnp.sum(oh_b, axis=1, keepdims=True)
    groups = jnp.broadcast_to(jnp.floor((tot_a + tot_b + (ROW_GROUP - 1)) * (1.0 / ROW_GROUP)),
                              (N_EXPERTS, LANES))
    first_group = _dot(lower_ref[...], groups.astype(BF16))
    base = ROW_GROUP * first_group[:, 0:1]
    pos_a = jnp.sum(oh_a * (base + prefix[:N_EXPERTS]), axis=0, keepdims=True)
    pos_b = jnp.sum(oh_b * (base + tot_a + prefix[N_EXPERTS:]), axis=0, keepdims=True)

    carry = carry_ref[...]
    lane = lax.broadcasted_iota(jnp.int32, (N_EXPERTS, LANES), 1)
    seg_ref[...] = jnp.where(lane == 0, groups, jnp.where(lane == 1, first_group,
                                                          jnp.where(lane == 2, carry, 0.0)))
    carry_ref[...] = carry + groups
    counts_ref[...] = carry + groups

    n_local = xs_ref.shape[0]
    riota = lax.broadcasted_iota(jnp.int32, (n_local, ts), 0)
    sel_a = riota == pos_a.astype(jnp.int32)
    sel_b = riota == pos_b.astype(jnp.int32)
    perm = (jnp.where(sel_a, 1.0, 0.0) + jnp.where(sel_b, 1.0, 0.0)).astype(BF16)
    xs = _dot(perm, n3b)
    lo = lax.shift_right_logical(pltpu.bitcast(xs[:, :half], jnp.uint32), jnp.uint32(16))
    hi = pltpu.bitcast(xs[:, half:], jnp.uint32) & jnp.uint32(0xFFFF0000)
    xs_ref[:, :half] = lo | hi
    w_sorted = jnp.sum(jnp.where(sel_a, g_w * w1, 0.0) + jnp.where(sel_b, g_w * w2, 0.0),
                       axis=1, keepdims=True)
    xs_ref[:, half:] = pltpu.bitcast(jnp.broadcast_to(w_sorted, (n_local, LANES)), jnp.uint32)

    res = jnp.concatenate([pos_a, pos_b, jnp.zeros((LANES - 2, ts), F32)], axis=0)
    rinfo_ref[...] = res.T


def _xattn_router(h1, gx, wq, kt, v, wo, gm, wr, br, upper, lower, n_rows):
    bsz, seq, d = h1.shape
    ts = SEQ_TILE
    m = v.shape[1]
    nt = seq // ts
    n_tiles = bsz * nt
    zero_rows = n_rows // n_tiles
    assert zero_rows * n_tiles == n_rows and zero_rows % SUBLANES == 0
    tok_spec = pl.BlockSpec((1, ts, d), lambda b, s: (b, s, 0))
    return pl.pallas_call(
        _xattn_router_kernel,
        grid=(bsz, nt),
        in_specs=[tok_spec, _const_spec(gx.shape), _const_spec(wq.shape),
                  pl.BlockSpec((1, d, m), lambda b, s: (b, 0, 0)),
                  pl.BlockSpec((1, m, d), lambda b, s: (b, 0, 0)),
                  _const_spec(wo.shape), _const_spec(gm.shape), _const_spec(wr.shape), _const_spec(br.shape),
                  _const_spec(upper.shape), _const_spec(lower.shape)],
        out_specs=[tok_spec,
                   pl.BlockSpec((LOCAL_ROWS, ROW_WORDS), lambda b, s: (b * nt + s, 0)),
                   pl.BlockSpec((ts, LANES), lambda b, s: (b * nt + s, 0)),
                   pl.BlockSpec((N_EXPERTS, LANES), lambda b, s: (b * nt + s, 0)),
                   pl.BlockSpec((N_EXPERTS, LANES), lambda b, s: (0, 0)),
                   pl.BlockSpec((zero_rows, ROW_WORDS), lambda b, s: (b * nt + s, 0))],
        out_shape=[jax.ShapeDtypeStruct((bsz, seq, d), F32),
                   jax.ShapeDtypeStruct((n_tiles * LOCAL_ROWS, ROW_WORDS), jnp.uint32),
                   jax.ShapeDtypeStruct((bsz * seq, LANES), F32),
                   jax.ShapeDtypeStruct((n_tiles * N_EXPERTS, LANES), F32),
                   jax.ShapeDtypeStruct((N_EXPERTS, LANES), F32),
                   jax.ShapeDtypeStruct((n_rows, ROW_WORDS), jnp.uint32)],
        scratch_shapes=[pltpu.VMEM((N_EXPERTS, LANES), F32)],
        compiler_params=pltpu.CompilerParams(
            dimension_semantics=("arbitrary", "arbitrary"), vmem_limit_bytes=VMEM_LIMIT_BYTES),
        name="xattn_router",
    )(h1, gx, wq, kt, v, wo, gm, wr, br, upper, lower)


TAB_COUNT, TAB_SRC, TAB_DST, TAB_LOCAL = (k * N_EXPERTS for k in range(4))


def _group_rows(g):
    return pl.ds(pl.multiple_of(g * ROW_GROUP, ROW_GROUP), ROW_GROUP)


def _tile_groups(tab_ref):
    return lax.fori_loop(0, N_EXPERTS, lambda e, c: c + tab_ref[0, 0, TAB_COUNT + e], 0)


def _dispatch_kernel(tab_ref, xs_ref, buf_in_ref, buf_ref, sem):
    del buf_in_ref

    def group_copy(src, dst):
        return pltpu.make_async_copy(xs_ref.at[_group_rows(src)], buf_ref.at[_group_rows(dst)], sem)

    def expert(e, c):
        src0 = tab_ref[0, 0, TAB_SRC + e]
        dst0 = tab_ref[0, 0, TAB_DST + e]

        def start(k, c2):
            group_copy(src0 + k, dst0 + k).start()
            return c2
        return lax.fori_loop(0, tab_ref[0, 0, TAB_COUNT + e], start, c)

    lax.fori_loop(0, N_EXPERTS, expert, 0)

    def wait(k, c):
        group_copy(0, 0).wait()
        return c
    lax.fori_loop(0, _tile_groups(tab_ref), wait, 0)


def _dispatch(table, xs, buf0):
    n_tiles = table.shape[0]
    return pl.pallas_call(
        _dispatch_kernel,
        grid=(n_tiles,),
        in_specs=[pl.BlockSpec((1, 1, table.shape[2]), lambda i: (i, 0, 0), memory_space=pltpu.SMEM),
                  pl.BlockSpec(memory_space=pl.ANY),
                  pl.BlockSpec(memory_space=pl.ANY)],
        out_specs=pl.BlockSpec(memory_space=pl.ANY),
        out_shape=jax.ShapeDtypeStruct(buf0.shape, buf0.dtype),
        scratch_shapes=[pltpu.SemaphoreType.DMA],
        input_output_aliases={2: 0},
        compiler_params=pltpu.CompilerParams(
            dimension_semantics=("arbitrary",), vmem_limit_bytes=VMEM_LIMIT_BYTES),
        name="dispatch",
    )(table, xs, buf0)


def _expert_kernel(be_ref, nused_ref, xb_ref, wg_ref, wu_ref, wd_ref, y_ref, wgu_s, wd_s):
    i = pl.program_id(0)
    used = i < nused_ref[0]

    @pl.when(jnp.logical_and(used, jnp.logical_or(i == 0, be_ref[i] != be_ref[jnp.maximum(i - 1, 0)])))
    def _():
        wgu_s[:, :D_EXPERT] = wg_ref[0].astype(BF16)
        wgu_s[:, D_EXPERT:] = wu_ref[0].astype(BF16)
        wd_s[...] = wd_ref[0].astype(BF16)

    @pl.when(used)
    def _():
        half = xb_ref.shape[1] - LANES
        w = xb_ref[:, :half]
        x_lo = pltpu.bitcast(lax.shift_left(w, jnp.uint32(16)), F32).astype(BF16)
        x_hi = pltpu.bitcast(w & jnp.uint32(0xFFFF0000), F32).astype(BF16)
        gu = _dot(x_lo, wgu_s[:half, :]) + _dot(x_hi, wgu_s[half:, :])
        mid = (_silu(gu[:, :D_EXPERT]) * gu[:, D_EXPERT:]).astype(BF16)
        comb = pltpu.bitcast(xb_ref[:, half:half + LANES], F32)[:, 0:1]
        y_ref[...] = _dot(mid, wd_s[...]) * comb

    @pl.when(jnp.logical_not(used))
    def _():
        y_ref[...] = jnp.zeros(y_ref.shape, F32)


def _experts(block_expert, n_used, xbuf, w_gate, w_up, w_down):
    w = xbuf.shape[1]
    bm = EXPERT_BLOCK
    n_blocks = block_expert.shape[0]
    n_rows = n_blocks * bm
    d, de = w_gate.shape[1], w_gate.shape[2]
    grid_spec = pltpu.PrefetchScalarGridSpec(
        num_scalar_prefetch=2,
        grid=(n_blocks,),
        in_specs=[pl.BlockSpec((bm, w), lambda i, be, nu: (i, 0)),
                  pl.BlockSpec((1, d, de), lambda i, be, nu: (be[i], 0, 0)),
                  pl.BlockSpec((1, d, de), lambda i, be, nu: (be[i], 0, 0)),
                  pl.BlockSpec((1, de, d), lambda i, be, nu: (be[i], 0, 0))],
        out_specs=pl.BlockSpec((bm, d), lambda i, be, nu: (i, 0)),
        scratch_shapes=[pltpu.VMEM((d, 2 * de), BF16), pltpu.VMEM((de, d), BF16)],
    )
    return pl.pallas_call(
        _expert_kernel,
        grid_spec=grid_spec,
        out_shape=jax.ShapeDtypeStruct((n_rows, d), F32),
        compiler_params=pltpu.CompilerParams(
            dimension_semantics=("arbitrary",), vmem_limit_bytes=VMEM_LIMIT_BYTES),
        name="experts",
    )(block_expert, n_used, xbuf, w_gate, w_up, w_down)


def _combine_kernel(tab_ref, tab_next_ref, h2_ref, rinfo_ref, gf_ref, ybuf_ref, out_ref, ys_ref, sem):
    tr = h2_ref.shape[0]
    n_local = ys_ref.shape[1]
    i = pl.program_id(0)
    slot = i % 2

    def group_copy(s, src, dst):
        return pltpu.make_async_copy(ybuf_ref.at[_group_rows(src)], ys_ref.at[s, _group_rows(dst)], sem.at[s])

    def gather(tab, s):
        ys_ref[s] = jnp.zeros(ys_ref.shape[1:], F32)

        def expert(e, c):
            src0 = tab[0, 0, TAB_DST + e]
            dst0 = tab[0, 0, TAB_LOCAL + e]

            def start(k, c2):
                group_copy(s, src0 + k, dst0 + k).start()
                return c2
            return lax.fori_loop(0, tab[0, 0, TAB_COUNT + e], start, c)
        lax.fori_loop(0, N_EXPERTS, expert, 0)

    @pl.when(i == 0)
    def _():
        gather(tab_ref, 0)

    @pl.when(i + 1 < pl.num_programs(0))
    def _():
        gather(tab_next_ref, 1 - slot)

    def wait(k, c):
        group_copy(slot, 0, 0).wait()
        return c
    lax.fori_loop(0, _tile_groups(tab_ref), wait, 0)

    info = rinfo_ref[...]
    ciota = lax.broadcasted_iota(jnp.int32, (tr, n_local), 1)
    pick = (jnp.where(ciota == info[:, 0:1].astype(jnp.int32), 1.0, 0.0)
            + jnp.where(ciota == info[:, 1:2].astype(jnp.int32), 1.0, 0.0)).astype(BF16)
    h = h2_ref[...] + _dot(pick, ys_ref[slot].astype(BF16))
    out_ref[...] = _rms(h, gf_ref[...])


def _combine(table, h2, rinfo, gf, ybuf):
    t, d = h2.shape
    tr = SEQ_TILE
    n_tiles = t // tr
    tab_spec = (1, 1, table.shape[2])
    return pl.pallas_call(
        _combine_kernel,
        grid=(n_tiles,),
        in_specs=[pl.BlockSpec(tab_spec, lambda i: (i, 0, 0), memory_space=pltpu.SMEM),
                  pl.BlockSpec(tab_spec, lambda i: (jnp.minimum(i + 1, n_tiles - 1), 0, 0),
                               memory_space=pltpu.SMEM),
                  pl.BlockSpec((tr, d), lambda i: (i, 0)),
                  pl.BlockSpec((tr, LANES), lambda i: (i, 0)),
                  _const_spec(gf.shape),
                  pl.BlockSpec(memory_space=pl.ANY)],
        out_specs=pl.BlockSpec((tr, d), lambda i: (i, 0)),
        out_shape=jax.ShapeDtypeStruct((t, d), F32),
        scratch_shapes=[pltpu.VMEM((2, LOCAL_ROWS, d), F32), pltpu.SemaphoreType.DMA((2,))],
        compiler_params=pltpu.CompilerParams(
            dimension_semantics=("arbitrary",), vmem_limit_bytes=VMEM_LIMIT_BYTES),
        name="combine",
    )(table, table, h2, rinfo, gf, ybuf)


def _pad_lanes(v, fill=0.0):
    return jnp.pad(v, (0, LANES - v.shape[0]), constant_values=fill)[None, :]


def kernel(x, mem, norm_mem, norm_mix, w_in, conv_ssd_w, conv_ssd_b, dt_bias, a_log, d_skip, norm_ssd_gate,
           conv_short_w, w_out, norm_xattn, w_q, w_kv, w_o, norm_moe, w_router_group, b_router_group,
           w_router_expert, b_router_expert, w_gate, w_up, w_down, norm_final):
    bsz, seq, d = x.shape
    t = bsz * seq
    assert w_in.shape[0] == 1, "single-layer model"
    assert d == D_MODEL and seq % SEQ_TILE == 0 and t % ROW_TILE == 0

    tri = jnp.tril(jnp.ones((SSD_CHUNK, SSD_CHUNK), F32))
    upper = jnp.triu(jnp.ones((SEQ_TILE, SEQ_TILE), F32), k=1).astype(BF16)

    o2 = SSD_WIDTH + XBC_WIDTH
    o3 = o2 + SSD_HEADS
    wl = w_in[0]
    win = jnp.concatenate(
        [wl[:, :o2], wl[:, o3:], wl[:, o2:o3], jnp.zeros((d, LANES - SSD_HEADS), F32)], axis=1).astype(BF16)
    h1 = _mixer(x, norm_mix, win, conv_ssd_w[0], conv_ssd_b, _pad_lanes(dt_bias[0]),
                _pad_lanes(-jnp.exp(a_log[0])), jnp.repeat(d_skip[0], SSD_HEAD_DIM)[None, :], norm_ssd_gate,
                conv_short_w[0], w_out[0].astype(BF16), tri)

    kt, v = _memory_kv(mem, norm_mem[None, :], w_kv[0].astype(BF16))
    wr = jnp.zeros((d, LANES), F32)
    wr = wr.at[:, ROUTER_GROUP_ROW:ROUTER_GROUP_ROW + N_GROUPS_MOE].set(w_router_group[0])
    wr = wr.at[:, ROUTER_EXPERT_ROW:ROUTER_EXPERT_ROW + N_EXPERTS].set(w_router_expert[0])
    wr_hi = wr.astype(BF16)
    wr_lo = (wr - wr_hi.astype(F32)).astype(BF16)
    br = jnp.zeros((LANES,), F32)
    br = br.at[ROUTER_GROUP_ROW:ROUTER_GROUP_ROW + N_GROUPS_MOE].set(b_router_group[0])
    br = br.at[ROUTER_EXPERT_ROW:ROUTER_EXPERT_ROW + N_EXPERTS].set(b_router_expert[0])
    lower = jnp.tril(jnp.ones((N_EXPERTS, N_EXPERTS), F32), k=-1).astype(BF16)
    n_tiles = t // SEQ_TILE
    bgroups = EXPERT_BLOCK // ROW_GROUP
    max_groups = (2 * t + n_tiles * N_EXPERTS * (ROW_GROUP - 1)) // ROW_GROUP
    n_blocks = -(-max_groups // bgroups) + N_EXPERTS
    even = max(1, (n_tiles * SUBLANES) // math.gcd(n_tiles * SUBLANES, EXPERT_BLOCK))
    n_blocks = -(-n_blocks // even) * even
    h2, xs, rinfo, seg, counts, xbuf0 = _xattn_router(
        h1, norm_xattn, w_q[0].astype(BF16), kt, v, w_o[0].astype(BF16), norm_moe,
        jnp.concatenate([wr_hi, wr_lo], axis=1), br[None, :], upper, lower, n_blocks * EXPERT_BLOCK)

    seg = seg.reshape(n_tiles, N_EXPERTS, LANES)[:, :, 0:3].astype(jnp.int32)
    seg_groups, seg_local, seg_before = seg[:, :, 0], seg[:, :, 1], seg[:, :, 2]
    sizes = counts[:, 0].astype(jnp.int32)
    padded = ((sizes + bgroups - 1) // bgroups) * bgroups
    pad_end = jnp.cumsum(padded)
    pad_start = pad_end - padded
    seg_src = jnp.arange(n_tiles, dtype=jnp.int32)[:, None] * (LOCAL_ROWS // ROW_GROUP) + seg_local
    seg_dst = pad_start[None, :] + seg_before
    table = jnp.concatenate([seg_groups, seg_src, seg_dst, seg_local], axis=1)[:, None, :]
    block_group = jnp.arange(n_blocks, dtype=jnp.int32) * bgroups
    block_expert = jnp.minimum(jnp.sum((pad_end[None, :] <= block_group[:, None]).astype(jnp.int32), axis=-1),
                               N_EXPERTS - 1)
    n_used = (pad_end[-1:] // bgroups).astype(jnp.int32)

    xbuf = _dispatch(table, xs, xbuf0)
    ybuf = _experts(block_expert, n_used, xbuf, w_gate[0], w_up[0], w_down[0])
    out = _combine(table, h2.reshape(t, d), rinfo, norm_final[None, :], ybuf)
    return out.reshape(bsz, seq, d)
```

```python
import math

import jax
import jax.numpy as jnp
from jax import lax
from jax.experimental import pallas as pl
from jax.experimental.pallas import tpu as pltpu

F32 = jnp.float32
BF16 = jnp.bfloat16
EPS = 1e-6

LANES = 128
SUBLANES = 8
VMEM_LIMIT_BYTES = 56 * 1024 * 1024

D_MODEL = 1024
SSD_WIDTH = 512
SSD_HEAD_DIM = 64
SSD_HEADS = 8
SSD_GROUPS = 2
SSD_HEADS_PER_GROUP = 4
SSD_STATE = 64
SSD_CHUNK = 128
XBC_WIDTH = SSD_WIDTH + 2 * SSD_GROUPS * SSD_STATE
SC_WIDTH = 512
XA_HEADS = 4
XA_HEAD_DIM = 256
N_GROUPS_MOE = 4
EXPERTS_PER_GROUP = 8
N_EXPERTS = 32
D_EXPERT = 512

COL_Z = 0
COL_XBC = COL_Z + SSD_WIDTH
COL_GB = COL_XBC + XBC_WIDTH
COL_GC = COL_GB + SC_WIDTH
COL_V = COL_GC + SC_WIDTH
COL_DT = COL_V + SC_WIDTH
PROJ_WIDTH = COL_DT + LANES

SEQ_TILE = 512
ROW_TILE = 512
EXPERT_BLOCK = 512
HALO = SUBLANES
ROW_GROUP = SUBLANES
LOCAL_ROWS = 2 * SEQ_TILE + N_EXPERTS * ROW_GROUP
ROW_WORDS = D_MODEL // 2 + LANES

ROUTER_GROUP_ROW = 0
ROUTER_EXPERT_ROW = SUBLANES


def _rms(x, g):
    return x * lax.rsqrt(jnp.mean(x * x, axis=-1, keepdims=True) + EPS) * g


def _silu(x):
    return x * jax.nn.sigmoid(x)


def _dot(a, b):
    return jnp.dot(a, b, preferred_element_type=F32)


def _mixer_kernel(x_ref, gmix_ref, win_ref, cw_ref, cb_ref, dtb_ref, a_ref, dskip_ref, gate_ref,
                  scw_ref, wout_ref, tri_ref, h_ref,
                  conv_ref, cv_ref, act_ref, dt_ref, y_ref, state_ref, sinc_ref, loff_ref, elast_ref):
    ts = x_ref.shape[1]
    s = pl.program_id(1)
    x = x_ref[0]
    nb = _rms(x, gmix_ref[...]).astype(BF16)

    @pl.when(s == 0)
    def _():
        conv_ref[0:HALO, :] = jnp.zeros((HALO, XBC_WIDTH), F32)
        cv_ref[0:HALO, :] = jnp.zeros((HALO, SC_WIDTH), F32)
        state_ref[...] = jnp.zeros(state_ref.shape, F32)

    @pl.when(s > 0)
    def _():
        conv_ref[0:HALO, :] = conv_ref[ts:ts + HALO, :]
        cv_ref[0:HALO, :] = cv_ref[ts:ts + HALO, :]

    xbc = _dot(nb, win_ref[:, COL_XBC:COL_XBC + XBC_WIDTH])
    conv_ref[HALO:HALO + ts, :] = xbc
    acc = cb_ref[...] + cw_ref[3:4, :] * xbc
    for i in range(3):
        acc = acc + cw_ref[i:i + 1, :] * conv_ref[HALO - 3 + i:HALO - 3 + i + ts, :]
    act_ref[...] = _silu(acc)

    dtr = _dot(nb, win_ref[:, COL_DT:COL_DT + LANES]) + dtb_ref[...]
    dt_ref[...] = jnp.maximum(dtr, 0.0) + jnp.log1p(jnp.exp(-jnp.abs(dtr)))

    row = lax.broadcasted_iota(jnp.int32, (SSD_CHUNK, SSD_CHUNK), 0)
    col = lax.broadcasted_iota(jnp.int32, (SSD_CHUNK, SSD_CHUNK), 1)
    causal = row >= col
    lo_q = col < SSD_HEAD_DIM
    lo_n = lax.broadcasted_iota(jnp.int32, (SSD_STATE, SSD_CHUNK), 1) < SSD_HEAD_DIM
    lo_e = lax.broadcasted_iota(jnp.int32, (SUBLANES, SSD_CHUNK), 1) < SSD_HEAD_DIM

    def chunk_prologue(c):
        rows = slice(c * SSD_CHUNK, (c + 1) * SSD_CHUNK)
        dt = dt_ref[rows, :]
        adt = dt * a_ref[...]
        cs = jnp.dot(tri_ref[...], adt, preferred_element_type=F32,
                     precision=lax.Precision.HIGHEST)
        bm = act_ref[rows, SSD_WIDTH:SSD_WIDTH + SSD_GROUPS * SSD_STATE]
        cm = act_ref[rows, SSD_WIDTH + SSD_GROUPS * SSD_STATE:XBC_WIDTH]
        cbs = []
        for g in range(SSD_GROUPS):
            cg = cm[:, g * SSD_STATE:(g + 1) * SSD_STATE]
            bg = bm[:, g * SSD_STATE:(g + 1) * SSD_STATE]
            cbs.append(lax.dot_general(cg.astype(BF16), bg.astype(BF16), (((1,), (1,)), ((), ())),
                                       preferred_element_type=F32))
        return cs, cs.T, dt.T, bm.T, cm, pltpu.roll(cm, SSD_STATE, axis=1), cbs

    def chunk_body(c, prologue):
        rows = slice(c * SSD_CHUNK, (c + 1) * SSD_CHUNK)
        cs, cs_t, dt_t, bm_t, cm, cm_sw, cbs = prologue
        for g in range(SSD_GROUPS):
            bg_t = bm_t[g * SSD_STATE:(g + 1) * SSD_STATE, :]
            cb = cbs[g]
            c2 = jnp.where(lo_q, cm, cm_sw) if g == 0 else jnp.where(lo_q, cm_sw, cm)
            for jj in range(SSD_HEADS_PER_GROUP // 2):
                j = g * (SSD_HEADS_PER_GROUP // 2) + jj
                idx = c * (SSD_HEADS // 2) + j
                xp = act_ref[rows, j * LANES:(j + 1) * LANES]
                rhs = jnp.concatenate(
                    [jnp.where(lo_q, xp, 0.0), jnp.where(lo_q, 0.0, xp)], axis=0).astype(BF16)
                diag, off, w_rows, e_last = [], [], [], []
                for h in (2 * j, 2 * j + 1):
                    cs_col = jnp.broadcast_to(cs[:, h:h + 1], (SSD_CHUNK, SSD_CHUNK))
                    cs_row = cs_t[h:h + 1, :]
                    dt_row = dt_t[h:h + 1, :]
                    decay = jnp.exp(jnp.where(causal, cs_col - cs_row, -jnp.inf))
                    diag.append((cb * dt_row) * decay)
                    off.append(jnp.exp(cs_col))
                    cl = cs_t[h:h + 1, SSD_CHUNK - 1:SSD_CHUNK]
                    w_rows.append(jnp.exp(cl - cs_row) * dt_row)
                    e_last.append(jnp.exp(cl))
                y_ref[rows, j * LANES:(j + 1) * LANES] = _dot(jnp.concatenate(diag, axis=1).astype(BF16), rhs)
                lhs2 = jnp.concatenate([bg_t * w_rows[0], bg_t * w_rows[1]], axis=1).astype(BF16)
                sinc_ref[idx] = _dot(lhs2, rhs)
                loff_ref[idx] = (c2 * jnp.where(lo_q, off[0], off[1])).astype(BF16)
                elast_ref[idx] = jnp.where(lo_e, e_last[0], e_last[1])

    n_chunks = ts // SSD_CHUNK
    prologues = [chunk_prologue(c) for c in range(n_chunks)]

    g_c = _dot(nb, win_ref[:, COL_GC:COL_GC + SC_WIDTH])
    v = _dot(nb, win_ref[:, COL_V:COL_V + SC_WIDTH])
    u = g_c * v
    cv_ref[HALO:HALO + ts, :] = u
    conv3 = scw_ref[2:3, :] * u
    for i in range(2):
        conv3 = conv3 + scw_ref[i:i + 1, :] * cv_ref[HALO - 2 + i:HALO - 2 + i + ts, :]
    g_b = _dot(nb, win_ref[:, COL_GB:COL_GB + SC_WIDTH])
    out_sc = _dot((g_b * conv3).astype(BF16), wout_ref[SSD_WIDTH:, :])
    gate_z = _silu(_dot(nb, win_ref[:, COL_Z:COL_Z + SSD_WIDTH]))

    for c in range(n_chunks):
        chunk_body(c, prologues[c])

    for c in range(n_chunks):
        for j in range(SSD_HEADS // 2):
            idx = c * (SSD_HEADS // 2) + j
            stp = state_ref[j]
            rhs_st = jnp.concatenate(
                [jnp.where(lo_n, stp, 0.0), jnp.where(lo_n, 0.0, stp)], axis=0).astype(BF16)
            y_ref[c * SSD_CHUNK:(c + 1) * SSD_CHUNK, j * LANES:(j + 1) * LANES] += _dot(loff_ref[idx], rhs_st)
            state_ref[j] = elast_ref[idx][0:1, :] * stp + sinc_ref[idx]

    y = y_ref[...] + act_ref[:, 0:SSD_WIDTH] * dskip_ref[...]
    gated = y * gate_z
    gw = SSD_WIDTH // SSD_GROUPS
    y_ssd = jnp.concatenate(
        [_rms(gated[:, g * gw:(g + 1) * gw], gate_ref[:, g * gw:(g + 1) * gw]) for g in range(SSD_GROUPS)],
        axis=-1).astype(BF16)
    h_ref[0] = x + out_sc + _dot(y_ssd, wout_ref[0:SSD_WIDTH, :])


def _const_spec(shape):
    nd = len(shape)
    return pl.BlockSpec(shape, lambda *_: (0,) * nd)


def _mixer(x, gmix, win, cw, cb, dtb, a_row, dskip, gate, scw, wout, tri):
    bsz, seq, d = x.shape
    ts = SEQ_TILE
    consts = (gmix, win, cw, cb, dtb, a_row, dskip, gate, scw, wout, tri)
    return pl.pallas_call(
        _mixer_kernel,
        grid=(bsz, seq // ts),
        in_specs=[pl.BlockSpec((1, ts, d), lambda b, s: (b, s, 0))] + [_const_spec(c.shape) for c in consts],
        out_specs=pl.BlockSpec((1, ts, d), lambda b, s: (b, s, 0)),
        out_shape=jax.ShapeDtypeStruct((bsz, seq, d), F32),
        scratch_shapes=[
            pltpu.VMEM((ts + HALO, XBC_WIDTH), F32),
            pltpu.VMEM((ts + HALO, SC_WIDTH), F32),
            pltpu.VMEM((ts, XBC_WIDTH), F32),
            pltpu.VMEM((ts, LANES), F32),
            pltpu.VMEM((ts, SSD_WIDTH), F32),
            pltpu.VMEM((SSD_HEADS // 2, SSD_STATE, 2 * SSD_HEAD_DIM), F32),
            pltpu.VMEM((ts // SSD_CHUNK * SSD_HEADS // 2, SSD_STATE, 2 * SSD_HEAD_DIM), F32),
            pltpu.VMEM((ts // SSD_CHUNK * SSD_HEADS // 2, SSD_CHUNK, 2 * SSD_STATE), BF16),
            pltpu.VMEM((ts // SSD_CHUNK * SSD_HEADS // 2, SUBLANES, 2 * SSD_HEAD_DIM), F32),
        ],
        compiler_params=pltpu.CompilerParams(
            dimension_semantics=("arbitrary", "arbitrary"), vmem_limit_bytes=VMEM_LIMIT_BYTES),
        name="mixer",
    )(x, *consts)


def _kv_kernel(mem_ref, g_ref, wkv_ref, kt_ref, v_ref):
    d = mem_ref.shape[2]
    mn = _rms(mem_ref[0], g_ref[...]).astype(BF16)
    kv = _dot(mn, wkv_ref[...])
    kt_ref[0] = (kv[:, :d] * (XA_HEAD_DIM ** -0.5)).T.astype(BF16)
    v_ref[0] = kv[:, d:].astype(BF16)


def _memory_kv(mem, g, wkv):
    bsz, m, d = mem.shape
    return pl.pallas_call(
        _kv_kernel,
        grid=(bsz,),
        in_specs=[pl.BlockSpec((1, m, d), lambda b: (b, 0, 0)), _const_spec(g.shape), _const_spec(wkv.shape)],
        out_specs=[pl.BlockSpec((1, d, m), lambda b: (b, 0, 0)), pl.BlockSpec((1, m, d), lambda b: (b, 0, 0))],
        out_shape=[jax.ShapeDtypeStruct((bsz, d, m), BF16), jax.ShapeDtypeStruct((bsz, m, d), BF16)],
        compiler_params=pltpu.CompilerParams(
            dimension_semantics=("arbitrary",), vmem_limit_bytes=VMEM_LIMIT_BYTES),
        name="memory_kv",
    )(mem, g, wkv)


def _xattn_router_kernel(h1_ref, gx_ref, wq_ref, kt_ref, v_ref, wo_ref, gm_ref, wr_ref, br_ref, upper_ref,
                         lower_ref, h2_ref, xs_ref, rinfo_ref, seg_ref, counts_ref, zero_ref, carry_ref):
    ts = h1_ref.shape[1]
    first = jnp.logical_and(pl.program_id(0) == 0, pl.program_id(1) == 0)

    @pl.when(first)
    def _():
        carry_ref[...] = jnp.zeros(carry_ref.shape, F32)

    h1 = h1_ref[0]
    q = _dot(_rms(h1, gx_ref[...]).astype(BF16), wq_ref[...]).astype(BF16)
    outs = []
    for h in range(XA_HEADS):
        hs = slice(h * XA_HEAD_DIM, (h + 1) * XA_HEAD_DIM)
        sc = _dot(q[:, hs], kt_ref[0, hs, :])
        p = jnp.exp(sc - jnp.max(sc, axis=-1, keepdims=True))
        o = _dot(p.astype(BF16), v_ref[0, :, hs]) / jnp.sum(p, axis=-1, keepdims=True)
        outs.append(o.astype(BF16))
    h2 = h1 + _dot(jnp.concatenate(outs, axis=-1), wo_ref[...])
    h2_ref[0] = h2

    n3 = _rms(h2, gm_ref[...])
    n3b = n3.astype(BF16)
    half = n3.shape[1] // 2
    n3r = n3b.astype(F32)
    zero_ref[...] = jnp.zeros(zero_ref.shape, zero_ref.dtype)

    n3_lo = (n3 - n3r).astype(BF16)
    lg2 = _dot(n3b, wr_ref[...])
    logits = lg2[:, :LANES] + lg2[:, LANES:] + _dot(n3_lo, wr_ref[:, :LANES]) + br_ref[...]
    lt = logits.T

    l0, l1, l2, l3 = (lt[ROUTER_GROUP_ROW + i:ROUTER_GROUP_ROW + i + 1, :] for i in range(N_GROUPS_MOE))
    m = jnp.maximum(jnp.maximum(l0, l1), jnp.maximum(l2, l3))
    gidx = jnp.where(l0 == m, 0, jnp.where(l1 == m, 1, jnp.where(l2 == m, 2, 3)))
    g_w = 1.0 / (jnp.exp(l0 - m) + jnp.exp(l1 - m) + jnp.exp(l2 - m) + jnp.exp(l3 - m))

    def group_rows(g):
        r = ROUTER_EXPERT_ROW + g * EXPERTS_PER_GROUP
        return lt[r:r + EXPERTS_PER_GROUP, :]

    el = jnp.where(gidx == 0, group_rows(0),
                   jnp.where(gidx == 1, group_rows(1), jnp.where(gidx == 2, group_rows(2), group_rows(3))))
    sub = lax.broadcasted_iota(jnp.int32, el.shape, 0)
    m1 = jnp.max(el, axis=0, keepdims=True)
    i1 = jnp.min(jnp.where(el == m1, sub, EXPERTS_PER_GROUP), axis=0, keepdims=True)
    el2 = jnp.where(sub == i1, -jnp.inf, el)
    m2 = jnp.max(el2, axis=0, keepdims=True)
    i2 = jnp.min(jnp.where(el2 == m2, sub, EXPERTS_PER_GROUP), axis=0, keepdims=True)
    e2 = jnp.exp(m2 - m1)
    w1 = 1.0 / (1.0 + e2)
    w2 = e2 / (1.0 + e2)
    e_a = gidx * EXPERTS_PER_GROUP + i1
    e_b = gidx * EXPERTS_PER_GROUP + i2

    eio = lax.broadcasted_iota(jnp.int32, (N_EXPERTS, ts), 0)
    oh_a = (eio == e_a).astype(F32)
    oh_b = (eio == e_b).astype(F32)
    prefix = _dot(jnp.concatenate([oh_a, oh_b], axis=0).astype(BF16), upper_ref[...])
    tot_a = jnp.sum(oh_a, axis=1, keepdims=True)
    tot_b = jnp.sum(oh_b, axis=1, keepdims=True)
    groups = jnp.broadcast_to(jnp.floor((tot_a + tot_b + (ROW_GROUP - 1)) * (1.0 / ROW_GROUP)),
                              (N_EXPERTS, LANES))
    first_group = _dot(lower_ref[...], groups.astype(BF16))
    base = ROW_GROUP * first_group[:, 0:1]
    pos_a = jnp.sum(oh_a * (base + prefix[:N_EXPERTS]), axis=0, keepdims=True)
    pos_b = jnp.sum(oh_b * (base + tot_a + prefix[N_EXPERTS:]), axis=0, keepdims=True)

    carry = carry_ref[...]
    lane = lax.broadcasted_iota(jnp.int32, (N_EXPERTS, LANES), 1)
    seg_ref[...] = jnp.where(lane == 0, groups, jnp.where(lane == 1, first_group,
                                                          jnp.where(lane == 2, carry, 0.0)))
    carry_ref[...] = carry + groups
    counts_ref[...] = carry + groups

    n_local = xs_ref.shape[0]
    riota = lax.broadcasted_iota(jnp.int32, (n_local, ts), 0)
    sel_a = riota == pos_a.astype(jnp.int32)
    sel_b = riota == pos_b.astype(jnp.int32)
    perm = (jnp.where(sel_a, 1.0, 0.0) + jnp.where(sel_b, 1.0, 0.0)).astype(BF16)
    xs = _dot(perm, n3b)
    lo = lax.shift_right_logical(pltpu.bitcast(xs[:, :half], jnp.uint32), jnp.uint32(16))
    hi = pltpu.bitcast(xs[:, half:], jnp.uint32) & jnp.uint32(0xFFFF0000)
    xs_ref[:, :half] = lo | hi
    w_sorted = jnp.sum(jnp.where(sel_a, g_w * w1, 0.0) + jnp.where(sel_b, g_w * w2, 0.0),
                       axis=1, keepdims=True)
    xs_ref[:, half:] = pltpu.bitcast(jnp.broadcast_to(w_sorted, (n_local, LANES)), jnp.uint32)

    res = jnp.concatenate([pos_a, pos_b, jnp.zeros((LANES - 2, ts), F32)], axis=0)
    rinfo_ref[...] = res.T


def _xattn_router(h1, gx, wq, kt, v, wo, gm, wr, br, upper, lower, n_rows):
    bsz, seq, d = h1.shape
    ts = SEQ_TILE
    m = v.shape[1]
    nt = seq // ts
    n_tiles = bsz * nt
    zero_rows = n_rows // n_tiles
    assert zero_rows * n_tiles == n_rows and zero_rows % SUBLANES == 0
    tok_spec = pl.BlockSpec((1, ts, d), lambda b, s: (b, s, 0))
    return pl.pallas_call(
        _xattn_router_kernel,
        grid=(bsz, nt),
        in_specs=[tok_spec, _const_spec(gx.shape), _const_spec(wq.shape),
                  pl.BlockSpec((1, d, m), lambda b, s: (b, 0, 0)),
                  pl.BlockSpec((1, m, d), lambda b, s: (b, 0, 0)),
                  _const_spec(wo.shape), _const_spec(gm.shape), _const_spec(wr.shape), _const_spec(br.shape),
                  _const_spec(upper.shape), _const_spec(lower.shape)],
        out_specs=[tok_spec,
                   pl.BlockSpec((LOCAL_ROWS, ROW_WORDS), lambda b, s: (b * nt + s, 0)),
                   pl.BlockSpec((ts, LANES), lambda b, s: (b * nt + s, 0)),
                   pl.BlockSpec((N_EXPERTS, LANES), lambda b, s: (b * nt + s, 0)),
                   pl.BlockSpec((N_EXPERTS, LANES), lambda b, s: (0, 0)),
                   pl.BlockSpec((zero_rows, ROW_WORDS), lambda b, s: (b * nt + s, 0))],
        out_shape=[jax.ShapeDtypeStruct((bsz, seq, d), F32),
                   jax.ShapeDtypeStruct((n_tiles * LOCAL_ROWS, ROW_WORDS), jnp.uint32),
                   jax.ShapeDtypeStruct((bsz * seq, LANES), F32),
                   jax.ShapeDtypeStruct((n_tiles * N_EXPERTS, LANES), F32),
                   jax.ShapeDtypeStruct((N_EXPERTS, LANES), F32),
                   jax.ShapeDtypeStruct((n_rows, ROW_WORDS), jnp.uint32)],
        scratch_shapes=[pltpu.VMEM((N_EXPERTS, LANES), F32)],
        compiler_params=pltpu.CompilerParams(
            dimension_semantics=("arbitrary", "arbitrary"), vmem_limit_bytes=VMEM_LIMIT_BYTES),
        name="xattn_router",
    )(h1, gx, wq, kt, v, wo, gm, wr, br, upper, lower)


TAB_COUNT, TAB_DST, TAB_LOCAL = (k * N_EXPERTS for k in range(3))


def _group_rows(g):
    return pl.ds(pl.multiple_of(g * ROW_GROUP, ROW_GROUP), ROW_GROUP)


def _tile_groups(tab_ref):
    return lax.fori_loop(0, N_EXPERTS, lambda e, c: c + tab_ref[0, 0, TAB_COUNT + e], 0)


def _dispatch_kernel(tab_ref, xs_ref, buf_in_ref, buf_ref, sem):
    del buf_in_ref

    def group_copy(src, dst):
        return pltpu.make_async_copy(xs_ref.at[_group_rows(src)], buf_ref.at[_group_rows(dst)], sem)

    def expert(e, c):
        src0 = tab_ref[0, 0, TAB_LOCAL + e]
        dst0 = tab_ref[0, 0, TAB_DST + e]

        def start(k, c2):
            group_copy(src0 + k, dst0 + k).start()
            return c2
        return lax.fori_loop(0, tab_ref[0, 0, TAB_COUNT + e], start, c)

    lax.fori_loop(0, N_EXPERTS, expert, 0)

    def wait(k, c):
        group_copy(0, 0).wait()
        return c
    lax.fori_loop(0, _tile_groups(tab_ref), wait, 0)


def _dispatch(table, xs, buf0):
    n_tiles = table.shape[0]
    return pl.pallas_call(
        _dispatch_kernel,
        grid=(n_tiles,),
        in_specs=[pl.BlockSpec((1, 1, table.shape[2]), lambda i: (i, 0, 0), memory_space=pltpu.SMEM),
                  pl.BlockSpec((LOCAL_ROWS, xs.shape[1]), lambda i: (i, 0)),
                  pl.BlockSpec(memory_space=pl.ANY)],
        out_specs=pl.BlockSpec(memory_space=pl.ANY),
        out_shape=jax.ShapeDtypeStruct(buf0.shape, buf0.dtype),
        scratch_shapes=[pltpu.SemaphoreType.DMA],
        input_output_aliases={2: 0},
        compiler_params=pltpu.CompilerParams(
            dimension_semantics=("arbitrary",), vmem_limit_bytes=VMEM_LIMIT_BYTES),
        name="dispatch",
    )(table, xs, buf0)


def _expert_kernel(be_ref, nused_ref, xb_ref, wg_ref, wu_ref, wd_ref, y_ref, wgu_s, wd_s):
    i = pl.program_id(0)
    used = i < nused_ref[0]

    @pl.when(jnp.logical_and(used, jnp.logical_or(i == 0, be_ref[i] != be_ref[jnp.maximum(i - 1, 0)])))
    def _():
        wgu_s[:, :D_EXPERT] = wg_ref[0].astype(BF16)
        wgu_s[:, D_EXPERT:] = wu_ref[0].astype(BF16)
        wd_s[...] = wd_ref[0].astype(BF16)

    @pl.when(used)
    def _():
        half = xb_ref.shape[1] - LANES
        w = xb_ref[:, :half]
        x_lo = pltpu.bitcast(lax.shift_left(w, jnp.uint32(16)), F32).astype(BF16)
        x_hi = pltpu.bitcast(w & jnp.uint32(0xFFFF0000), F32).astype(BF16)
        gu = _dot(x_lo, wgu_s[:half, :]) + _dot(x_hi, wgu_s[half:, :])
        mid = (_silu(gu[:, :D_EXPERT]) * gu[:, D_EXPERT:]).astype(BF16)
        comb = pltpu.bitcast(xb_ref[:, half:half + LANES], F32)[:, 0:1]
        y_ref[...] = _dot(mid, wd_s[...]) * comb

    @pl.when(jnp.logical_not(used))
    def _():
        y_ref[...] = jnp.zeros(y_ref.shape, F32)


def _experts(block_expert, n_used, xbuf, w_gate, w_up, w_down):
    w = xbuf.shape[1]
    bm = EXPERT_BLOCK
    n_blocks = block_expert.shape[0]
    n_rows = n_blocks * bm
    d, de = w_gate.shape[1], w_gate.shape[2]
    grid_spec = pltpu.PrefetchScalarGridSpec(
        num_scalar_prefetch=2,
        grid=(n_blocks,),
        in_specs=[pl.BlockSpec((bm, w), lambda i, be, nu: (i, 0)),
                  pl.BlockSpec((1, d, de), lambda i, be, nu: (be[i], 0, 0)),
                  pl.BlockSpec((1, d, de), lambda i, be, nu: (be[i], 0, 0)),
                  pl.BlockSpec((1, de, d), lambda i, be, nu: (be[i], 0, 0))],
        out_specs=pl.BlockSpec((bm, d), lambda i, be, nu: (i, 0)),
        scratch_shapes=[pltpu.VMEM((d, 2 * de), BF16), pltpu.VMEM((de, d), BF16)],
    )
    return pl.pallas_call(
        _expert_kernel,
        grid_spec=grid_spec,
        out_shape=jax.ShapeDtypeStruct((n_rows, d), F32),
        compiler_params=pltpu.CompilerParams(
            dimension_semantics=("arbitrary",), vmem_limit_bytes=VMEM_LIMIT_BYTES),
        name="experts",
    )(block_expert, n_used, xbuf, w_gate, w_up, w_down)


def _combine_kernel(tab_ref, tab_next_ref, h2_ref, rinfo_ref, gf_ref, ybuf_ref, out_ref, ys_ref, sem):
    tr = h2_ref.shape[0]
    n_local = ys_ref.shape[1]
    i = pl.program_id(0)
    slot = i % 2

    def group_copy(s, src, dst):
        return pltpu.make_async_copy(ybuf_ref.at[_group_rows(src)], ys_ref.at[s, _group_rows(dst)], sem.at[s])

    def gather(tab, s):
        ys_ref[s] = jnp.zeros(ys_ref.shape[1:], F32)

        def expert(e, c):
            src0 = tab[0, 0, TAB_DST + e]
            dst0 = tab[0, 0, TAB_LOCAL + e]

            def start(k, c2):
                group_copy(s, src0 + k, dst0 + k).start()
                return c2
            return lax.fori_loop(0, tab[0, 0, TAB_COUNT + e], start, c)
        lax.fori_loop(0, N_EXPERTS, expert, 0)

    @pl.when(i == 0)
    def _():
        gather(tab_ref, 0)

    @pl.when(i + 1 < pl.num_programs(0))
    def _():
        gather(tab_next_ref, 1 - slot)

    def wait(k, c):
        group_copy(slot, 0, 0).wait()
        return c
    lax.fori_loop(0, _tile_groups(tab_ref), wait, 0)

    info = rinfo_ref[...]
    ciota = lax.broadcasted_iota(jnp.int32, (tr, n_local), 1)
    pick = (jnp.where(ciota == info[:, 0:1].astype(jnp.int32), 1.0, 0.0)
            + jnp.where(ciota == info[:, 1:2].astype(jnp.int32), 1.0, 0.0)).astype(BF16)
    h = h2_ref[...] + _dot(pick, ys_ref[slot].astype(BF16))
    out_ref[...] = _rms(h, gf_ref[...])


def _combine(table, h2, rinfo, gf, ybuf):
    t, d = h2.shape
    tr = SEQ_TILE
    n_tiles = t // tr
    tab_spec = (1, 1, table.shape[2])
    return pl.pallas_call(
        _combine_kernel,
        grid=(n_tiles,),
        in_specs=[pl.BlockSpec(tab_spec, lambda i: (i, 0, 0), memory_space=pltpu.SMEM),
                  pl.BlockSpec(tab_spec, lambda i: (jnp.minimum(i + 1, n_tiles - 1), 0, 0),
                               memory_space=pltpu.SMEM),
                  pl.BlockSpec((tr, d), lambda i: (i, 0)),
                  pl.BlockSpec((tr, LANES), lambda i: (i, 0)),
                  _const_spec(gf.shape),
                  pl.BlockSpec(memory_space=pl.ANY)],
        out_specs=pl.BlockSpec((tr, d), lambda i: (i, 0)),
        out_shape=jax.ShapeDtypeStruct((t, d), F32),
        scratch_shapes=[pltpu.VMEM((2, LOCAL_ROWS, d), F32), pltpu.SemaphoreType.DMA((2,))],
        compiler_params=pltpu.CompilerParams(
            dimension_semantics=("arbitrary",), vmem_limit_bytes=VMEM_LIMIT_BYTES),
        name="combine",
    )(table, table, h2, rinfo, gf, ybuf)


def _pad_lanes(v, fill=0.0):
    return jnp.pad(v, (0, LANES - v.shape[0]), constant_values=fill)[None, :]


def kernel(x, mem, norm_mem, norm_mix, w_in, conv_ssd_w, conv_ssd_b, dt_bias, a_log, d_skip, norm_ssd_gate,
           conv_short_w, w_out, norm_xattn, w_q, w_kv, w_o, norm_moe, w_router_group, b_router_group,
           w_router_expert, b_router_expert, w_gate, w_up, w_down, norm_final):
    bsz, seq, d = x.shape
    t = bsz * seq
    assert w_in.shape[0] == 1, "single-layer model"
    assert d == D_MODEL and seq % SEQ_TILE == 0 and t % ROW_TILE == 0

    tri = jnp.tril(jnp.ones((SSD_CHUNK, SSD_CHUNK), F32))
    upper = jnp.triu(jnp.ones((SEQ_TILE, SEQ_TILE), F32), k=1).astype(BF16)

    o2 = SSD_WIDTH + XBC_WIDTH
    o3 = o2 + SSD_HEADS
    wl = w_in[0]
    win = jnp.concatenate(
        [wl[:, :o2], wl[:, o3:], wl[:, o2:o3], jnp.zeros((d, LANES - SSD_HEADS), F32)], axis=1).astype(BF16)
    h1 = _mixer(x, norm_mix, win, conv_ssd_w[0], conv_ssd_b, _pad_lanes(dt_bias[0]),
                _pad_lanes(-jnp.exp(a_log[0])), jnp.repeat(d_skip[0], SSD_HEAD_DIM)[None, :], norm_ssd_gate,
                conv_short_w[0], w_out[0].astype(BF16), tri)

    kt, v = _memory_kv(mem, norm_mem[None, :], w_kv[0].astype(BF16))
    wr = jnp.zeros((d, LANES), F32)
    wr = wr.at[:, ROUTER_GROUP_ROW:ROUTER_GROUP_ROW + N_GROUPS_MOE].set(w_router_group[0])
    wr = wr.at[:, ROUTER_EXPERT_ROW:ROUTER_EXPERT_ROW + N_EXPERTS].set(w_router_expert[0])
    wr_hi = wr.astype(BF16)
    wr_lo = (wr - wr_hi.astype(F32)).astype(BF16)
    br = jnp.zeros((LANES,), F32)
    br = br.at[ROUTER_GROUP_ROW:ROUTER_GROUP_ROW + N_GROUPS_MOE].set(b_router_group[0])
    br = br.at[ROUTER_EXPERT_ROW:ROUTER_EXPERT_ROW + N_EXPERTS].set(b_router_expert[0])
    lower = jnp.tril(jnp.ones((N_EXPERTS, N_EXPERTS), F32), k=-1).astype(BF16)
    n_tiles = t // SEQ_TILE
    bgroups = EXPERT_BLOCK // ROW_GROUP
    max_groups = (2 * t + n_tiles * N_EXPERTS * (ROW_GROUP - 1)) // ROW_GROUP
    n_blocks = -(-max_groups // bgroups) + N_EXPERTS
    even = max(1, (n_tiles * SUBLANES) // math.gcd(n_tiles * SUBLANES, EXPERT_BLOCK))
    n_blocks = -(-n_blocks // even) * even
    h2, xs, rinfo, seg, counts, xbuf0 = _xattn_router(
        h1, norm_xattn, w_q[0].astype(BF16), kt, v, w_o[0].astype(BF16), norm_moe,
        jnp.concatenate([wr_hi, wr_lo], axis=1), br[None, :], upper, lower, n_blocks * EXPERT_BLOCK)

    seg = seg.reshape(n_tiles, N_EXPERTS, LANES)[:, :, 0:3].astype(jnp.int32)
    seg_groups, seg_local, seg_before = seg[:, :, 0], seg[:, :, 1], seg[:, :, 2]
    sizes = counts[:, 0].astype(jnp.int32)
    padded = ((sizes + bgroups - 1) // bgroups) * bgroups
    pad_end = jnp.cumsum(padded)
    pad_start = pad_end - padded
    seg_dst = pad_start[None, :] + seg_before
    table = jnp.concatenate([seg_groups, seg_dst, seg_local], axis=1)[:, None, :]
    block_group = jnp.arange(n_blocks, dtype=jnp.int32) * bgroups
    block_expert = jnp.minimum(jnp.sum((pad_end[None, :] <= block_group[:, None]).astype(jnp.int32), axis=-1),
                               N_EXPERTS - 1)
    n_used = (pad_end[-1:] // bgroups).astype(jnp.int32)

    xbuf = _dispatch(table, xs, xbuf0)
    ybuf = _experts(block_expert, n_used, xbuf, w_gate[0], w_up[0], w_down[0])
    out = _combine(table, h2.reshape(t, d), rinfo, norm_final[None, :], ybuf)
    return out.reshape(bsz, seq, d)
```

```python
import math

import jax
import jax.numpy as jnp
from jax import lax
from jax.experimental import pallas as pl
from jax.experimental.pallas import tpu as pltpu

F32 = jnp.float32
BF16 = jnp.bfloat16
EPS = 1e-6

LANES = 128
SUBLANES = 8
VMEM_LIMIT_BYTES = 56 * 1024 * 1024

D_MODEL = 1024
SSD_WIDTH = 512
SSD_HEAD_DIM = 64
SSD_HEADS = 8
SSD_GROUPS = 2
SSD_HEADS_PER_GROUP = 4
SSD_STATE = 64
SSD_CHUNK = 128
XBC_WIDTH = SSD_WIDTH + 2 * SSD_GROUPS * SSD_STATE
SC_WIDTH = 512
XA_HEADS = 4
XA_HEAD_DIM = 256
N_GROUPS_MOE = 4
EXPERTS_PER_GROUP = 8
N_EXPERTS = 32
D_EXPERT = 512

COL_Z = 0
COL_XBC = COL_Z + SSD_WIDTH
COL_GB = COL_XBC + XBC_WIDTH
COL_GC = COL_GB + SC_WIDTH
COL_V = COL_GC + SC_WIDTH
COL_DT = COL_V + SC_WIDTH
PROJ_WIDTH = COL_DT + LANES

SEQ_TILE = 512
ROW_TILE = 512
EXPERT_BLOCK = 512
HALO = SUBLANES
ROW_GROUP = SUBLANES
LOCAL_ROWS = 2 * SEQ_TILE + N_EXPERTS * ROW_GROUP
ROW_WORDS = D_MODEL // 2 + LANES

ROUTER_GROUP_ROW = 0
ROUTER_EXPERT_ROW = SUBLANES


def _rms(x, g):
    return x * lax.rsqrt(jnp.mean(x * x, axis=-1, keepdims=True) + EPS) * g


def _silu(x):
    return x * jax.nn.sigmoid(x)


def _dot(a, b):
    return jnp.dot(a, b, preferred_element_type=F32)


def _mixer_kernel(x_ref, gmix_ref, win_ref, cw_ref, cb_ref, dtb_ref, a_ref, dskip_ref, gate_ref,
                  scw_ref, wout_ref, tri_ref, h_ref,
                  conv_ref, cv_ref, act_ref, dt_ref, y_ref, state_ref, sinc_ref, loff_ref, elast_ref):
    ts = x_ref.shape[1]
    s = pl.program_id(1)
    x = x_ref[0]
    nb = _rms(x, gmix_ref[...]).astype(BF16)

    @pl.when(s == 0)
    def _():
        conv_ref[0:HALO, :] = jnp.zeros((HALO, XBC_WIDTH), F32)
        cv_ref[0:HALO, :] = jnp.zeros((HALO, SC_WIDTH), F32)
        state_ref[...] = jnp.zeros(state_ref.shape, F32)

    @pl.when(s > 0)
    def _():
        conv_ref[0:HALO, :] = conv_ref[ts:ts + HALO, :]
        cv_ref[0:HALO, :] = cv_ref[ts:ts + HALO, :]

    xbc = _dot(nb, win_ref[:, COL_XBC:COL_XBC + XBC_WIDTH])
    conv_ref[HALO:HALO + ts, :] = xbc
    acc = cb_ref[...] + cw_ref[3:4, :] * xbc
    for i in range(3):
        acc = acc + cw_ref[i:i + 1, :] * conv_ref[HALO - 3 + i:HALO - 3 + i + ts, :]
    act_ref[...] = _silu(acc)

    dtr = _dot(nb, win_ref[:, COL_DT:COL_DT + LANES]) + dtb_ref[...]
    dt_ref[...] = jnp.maximum(dtr, 0.0) + jnp.log1p(jnp.exp(-jnp.abs(dtr)))

    row = lax.broadcasted_iota(jnp.int32, (SSD_CHUNK, SSD_CHUNK), 0)
    col = lax.broadcasted_iota(jnp.int32, (SSD_CHUNK, SSD_CHUNK), 1)
    causal = row >= col
    lo_q = col < SSD_HEAD_DIM
    lo_n = lax.broadcasted_iota(jnp.int32, (SSD_STATE, SSD_CHUNK), 1) < SSD_HEAD_DIM
    lo_e = lax.broadcasted_iota(jnp.int32, (SUBLANES, SSD_CHUNK), 1) < SSD_HEAD_DIM

    def chunk_prologue(c):
        rows = slice(c * SSD_CHUNK, (c + 1) * SSD_CHUNK)
        dt = dt_ref[rows, :]
        adt = dt * a_ref[...]
        cs = jnp.dot(tri_ref[...], adt, preferred_element_type=F32,
                     precision=lax.Precision.HIGHEST)
        bm = act_ref[rows, SSD_WIDTH:SSD_WIDTH + SSD_GROUPS * SSD_STATE]
        cm = act_ref[rows, SSD_WIDTH + SSD_GROUPS * SSD_STATE:XBC_WIDTH]
        cbs = []
        for g in range(SSD_GROUPS):
            cg = cm[:, g * SSD_STATE:(g + 1) * SSD_STATE]
            bg = bm[:, g * SSD_STATE:(g + 1) * SSD_STATE]
            cbs.append(lax.dot_general(cg.astype(BF16), bg.astype(BF16), (((1,), (1,)), ((), ())),
                                       preferred_element_type=F32))
        return cs, cs.T, dt.T, bm.T, cm, pltpu.roll(cm, SSD_STATE, axis=1), cbs

    def chunk_body(c, prologue):
        rows = slice(c * SSD_CHUNK, (c + 1) * SSD_CHUNK)
        cs, cs_t, dt_t, bm_t, cm, cm_sw, cbs = prologue
        for g in range(SSD_GROUPS):
            bg_t = bm_t[g * SSD_STATE:(g + 1) * SSD_STATE, :]
            cb = cbs[g]
            c2 = jnp.where(lo_q, cm, cm_sw) if g == 0 else jnp.where(lo_q, cm_sw, cm)
            for jj in range(SSD_HEADS_PER_GROUP // 2):
                j = g * (SSD_HEADS_PER_GROUP // 2) + jj
                idx = c * (SSD_HEADS // 2) + j
                xp = act_ref[rows, j * LANES:(j + 1) * LANES]
                rhs = jnp.concatenate(
                    [jnp.where(lo_q, xp, 0.0), jnp.where(lo_q, 0.0, xp)], axis=0).astype(BF16)
                diag, off, w_rows, e_last = [], [], [], []
                for h in (2 * j, 2 * j + 1):
                    cs_col = jnp.broadcast_to(cs[:, h:h + 1], (SSD_CHUNK, SSD_CHUNK))
                    cs_row = cs_t[h:h + 1, :]
                    dt_row = dt_t[h:h + 1, :]
                    decay = jnp.exp(jnp.where(causal, cs_col - cs_row, -jnp.inf))
                    diag.append((cb * dt_row) * decay)
                    off.append(jnp.exp(cs_col))
                    cl = cs_t[h:h + 1, SSD_CHUNK - 1:SSD_CHUNK]
                    w_rows.append(jnp.exp(cl - cs_row) * dt_row)
                    e_last.append(jnp.exp(cl))
                y_ref[rows, j * LANES:(j + 1) * LANES] = _dot(jnp.concatenate(diag, axis=1).astype(BF16), rhs)
                lhs2 = jnp.concatenate([bg_t * w_rows[0], bg_t * w_rows[1]], axis=1).astype(BF16)
                sinc_ref[idx] = _dot(lhs2, rhs)
                loff_ref[idx] = (c2 * jnp.where(lo_q, off[0], off[1])).astype(BF16)
                elast_ref[idx] = jnp.where(lo_e, e_last[0], e_last[1])

    n_chunks = ts // SSD_CHUNK
    prologues = [chunk_prologue(c) for c in range(n_chunks)]

    g_c = _dot(nb, win_ref[:, COL_GC:COL_GC + SC_WIDTH])
    v = _dot(nb, win_ref[:, COL_V:COL_V + SC_WIDTH])
    u = g_c * v
    cv_ref[HALO:HALO + ts, :] = u
    conv3 = scw_ref[2:3, :] * u
    for i in range(2):
        conv3 = conv3 + scw_ref[i:i + 1, :] * cv_ref[HALO - 2 + i:HALO - 2 + i + ts, :]
    g_b = _dot(nb, win_ref[:, COL_GB:COL_GB + SC_WIDTH])
    out_sc = _dot((g_b * conv3).astype(BF16), wout_ref[SSD_WIDTH:, :])
    gate_z = _silu(_dot(nb, win_ref[:, COL_Z:COL_Z + SSD_WIDTH]))

    for c in range(n_chunks):
        chunk_body(c, prologues[c])

    for c in range(n_chunks):
        for j in range(SSD_HEADS // 2):
            idx = c * (SSD_HEADS // 2) + j
            stp = state_ref[j]
            rhs_st = jnp.concatenate(
                [jnp.where(lo_n, stp, 0.0), jnp.where(lo_n, 0.0, stp)], axis=0).astype(BF16)
            y_ref[c * SSD_CHUNK:(c + 1) * SSD_CHUNK, j * LANES:(j + 1) * LANES] += _dot(loff_ref[idx], rhs_st)
            state_ref[j] = elast_ref[idx][0:1, :] * stp + sinc_ref[idx]

    y = y_ref[...] + act_ref[:, 0:SSD_WIDTH] * dskip_ref[...]
    gated = y * gate_z
    gw = SSD_WIDTH // SSD_GROUPS
    y_ssd = jnp.concatenate(
        [_rms(gated[:, g * gw:(g + 1) * gw], gate_ref[:, g * gw:(g + 1) * gw]) for g in range(SSD_GROUPS)],
        axis=-1).astype(BF16)
    h_ref[0] = x + out_sc + _dot(y_ssd, wout_ref[0:SSD_WIDTH, :])


def _const_spec(shape):
    nd = len(shape)
    return pl.BlockSpec(shape, lambda *_: (0,) * nd)


def _mixer(x, gmix, win, cw, cb, dtb, a_row, dskip, gate, scw, wout, tri):
    bsz, seq, d = x.shape
    ts = SEQ_TILE
    consts = (gmix, win, cw, cb, dtb, a_row, dskip, gate, scw, wout, tri)
    return pl.pallas_call(
        _mixer_kernel,
        grid=(bsz, seq // ts),
        in_specs=[pl.BlockSpec((1, ts, d), lambda b, s: (b, s, 0))] + [_const_spec(c.shape) for c in consts],
        out_specs=pl.BlockSpec((1, ts, d), lambda b, s: (b, s, 0)),
        out_shape=jax.ShapeDtypeStruct((bsz, seq, d), F32),
        scratch_shapes=[
            pltpu.VMEM((ts + HALO, XBC_WIDTH), F32),
            pltpu.VMEM((ts + HALO, SC_WIDTH), F32),
            pltpu.VMEM((ts, XBC_WIDTH), F32),
            pltpu.VMEM((ts, LANES), F32),
            pltpu.VMEM((ts, SSD_WIDTH), F32),
            pltpu.VMEM((SSD_HEADS // 2, SSD_STATE, 2 * SSD_HEAD_DIM), F32),
            pltpu.VMEM((ts // SSD_CHUNK * SSD_HEADS // 2, SSD_STATE, 2 * SSD_HEAD_DIM), F32),
            pltpu.VMEM((ts // SSD_CHUNK * SSD_HEADS // 2, SSD_CHUNK, 2 * SSD_STATE), BF16),
            pltpu.VMEM((ts // SSD_CHUNK * SSD_HEADS // 2, SUBLANES, 2 * SSD_HEAD_DIM), F32),
        ],
        compiler_params=pltpu.CompilerParams(
            dimension_semantics=("arbitrary", "arbitrary"), vmem_limit_bytes=VMEM_LIMIT_BYTES),
        name="mixer",
    )(x, *consts)


def _kv_kernel(mem_ref, g_ref, wkv_ref, kt_ref, v_ref):
    d = mem_ref.shape[2]
    mn = _rms(mem_ref[0], g_ref[...]).astype(BF16)
    kv = _dot(mn, wkv_ref[...])
    kt_ref[0] = (kv[:, :d] * (XA_HEAD_DIM ** -0.5)).T.astype(BF16)
    v_ref[0] = kv[:, d:].astype(BF16)


def _memory_kv(mem, g, wkv):
    bsz, m, d = mem.shape
    return pl.pallas_call(
        _kv_kernel,
        grid=(bsz,),
        in_specs=[pl.BlockSpec((1, m, d), lambda b: (b, 0, 0)), _const_spec(g.shape), _const_spec(wkv.shape)],
        out_specs=[pl.BlockSpec((1, d, m), lambda b: (b, 0, 0)), pl.BlockSpec((1, m, d), lambda b: (b, 0, 0))],
        out_shape=[jax.ShapeDtypeStruct((bsz, d, m), BF16), jax.ShapeDtypeStruct((bsz, m, d), BF16)],
        compiler_params=pltpu.CompilerParams(
            dimension_semantics=("arbitrary",), vmem_limit_bytes=VMEM_LIMIT_BYTES),
        name="memory_kv",
    )(mem, g, wkv)


def _xattn_router_kernel(h1_ref, gx_ref, wq_ref, kt_ref, v_ref, wo_ref, gm_ref, wr_ref, br_ref, upper_ref,
                         lower_ref, h2_ref, xs_ref, rinfo_ref, seg_ref, counts_ref, zero_ref, carry_ref):
    ts = h1_ref.shape[1]
    first = jnp.logical_and(pl.program_id(0) == 0, pl.program_id(1) == 0)

    @pl.when(first)
    def _():
        carry_ref[...] = jnp.zeros(carry_ref.shape, F32)

    h1 = h1_ref[0]
    q = _dot(_rms(h1, gx_ref[...]).astype(BF16), wq_ref[...]).astype(BF16)
    outs = []
    for h in range(XA_HEADS):
        hs = slice(h * XA_HEAD_DIM, (h + 1) * XA_HEAD_DIM)
        sc = _dot(q[:, hs], kt_ref[0, hs, :])
        p = jnp.exp(sc - jnp.max(sc, axis=-1, keepdims=True))
        o = _dot(p.astype(BF16), v_ref[0, :, hs]) / jnp.sum(p, axis=-1, keepdims=True)
        outs.append(o.astype(BF16))
    h2 = h1 + _dot(jnp.concatenate(outs, axis=-1), wo_ref[...])
    h2_ref[0] = h2

    n3 = _rms(h2, gm_ref[...])
    n3b = n3.astype(BF16)
    half = n3.shape[1] // 2
    n3r = n3b.astype(F32)
    zero_ref[...] = jnp.zeros(zero_ref.shape, zero_ref.dtype)

    n3_lo = (n3 - n3r).astype(BF16)
    lg2 = _dot(n3b, wr_ref[...])
    logits = lg2[:, :LANES] + lg2[:, LANES:] + _dot(n3_lo, wr_ref[:, :LANES]) + br_ref[...]
    lt = logits.T

    l0, l1, l2, l3 = (lt[ROUTER_GROUP_ROW + i:ROUTER_GROUP_ROW + i + 1, :] for i in range(N_GROUPS_MOE))
    m = jnp.maximum(jnp.maximum(l0, l1), jnp.maximum(l2, l3))
    gidx = jnp.where(l0 == m, 0, jnp.where(l1 == m, 1, jnp.where(l2 == m, 2, 3)))
    g_w = 1.0 / (jnp.exp(l0 - m) + jnp.exp(l1 - m) + jnp.exp(l2 - m) + jnp.exp(l3 - m))

    def group_rows(g):
        r = ROUTER_EXPERT_ROW + g * EXPERTS_PER_GROUP
        return lt[r:r + EXPERTS_PER_GROUP, :]

    el = jnp.where(gidx == 0, group_rows(0),
                   jnp.where(gidx == 1, group_rows(1), jnp.where(gidx == 2, group_rows(2), group_rows(3))))
    sub = lax.broadcasted_iota(jnp.int32, el.shape, 0)
    m1 = jnp.max(el, axis=0, keepdims=True)
    i1 = jnp.min(jnp.where(el == m1, sub, EXPERTS_PER_GROUP), axis=0, keepdims=True)
    el2 = jnp.where(sub == i1, -jnp.inf, el)
    m2 = jnp.max(el2, axis=0, keepdims=True)
    i2 = jnp.min(jnp.where(el2 == m2, sub, EXPERTS_PER_GROUP), axis=0, keepdims=True)
    e2 = jnp.exp(m2 - m1)
    w1 = 1.0 / (1.0 + e2)
    w2 = e2 / (1.0 + e2)
    e_a = gidx * EXPERTS_PER_GROUP + i1
    e_b = gidx * EXPERTS_PER_GROUP + i2

    eio = lax.broadcasted_iota(jnp.int32, (N_EXPERTS, ts), 0)
    oh_a = (eio == e_a).astype(F32)
    oh_b = (eio == e_b).astype(F32)
    prefix = _dot(jnp.concatenate([oh_a, oh_b], axis=0).astype(BF16), upper_ref[...])
    tot_a = jnp.sum(oh_a, axis=1, keepdims=True)
    tot_b = jnp.sum(oh_b, axis=1, keepdims=True)
    groups = jnp.broadcast_to(jnp.floor((tot_a + tot_b + (ROW_GROUP - 1)) * (1.0 / ROW_GROUP)),
                              (N_EXPERTS, LANES))
    first_group = _dot(lower_ref[...], groups.astype(BF16))
    base = ROW_GROUP * first_group[:, 0:1]
    pos_a = jnp.sum(oh_a * (base + prefix[:N_EXPERTS]), axis=0, keepdims=True)
    pos_b = jnp.sum(oh_b * (base + tot_a + prefix[N_EXPERTS:]), axis=0, keepdims=True)

    carry = carry_ref[...]
    lane = lax.broadcasted_iota(jnp.int32, (N_EXPERTS, LANES), 1)
    seg_ref[...] = jnp.where(lane == 0, groups, jnp.where(lane == 1, first_group,
                                                          jnp.where(lane == 2, carry, 0.0)))
    carry_ref[...] = carry + groups
    counts_ref[...] = carry + groups

    n_local = xs_ref.shape[0]
    riota = lax.broadcasted_iota(jnp.int32, (n_local, ts), 0)
    sel_a = riota == pos_a.astype(jnp.int32)
    sel_b = riota == pos_b.astype(jnp.int32)
    perm = (jnp.where(sel_a, 1.0, 0.0) + jnp.where(sel_b, 1.0, 0.0)).astype(BF16)
    xs = _dot(perm, n3b)
    lo = lax.shift_right_logical(pltpu.bitcast(xs[:, :half], jnp.uint32), jnp.uint32(16))
    hi = pltpu.bitcast(xs[:, half:], jnp.uint32) & jnp.uint32(0xFFFF0000)
    xs_ref[:, :half] = lo | hi
    w_sorted = jnp.sum(jnp.where(sel_a, g_w * w1, 0.0) + jnp.where(sel_b, g_w * w2, 0.0),
                       axis=1, keepdims=True)
    xs_ref[:, half:] = pltpu.bitcast(jnp.broadcast_to(w_sorted, (n_local, LANES)), jnp.uint32)

    res = jnp.concatenate([pos_a, pos_b, jnp.zeros((LANES - 2, ts), F32)], axis=0)
    rinfo_ref[...] = res.T


def _xattn_router(h1, gx, wq, kt, v, wo, gm, wr, br, upper, lower, n_rows):
    bsz, seq, d = h1.shape
    ts = SEQ_TILE
    m = v.shape[1]
    nt = seq // ts
    n_tiles = bsz * nt
    zero_rows = n_rows // n_tiles
    assert zero_rows * n_tiles == n_rows and zero_rows % SUBLANES == 0
    tok_spec = pl.BlockSpec((1, ts, d), lambda b, s: (b, s, 0))
    return pl.pallas_call(
        _xattn_router_kernel,
        grid=(bsz, nt),
        in_specs=[tok_spec, _const_spec(gx.shape), _const_spec(wq.shape),
                  pl.BlockSpec((1, d, m), lambda b, s: (b, 0, 0)),
                  pl.BlockSpec((1, m, d), lambda b, s: (b, 0, 0)),
                  _const_spec(wo.shape), _const_spec(gm.shape), _const_spec(wr.shape), _const_spec(br.shape),
                  _const_spec(upper.shape), _const_spec(lower.shape)],
        out_specs=[tok_spec,
                   pl.BlockSpec((LOCAL_ROWS, ROW_WORDS), lambda b, s: (b * nt + s, 0)),
                   pl.BlockSpec((ts, LANES), lambda b, s: (b * nt + s, 0)),
                   pl.BlockSpec((N_EXPERTS, LANES), lambda b, s: (b * nt + s, 0)),
                   pl.BlockSpec((N_EXPERTS, LANES), lambda b, s: (0, 0)),
                   pl.BlockSpec((zero_rows, ROW_WORDS), lambda b, s: (b * nt + s, 0))],
        out_shape=[jax.ShapeDtypeStruct((bsz, seq, d), F32),
                   jax.ShapeDtypeStruct((n_tiles * LOCAL_ROWS, ROW_WORDS), jnp.uint32),
                   jax.ShapeDtypeStruct((bsz * seq, LANES), F32),
                   jax.ShapeDtypeStruct((n_tiles * N_EXPERTS, LANES), F32),
                   jax.ShapeDtypeStruct((N_EXPERTS, LANES), F32),
                   jax.ShapeDtypeStruct((n_rows, ROW_WORDS), jnp.uint32)],
        scratch_shapes=[pltpu.VMEM((N_EXPERTS, LANES), F32)],
        compiler_params=pltpu.CompilerParams(
            dimension_semantics=("arbitrary", "arbitrary"), vmem_limit_bytes=VMEM_LIMIT_BYTES),
        name="xattn_router",
    )(h1, gx, wq, kt, v, wo, gm, wr, br, upper, lower)


TAB_COUNT, TAB_DST, TAB_LOCAL = (k * N_EXPERTS for k in range(3))


SEGMENT_RUN_LOG2 = 3
SEGMENT_CHUNKS = (4, 2, 1)


def _rows(first_group, n_groups):
    return pl.ds(pl.multiple_of(first_group * ROW_GROUP, ROW_GROUP), n_groups * ROW_GROUP)


def _for_each_segment_copy(tab_ref, make_copy, action):
    run = 1 << SEGMENT_RUN_LOG2

    def expert(e, c):
        n = tab_ref[0, 0, TAB_COUNT + e]
        dst0 = tab_ref[0, 0, TAB_DST + e]
        loc0 = tab_ref[0, 0, TAB_LOCAL + e]

        def full_run(k, c2):
            action(make_copy(loc0 + k * run, dst0 + k * run, run))
            return c2
        lax.fori_loop(0, lax.shift_right_logical(n, SEGMENT_RUN_LOG2), full_run, 0)
        for size in SEGMENT_CHUNKS:
            off = n & (-2 * size)

            @pl.when((n & size) != 0)
            def _():
                action(make_copy(loc0 + off, dst0 + off, size))
        return c
    lax.fori_loop(0, N_EXPERTS, expert, 0)


def _dispatch_kernel(tab_ref, xs_ref, buf_in_ref, buf_ref, sem):
    del buf_in_ref

    def make_copy(local, dst, n_groups):
        return pltpu.make_async_copy(xs_ref.at[_rows(local, n_groups)], buf_ref.at[_rows(dst, n_groups)], sem)

    _for_each_segment_copy(tab_ref, make_copy, lambda cp: cp.start())
    _for_each_segment_copy(tab_ref, make_copy, lambda cp: cp.wait())


def _dispatch(table, xs, buf0):
    n_tiles = table.shape[0]
    return pl.pallas_call(
        _dispatch_kernel,
        grid=(n_tiles,),
        in_specs=[pl.BlockSpec((1, 1, table.shape[2]), lambda i: (i, 0, 0), memory_space=pltpu.SMEM),
                  pl.BlockSpec((LOCAL_ROWS, xs.shape[1]), lambda i: (i, 0)),
                  pl.BlockSpec(memory_space=pl.ANY)],
        out_specs=pl.BlockSpec(memory_space=pl.ANY),
        out_shape=jax.ShapeDtypeStruct(buf0.shape, buf0.dtype),
        scratch_shapes=[pltpu.SemaphoreType.DMA],
        input_output_aliases={2: 0},
        compiler_params=pltpu.CompilerParams(
            dimension_semantics=("arbitrary",), vmem_limit_bytes=VMEM_LIMIT_BYTES),
        name="dispatch",
    )(table, xs, buf0)


def _expert_kernel(be_ref, nused_ref, xb_ref, wg_ref, wu_ref, wd_ref, y_ref, wgu_s, wd_s):
    i = pl.program_id(0)
    used = i < nused_ref[0]

    @pl.when(jnp.logical_and(used, jnp.logical_or(i == 0, be_ref[i] != be_ref[jnp.maximum(i - 1, 0)])))
    def _():
        wgu_s[:, :D_EXPERT] = wg_ref[0].astype(BF16)
        wgu_s[:, D_EXPERT:] = wu_ref[0].astype(BF16)
        wd_s[...] = wd_ref[0].astype(BF16)

    @pl.when(used)
    def _():
        half = xb_ref.shape[1] - LANES
        w = xb_ref[:, :half]
        x_lo = pltpu.bitcast(lax.shift_left(w, jnp.uint32(16)), F32).astype(BF16)
        x_hi = pltpu.bitcast(w & jnp.uint32(0xFFFF0000), F32).astype(BF16)
        gu = _dot(x_lo, wgu_s[:half, :]) + _dot(x_hi, wgu_s[half:, :])
        mid = (_silu(gu[:, :D_EXPERT]) * gu[:, D_EXPERT:]).astype(BF16)
        comb = pltpu.bitcast(xb_ref[:, half:half + LANES], F32)[:, 0:1]
        y_ref[...] = _dot(mid, wd_s[...]) * comb

    @pl.when(jnp.logical_not(used))
    def _():
        y_ref[...] = jnp.zeros(y_ref.shape, F32)


def _experts(block_expert, n_used, xbuf, w_gate, w_up, w_down):
    w = xbuf.shape[1]
    bm = EXPERT_BLOCK
    n_blocks = block_expert.shape[0]
    n_rows = n_blocks * bm
    d, de = w_gate.shape[1], w_gate.shape[2]
    grid_spec = pltpu.PrefetchScalarGridSpec(
        num_scalar_prefetch=2,
        grid=(n_blocks,),
        in_specs=[pl.BlockSpec((bm, w), lambda i, be, nu: (i, 0)),
                  pl.BlockSpec((1, d, de), lambda i, be, nu: (be[i], 0, 0)),
                  pl.BlockSpec((1, d, de), lambda i, be, nu: (be[i], 0, 0)),
                  pl.BlockSpec((1, de, d), lambda i, be, nu: (be[i], 0, 0))],
        out_specs=pl.BlockSpec((bm, d), lambda i, be, nu: (i, 0)),
        scratch_shapes=[pltpu.VMEM((d, 2 * de), BF16), pltpu.VMEM((de, d), BF16)],
    )
    return pl.pallas_call(
        _expert_kernel,
        grid_spec=grid_spec,
        out_shape=jax.ShapeDtypeStruct((n_rows, d), F32),
        compiler_params=pltpu.CompilerParams(
            dimension_semantics=("arbitrary",), vmem_limit_bytes=VMEM_LIMIT_BYTES),
        name="experts",
    )(block_expert, n_used, xbuf, w_gate, w_up, w_down)


def _combine_kernel(tab_ref, tab_next_ref, h2_ref, rinfo_ref, gf_ref, ybuf_ref, out_ref, ys_ref, sem):
    tr = h2_ref.shape[0]
    n_local = ys_ref.shape[1]
    i = pl.program_id(0)
    slot = i % 2

    def copier(s):
        def make_copy(local, src, n_groups):
            return pltpu.make_async_copy(ybuf_ref.at[_rows(src, n_groups)],
                                         ys_ref.at[s, _rows(local, n_groups)], sem.at[s])
        return make_copy

    def gather(tab, s):
        covered = tab[0, 0, TAB_LOCAL + N_EXPERTS - 1] + tab[0, 0, TAB_COUNT + N_EXPERTS - 1]

        def zero(g, c):
            ys_ref[s, _rows(g, 1), :] = jnp.zeros((ROW_GROUP, ys_ref.shape[2]), F32)
            return c
        lax.fori_loop(covered, n_local // ROW_GROUP, zero, 0)
        _for_each_segment_copy(tab, copier(s), lambda cp: cp.start())

    @pl.when(i == 0)
    def _():
        gather(tab_ref, 0)

    @pl.when(i + 1 < pl.num_programs(0))
    def _():
        gather(tab_next_ref, 1 - slot)

    _for_each_segment_copy(tab_ref, copier(slot), lambda cp: cp.wait())

    info = rinfo_ref[...]
    ciota = lax.broadcasted_iota(jnp.int32, (tr, n_local), 1)
    pick = (jnp.where(ciota == info[:, 0:1].astype(jnp.int32), 1.0, 0.0)
            + jnp.where(ciota == info[:, 1:2].astype(jnp.int32), 1.0, 0.0)).astype(BF16)
    h = h2_ref[...] + _dot(pick, ys_ref[slot].astype(BF16))
    out_ref[...] = _rms(h, gf_ref[...])


def _combine(table, h2, rinfo, gf, ybuf):
    t, d = h2.shape
    tr = SEQ_TILE
    n_tiles = t // tr
    tab_spec = (1, 1, table.shape[2])
    return pl.pallas_call(
        _combine_kernel,
        grid=(n_tiles,),
        in_specs=[pl.BlockSpec(tab_spec, lambda i: (i, 0, 0), memory_space=pltpu.SMEM),
                  pl.BlockSpec(tab_spec, lambda i: (jnp.minimum(i + 1, n_tiles - 1), 0, 0),
                               memory_space=pltpu.SMEM),
                  pl.BlockSpec((tr, d), lambda i: (i, 0)),
                  pl.BlockSpec((tr, LANES), lambda i: (i, 0)),
                  _const_spec(gf.shape),
                  pl.BlockSpec(memory_space=pl.ANY)],
        out_specs=pl.BlockSpec((tr, d), lambda i: (i, 0)),
        out_shape=jax.ShapeDtypeStruct((t, d), F32),
        scratch_shapes=[pltpu.VMEM((2, LOCAL_ROWS, d), F32), pltpu.SemaphoreType.DMA((2,))],
        compiler_params=pltpu.CompilerParams(
            dimension_semantics=("arbitrary",), vmem_limit_bytes=VMEM_LIMIT_BYTES),
        name="combine",
    )(table, table, h2, rinfo, gf, ybuf)


def _pad_lanes(v, fill=0.0):
    return jnp.pad(v, (0, LANES - v.shape[0]), constant_values=fill)[None, :]


def kernel(x, mem, norm_mem, norm_mix, w_in, conv_ssd_w, conv_ssd_b, dt_bias, a_log, d_skip, norm_ssd_gate,
           conv_short_w, w_out, norm_xattn, w_q, w_kv, w_o, norm_moe, w_router_group, b_router_group,
           w_router_expert, b_router_expert, w_gate, w_up, w_down, norm_final):
    bsz, seq, d = x.shape
    t = bsz * seq
    assert w_in.shape[0] == 1, "single-layer model"
    assert d == D_MODEL and seq % SEQ_TILE == 0 and t % ROW_TILE == 0

    tri = jnp.tril(jnp.ones((SSD_CHUNK, SSD_CHUNK), F32))
    upper = jnp.triu(jnp.ones((SEQ_TILE, SEQ_TILE), F32), k=1).astype(BF16)

    o2 = SSD_WIDTH + XBC_WIDTH
    o3 = o2 + SSD_HEADS
    wl = w_in[0]
    win = jnp.concatenate(
        [wl[:, :o2], wl[:, o3:], wl[:, o2:o3], jnp.zeros((d, LANES - SSD_HEADS), F32)], axis=1).astype(BF16)
    h1 = _mixer(x, norm_mix, win, conv_ssd_w[0], conv_ssd_b, _pad_lanes(dt_bias[0]),
                _pad_lanes(-jnp.exp(a_log[0])), jnp.repeat(d_skip[0], SSD_HEAD_DIM)[None, :], norm_ssd_gate,
                conv_short_w[0], w_out[0].astype(BF16), tri)

    kt, v = _memory_kv(mem, norm_mem[None, :], w_kv[0].astype(BF16))
    wr = jnp.zeros((d, LANES), F32)
    wr = wr.at[:, ROUTER_GROUP_ROW:ROUTER_GROUP_ROW + N_GROUPS_MOE].set(w_router_group[0])
    wr = wr.at[:, ROUTER_EXPERT_ROW:ROUTER_EXPERT_ROW + N_EXPERTS].set(w_router_expert[0])
    wr_hi = wr.astype(BF16)
    wr_lo = (wr - wr_hi.astype(F32)).astype(BF16)
    br = jnp.zeros((LANES,), F32)
    br = br.at[ROUTER_GROUP_ROW:ROUTER_GROUP_ROW + N_GROUPS_MOE].set(b_router_group[0])
    br = br.at[ROUTER_EXPERT_ROW:ROUTER_EXPERT_ROW + N_EXPERTS].set(b_router_expert[0])
    lower = jnp.tril(jnp.ones((N_EXPERTS, N_EXPERTS), F32), k=-1).astype(BF16)
    n_tiles = t // SEQ_TILE
    bgroups = EXPERT_BLOCK // ROW_GROUP
    max_groups = (2 * t + n_tiles * N_EXPERTS * (ROW_GROUP - 1)) // ROW_GROUP
    n_blocks = -(-max_groups // bgroups) + N_EXPERTS
    even = max(1, (n_tiles * SUBLANES) // math.gcd(n_tiles * SUBLANES, EXPERT_BLOCK))
    n_blocks = -(-n_blocks // even) * even
    h2, xs, rinfo, seg, counts, xbuf0 = _xattn_router(
        h1, norm_xattn, w_q[0].astype(BF16), kt, v, w_o[0].astype(BF16), norm_moe,
        jnp.concatenate([wr_hi, wr_lo], axis=1), br[None, :], upper, lower, n_blocks * EXPERT_BLOCK)

    seg = seg.reshape(n_tiles, N_EXPERTS, LANES)[:, :, 0:3].astype(jnp.int32)
    seg_groups, seg_local, seg_before = seg[:, :, 0], seg[:, :, 1], seg[:, :, 2]
    sizes = counts[:, 0].astype(jnp.int32)
    padded = ((sizes + bgroups - 1) // bgroups) * bgroups
    pad_end = jnp.cumsum(padded)
    pad_start = pad_end - padded
    seg_dst = pad_start[None, :] + seg_before
    table = jnp.concatenate([seg_groups, seg_dst, seg_local], axis=1)[:, None, :]
    block_group = jnp.arange(n_blocks, dtype=jnp.int32) * bgroups
    block_expert = jnp.minimum(jnp.sum((pad_end[None, :] <= block_group[:, None]).astype(jnp.int32), axis=-1),
                               N_EXPERTS - 1)
    n_used = (pad_end[-1:] // bgroups).astype(jnp.int32)

    xbuf = _dispatch(table, xs, xbuf0)
    ybuf = _experts(block_expert, n_used, xbuf, w_gate[0], w_up[0], w_down[0])
    out = _combine(table, h2.reshape(t, d), rinfo, norm_final[None, :], ybuf)
    return out.reshape(bsz, seq, d)
```

```python
import math

import jax
import jax.numpy as jnp
from jax import lax
from jax.experimental import pallas as pl
from jax.experimental.pallas import tpu as pltpu

F32 = jnp.float32
BF16 = jnp.bfloat16
EPS = 1e-6

LANES = 128
SUBLANES = 8
VMEM_LIMIT_BYTES = 56 * 1024 * 1024

D_MODEL = 1024
SSD_WIDTH = 512
SSD_HEAD_DIM = 64
SSD_HEADS = 8
SSD_GROUPS = 2
SSD_HEADS_PER_GROUP = 4
SSD_STATE = 64
SSD_CHUNK = 128
XBC_WIDTH = SSD_WIDTH + 2 * SSD_GROUPS * SSD_STATE
SC_WIDTH = 512
XA_HEADS = 4
XA_HEAD_DIM = 256
N_GROUPS_MOE = 4
EXPERTS_PER_GROUP = 8
N_EXPERTS = 32
D_EXPERT = 512

COL_Z = 0
COL_XBC = COL_Z + SSD_WIDTH
COL_GB = COL_XBC + XBC_WIDTH
COL_GC = COL_GB + SC_WIDTH
COL_V = COL_GC + SC_WIDTH
COL_DT = COL_V + SC_WIDTH
PROJ_WIDTH = COL_DT + LANES

SEQ_TILE = 512
ROW_TILE = 512
EXPERT_BLOCK = 512
HALO = SUBLANES
ROW_GROUP = SUBLANES
LOCAL_ROWS = 2 * SEQ_TILE + N_EXPERTS * ROW_GROUP
ROW_WORDS = D_MODEL // 2 + LANES

ROUTER_GROUP_ROW = 0
ROUTER_EXPERT_ROW = SUBLANES


def _rms(x, g):
    return x * lax.rsqrt(jnp.mean(x * x, axis=-1, keepdims=True) + EPS) * g


def _silu(x):
    return x * jax.nn.sigmoid(x)


def _dot(a, b):
    return jnp.dot(a, b, preferred_element_type=F32)


def _pack_halves(v):
    w = v.shape[1] // 2
    lo = lax.shift_right_logical(pltpu.bitcast(v[:, :w], jnp.uint32), jnp.uint32(16))
    hi = pltpu.bitcast(v[:, w:], jnp.uint32) & jnp.uint32(0xFFFF0000)
    return lo | hi


def _unpack_halves(words):
    lo = pltpu.bitcast(lax.shift_left(words, jnp.uint32(16)), F32).astype(BF16)
    hi = pltpu.bitcast(words & jnp.uint32(0xFFFF0000), F32).astype(BF16)
    return lo, hi


def _mixer_kernel(x_ref, gmix_ref, win_ref, cw_ref, cb_ref, dtb_ref, a_ref, dskip_ref, gate_ref,
                  scw_ref, wout_ref, tri_ref, h_ref,
                  conv_ref, cv_ref, act_ref, dt_ref, y_ref, state_ref, sinc_ref, loff_ref, elast_ref):
    ts = x_ref.shape[1]
    s = pl.program_id(1)
    x = x_ref[0]
    nb = _rms(x, gmix_ref[...]).astype(BF16)

    @pl.when(s == 0)
    def _():
        conv_ref[0:HALO, :] = jnp.zeros((HALO, XBC_WIDTH), F32)
        cv_ref[0:HALO, :] = jnp.zeros((HALO, SC_WIDTH), F32)
        state_ref[...] = jnp.zeros(state_ref.shape, F32)

    @pl.when(s > 0)
    def _():
        conv_ref[0:HALO, :] = conv_ref[ts:ts + HALO, :]
        cv_ref[0:HALO, :] = cv_ref[ts:ts + HALO, :]

    xbc = _dot(nb, win_ref[:, COL_XBC:COL_XBC + XBC_WIDTH])
    conv_ref[HALO:HALO + ts, :] = xbc
    acc = cb_ref[...] + cw_ref[3:4, :] * xbc
    for i in range(3):
        acc = acc + cw_ref[i:i + 1, :] * conv_ref[HALO - 3 + i:HALO - 3 + i + ts, :]
    act_ref[...] = _silu(acc)

    dtr = _dot(nb, win_ref[:, COL_DT:COL_DT + LANES]) + dtb_ref[...]
    dt_ref[...] = jnp.maximum(dtr, 0.0) + jnp.log1p(jnp.exp(-jnp.abs(dtr)))

    row = lax.broadcasted_iota(jnp.int32, (SSD_CHUNK, SSD_CHUNK), 0)
    col = lax.broadcasted_iota(jnp.int32, (SSD_CHUNK, SSD_CHUNK), 1)
    causal = row >= col
    lo_q = col < SSD_HEAD_DIM
    lo_n = lax.broadcasted_iota(jnp.int32, (SSD_STATE, SSD_CHUNK), 1) < SSD_HEAD_DIM
    lo_e = lax.broadcasted_iota(jnp.int32, (SUBLANES, SSD_CHUNK), 1) < SSD_HEAD_DIM

    def chunk_prologue(c):
        rows = slice(c * SSD_CHUNK, (c + 1) * SSD_CHUNK)
        dt = dt_ref[rows, :]
        adt = dt * a_ref[...]
        cs = jnp.dot(tri_ref[...], adt, preferred_element_type=F32,
                     precision=lax.Precision.HIGHEST)
        bm = act_ref[rows, SSD_WIDTH:SSD_WIDTH + SSD_GROUPS * SSD_STATE]
        cm = act_ref[rows, SSD_WIDTH + SSD_GROUPS * SSD_STATE:XBC_WIDTH]
        cbs = []
        for g in range(SSD_GROUPS):
            cg = cm[:, g * SSD_STATE:(g + 1) * SSD_STATE]
            bg = bm[:, g * SSD_STATE:(g + 1) * SSD_STATE]
            cbs.append(lax.dot_general(cg.astype(BF16), bg.astype(BF16), (((1,), (1,)), ((), ())),
                                       preferred_element_type=F32))
        return cs, cs.T, dt.T, bm.T, cm, pltpu.roll(cm, SSD_STATE, axis=1), cbs

    def chunk_body(c, prologue):
        rows = slice(c * SSD_CHUNK, (c + 1) * SSD_CHUNK)
        cs, cs_t, dt_t, bm_t, cm, cm_sw, cbs = prologue
        for g in range(SSD_GROUPS):
            bg_t = bm_t[g * SSD_STATE:(g + 1) * SSD_STATE, :]
            cb = cbs[g]
            c2 = jnp.where(lo_q, cm, cm_sw) if g == 0 else jnp.where(lo_q, cm_sw, cm)
            for jj in range(SSD_HEADS_PER_GROUP // 2):
                j = g * (SSD_HEADS_PER_GROUP // 2) + jj
                idx = c * (SSD_HEADS // 2) + j
                xp = act_ref[rows, j * LANES:(j + 1) * LANES]
                rhs = jnp.concatenate(
                    [jnp.where(lo_q, xp, 0.0), jnp.where(lo_q, 0.0, xp)], axis=0).astype(BF16)
                diag, off, w_rows, e_last = [], [], [], []
                for h in (2 * j, 2 * j + 1):
                    cs_col = jnp.broadcast_to(cs[:, h:h + 1], (SSD_CHUNK, SSD_CHUNK))
                    cs_row = cs_t[h:h + 1, :]
                    dt_row = dt_t[h:h + 1, :]
                    decay = jnp.exp(jnp.where(causal, cs_col - cs_row, -jnp.inf))
                    diag.append((cb * dt_row) * decay)
                    off.append(jnp.exp(cs_col))
                    cl = cs_t[h:h + 1, SSD_CHUNK - 1:SSD_CHUNK]
                    w_rows.append(jnp.exp(cl - cs_row) * dt_row)
                    e_last.append(jnp.exp(cl))
                y_ref[rows, j * LANES:(j + 1) * LANES] = _dot(jnp.concatenate(diag, axis=1).astype(BF16), rhs)
                lhs2 = jnp.concatenate([bg_t * w_rows[0], bg_t * w_rows[1]], axis=1).astype(BF16)
                sinc_ref[idx] = _dot(lhs2, rhs)
                loff_ref[idx] = (c2 * jnp.where(lo_q, off[0], off[1])).astype(BF16)
                elast_ref[idx] = jnp.where(lo_e, e_last[0], e_last[1])

    n_chunks = ts // SSD_CHUNK
    prologues = [chunk_prologue(c) for c in range(n_chunks)]

    g_c = _dot(nb, win_ref[:, COL_GC:COL_GC + SC_WIDTH])
    v = _dot(nb, win_ref[:, COL_V:COL_V + SC_WIDTH])
    u = g_c * v
    cv_ref[HALO:HALO + ts, :] = u
    conv3 = scw_ref[2:3, :] * u
    for i in range(2):
        conv3 = conv3 + scw_ref[i:i + 1, :] * cv_ref[HALO - 2 + i:HALO - 2 + i + ts, :]
    g_b = _dot(nb, win_ref[:, COL_GB:COL_GB + SC_WIDTH])
    out_sc = _dot((g_b * conv3).astype(BF16), wout_ref[SSD_WIDTH:, :])
    gate_z = _silu(_dot(nb, win_ref[:, COL_Z:COL_Z + SSD_WIDTH]))

    for c in range(n_chunks):
        chunk_body(c, prologues[c])

    for c in range(n_chunks):
        for j in range(SSD_HEADS // 2):
            idx = c * (SSD_HEADS // 2) + j
            stp = state_ref[j]
            rhs_st = jnp.concatenate(
                [jnp.where(lo_n, stp, 0.0), jnp.where(lo_n, 0.0, stp)], axis=0).astype(BF16)
            y_ref[c * SSD_CHUNK:(c + 1) * SSD_CHUNK, j * LANES:(j + 1) * LANES] += _dot(loff_ref[idx], rhs_st)
            state_ref[j] = elast_ref[idx][0:1, :] * stp + sinc_ref[idx]

    y = y_ref[...] + act_ref[:, 0:SSD_WIDTH] * dskip_ref[...]
    gated = y * gate_z
    gw = SSD_WIDTH // SSD_GROUPS
    y_ssd = jnp.concatenate(
        [_rms(gated[:, g * gw:(g + 1) * gw], gate_ref[:, g * gw:(g + 1) * gw]) for g in range(SSD_GROUPS)],
        axis=-1).astype(BF16)
    h_ref[0] = x + out_sc + _dot(y_ssd, wout_ref[0:SSD_WIDTH, :])


def _const_spec(shape):
    nd = len(shape)
    return pl.BlockSpec(shape, lambda *_: (0,) * nd)


def _mixer(x, gmix, win, cw, cb, dtb, a_row, dskip, gate, scw, wout, tri):
    bsz, seq, d = x.shape
    ts = SEQ_TILE
    consts = (gmix, win, cw, cb, dtb, a_row, dskip, gate, scw, wout, tri)
    return pl.pallas_call(
        _mixer_kernel,
        grid=(bsz, seq // ts),
        in_specs=[pl.BlockSpec((1, ts, d), lambda b, s: (b, s, 0))] + [_const_spec(c.shape) for c in consts],
        out_specs=pl.BlockSpec((1, ts, d), lambda b, s: (b, s, 0)),
        out_shape=jax.ShapeDtypeStruct((bsz, seq, d), F32),
        scratch_shapes=[
            pltpu.VMEM((ts + HALO, XBC_WIDTH), F32),
            pltpu.VMEM((ts + HALO, SC_WIDTH), F32),
            pltpu.VMEM((ts, XBC_WIDTH), F32),
            pltpu.VMEM((ts, LANES), F32),
            pltpu.VMEM((ts, SSD_WIDTH), F32),
            pltpu.VMEM((SSD_HEADS // 2, SSD_STATE, 2 * SSD_HEAD_DIM), F32),
            pltpu.VMEM((ts // SSD_CHUNK * SSD_HEADS // 2, SSD_STATE, 2 * SSD_HEAD_DIM), F32),
            pltpu.VMEM((ts // SSD_CHUNK * SSD_HEADS // 2, SSD_CHUNK, 2 * SSD_STATE), BF16),
            pltpu.VMEM((ts // SSD_CHUNK * SSD_HEADS // 2, SUBLANES, 2 * SSD_HEAD_DIM), F32),
        ],
        compiler_params=pltpu.CompilerParams(
            dimension_semantics=("arbitrary", "arbitrary"), vmem_limit_bytes=VMEM_LIMIT_BYTES),
        name="mixer",
    )(x, *consts)


def _kv_kernel(mem_ref, g_ref, wkv_ref, kt_ref, v_ref):
    d = mem_ref.shape[2]
    mn = _rms(mem_ref[0], g_ref[...]).astype(BF16)
    kv = _dot(mn, wkv_ref[...])
    kt_ref[0] = (kv[:, :d] * (XA_HEAD_DIM ** -0.5)).T.astype(BF16)
    v_ref[0] = kv[:, d:].astype(BF16)


def _memory_kv(mem, g, wkv):
    bsz, m, d = mem.shape
    return pl.pallas_call(
        _kv_kernel,
        grid=(bsz,),
        in_specs=[pl.BlockSpec((1, m, d), lambda b: (b, 0, 0)), _const_spec(g.shape), _const_spec(wkv.shape)],
        out_specs=[pl.BlockSpec((1, d, m), lambda b: (b, 0, 0)), pl.BlockSpec((1, m, d), lambda b: (b, 0, 0))],
        out_shape=[jax.ShapeDtypeStruct((bsz, d, m), BF16), jax.ShapeDtypeStruct((bsz, m, d), BF16)],
        compiler_params=pltpu.CompilerParams(
            dimension_semantics=("arbitrary",), vmem_limit_bytes=VMEM_LIMIT_BYTES),
        name="memory_kv",
    )(mem, g, wkv)


def _xattn_router_kernel(h1_ref, gx_ref, wq_ref, kt_ref, v_ref, wo_ref, gm_ref, wr_ref, br_ref, upper_ref,
                         lower_ref, h2_ref, xs_ref, rinfo_ref, seg_ref, counts_ref, zero_ref, carry_ref):
    ts = h1_ref.shape[1]
    first = jnp.logical_and(pl.program_id(0) == 0, pl.program_id(1) == 0)

    @pl.when(first)
    def _():
        carry_ref[...] = jnp.zeros(carry_ref.shape, F32)

    h1 = h1_ref[0]
    q = _dot(_rms(h1, gx_ref[...]).astype(BF16), wq_ref[...]).astype(BF16)
    outs = []
    for h in range(XA_HEADS):
        hs = slice(h * XA_HEAD_DIM, (h + 1) * XA_HEAD_DIM)
        sc = _dot(q[:, hs], kt_ref[0, hs, :])
        p = jnp.exp(sc - jnp.max(sc, axis=-1, keepdims=True))
        o = _dot(p.astype(BF16), v_ref[0, :, hs]) / jnp.sum(p, axis=-1, keepdims=True)
        outs.append(o.astype(BF16))
    h2 = h1 + _dot(jnp.concatenate(outs, axis=-1), wo_ref[...])
    h2_ref[0] = h2

    n3 = _rms(h2, gm_ref[...])
    n3b = n3.astype(BF16)
    half = n3.shape[1] // 2
    n3r = n3b.astype(F32)
    zero_ref[...] = jnp.zeros(zero_ref.shape, zero_ref.dtype)

    n3_lo = (n3 - n3r).astype(BF16)
    lg2 = _dot(n3b, wr_ref[...])
    logits = lg2[:, :LANES] + lg2[:, LANES:] + _dot(n3_lo, wr_ref[:, :LANES]) + br_ref[...]
    lt = logits.T

    l0, l1, l2, l3 = (lt[ROUTER_GROUP_ROW + i:ROUTER_GROUP_ROW + i + 1, :] for i in range(N_GROUPS_MOE))
    m = jnp.maximum(jnp.maximum(l0, l1), jnp.maximum(l2, l3))
    gidx = jnp.where(l0 == m, 0, jnp.where(l1 == m, 1, jnp.where(l2 == m, 2, 3)))
    g_w = 1.0 / (jnp.exp(l0 - m) + jnp.exp(l1 - m) + jnp.exp(l2 - m) + jnp.exp(l3 - m))

    def group_rows(g):
        r = ROUTER_EXPERT_ROW + g * EXPERTS_PER_GROUP
        return lt[r:r + EXPERTS_PER_GROUP, :]

    el = jnp.where(gidx == 0, group_rows(0),
                   jnp.where(gidx == 1, group_rows(1), jnp.where(gidx == 2, group_rows(2), group_rows(3))))
    sub = lax.broadcasted_iota(jnp.int32, el.shape, 0)
    m1 = jnp.max(el, axis=0, keepdims=True)
    i1 = jnp.min(jnp.where(el == m1, sub, EXPERTS_PER_GROUP), axis=0, keepdims=True)
    el2 = jnp.where(sub == i1, -jnp.inf, el)
    m2 = jnp.max(el2, axis=0, keepdims=True)
    i2 = jnp.min(jnp.where(el2 == m2, sub, EXPERTS_PER_GROUP), axis=0, keepdims=True)
    e2 = jnp.exp(m2 - m1)
    w1 = 1.0 / (1.0 + e2)
    w2 = e2 / (1.0 + e2)
    e_a = gidx * EXPERTS_PER_GROUP + i1
    e_b = gidx * EXPERTS_PER_GROUP + i2

    eio = lax.broadcasted_iota(jnp.int32, (N_EXPERTS, ts), 0)
    oh_a = (eio == e_a).astype(F32)
    oh_b = (eio == e_b).astype(F32)
    prefix = _dot(jnp.concatenate([oh_a, oh_b], axis=0).astype(BF16), upper_ref[...])
    tot_a = jnp.sum(oh_a, axis=1, keepdims=True)
    tot_b = jnp.sum(oh_b, axis=1, keepdims=True)
    groups = jnp.broadcast_to(jnp.floor((tot_a + tot_b + (ROW_GROUP - 1)) * (1.0 / ROW_GROUP)),
                              (N_EXPERTS, LANES))
    first_group = _dot(lower_ref[...], groups.astype(BF16))
    base = ROW_GROUP * first_group[:, 0:1]
    pos_a = jnp.sum(oh_a * (base + prefix[:N_EXPERTS]), axis=0, keepdims=True)
    pos_b = jnp.sum(oh_b * (base + tot_a + prefix[N_EXPERTS:]), axis=0, keepdims=True)

    carry = carry_ref[...]
    lane = lax.broadcasted_iota(jnp.int32, (N_EXPERTS, LANES), 1)
    seg_ref[...] = jnp.where(lane == 0, groups, jnp.where(lane == 1, first_group,
                                                          jnp.where(lane == 2, carry, 0.0)))
    carry_ref[...] = carry + groups
    counts_ref[...] = carry + groups

    n_local = xs_ref.shape[0]
    riota = lax.broadcasted_iota(jnp.int32, (n_local, ts), 0)
    sel_a = riota == pos_a.astype(jnp.int32)
    sel_b = riota == pos_b.astype(jnp.int32)
    perm = (jnp.where(sel_a, 1.0, 0.0) + jnp.where(sel_b, 1.0, 0.0)).astype(BF16)
    xs = _dot(perm, n3b)
    xs_ref[:, :half] = _pack_halves(xs)
    w_sorted = jnp.sum(jnp.where(sel_a, g_w * w1, 0.0) + jnp.where(sel_b, g_w * w2, 0.0),
                       axis=1, keepdims=True)
    xs_ref[:, half:] = pltpu.bitcast(jnp.broadcast_to(w_sorted, (n_local, LANES)), jnp.uint32)

    res = jnp.concatenate([pos_a, pos_b, jnp.zeros((LANES - 2, ts), F32)], axis=0)
    rinfo_ref[...] = res.T


def _xattn_router(h1, gx, wq, kt, v, wo, gm, wr, br, upper, lower, n_rows):
    bsz, seq, d = h1.shape
    ts = SEQ_TILE
    m = v.shape[1]
    nt = seq // ts
    n_tiles = bsz * nt
    zero_rows = n_rows // n_tiles
    assert zero_rows * n_tiles == n_rows and zero_rows % SUBLANES == 0
    tok_spec = pl.BlockSpec((1, ts, d), lambda b, s: (b, s, 0))
    return pl.pallas_call(
        _xattn_router_kernel,
        grid=(bsz, nt),
        in_specs=[tok_spec, _const_spec(gx.shape), _const_spec(wq.shape),
                  pl.BlockSpec((1, d, m), lambda b, s: (b, 0, 0)),
                  pl.BlockSpec((1, m, d), lambda b, s: (b, 0, 0)),
                  _const_spec(wo.shape), _const_spec(gm.shape), _const_spec(wr.shape), _const_spec(br.shape),
                  _const_spec(upper.shape), _const_spec(lower.shape)],
        out_specs=[tok_spec,
                   pl.BlockSpec((LOCAL_ROWS, ROW_WORDS), lambda b, s: (b * nt + s, 0)),
                   pl.BlockSpec((ts, LANES), lambda b, s: (b * nt + s, 0)),
                   pl.BlockSpec((N_EXPERTS, LANES), lambda b, s: (b * nt + s, 0)),
                   pl.BlockSpec((N_EXPERTS, LANES), lambda b, s: (0, 0)),
                   pl.BlockSpec((zero_rows, ROW_WORDS), lambda b, s: (b * nt + s, 0))],
        out_shape=[jax.ShapeDtypeStruct((bsz, seq, d), F32),
                   jax.ShapeDtypeStruct((n_tiles * LOCAL_ROWS, ROW_WORDS), jnp.uint32),
                   jax.ShapeDtypeStruct((bsz * seq, LANES), F32),
                   jax.ShapeDtypeStruct((n_tiles * N_EXPERTS, LANES), F32),
                   jax.ShapeDtypeStruct((N_EXPERTS, LANES), F32),
                   jax.ShapeDtypeStruct((n_rows, ROW_WORDS), jnp.uint32)],
        scratch_shapes=[pltpu.VMEM((N_EXPERTS, LANES), F32)],
        compiler_params=pltpu.CompilerParams(
            dimension_semantics=("arbitrary", "arbitrary"), vmem_limit_bytes=VMEM_LIMIT_BYTES),
        name="xattn_router",
    )(h1, gx, wq, kt, v, wo, gm, wr, br, upper, lower)


TAB_COUNT, TAB_DST, TAB_LOCAL = (k * N_EXPERTS for k in range(3))


SEGMENT_RUN_LOG2 = 3
SEGMENT_CHUNKS = (4, 2, 1)


def _rows(first_group, n_groups):
    return pl.ds(pl.multiple_of(first_group * ROW_GROUP, ROW_GROUP), n_groups * ROW_GROUP)


def _for_each_segment_copy(tab_ref, make_copy, action):
    run = 1 << SEGMENT_RUN_LOG2

    def expert(e, c):
        n = tab_ref[0, 0, TAB_COUNT + e]
        dst0 = tab_ref[0, 0, TAB_DST + e]
        loc0 = tab_ref[0, 0, TAB_LOCAL + e]

        def full_run(k, c2):
            action(make_copy(loc0 + k * run, dst0 + k * run, run))
            return c2
        lax.fori_loop(0, lax.shift_right_logical(n, SEGMENT_RUN_LOG2), full_run, 0)
        for size in SEGMENT_CHUNKS:
            off = n & (-2 * size)

            @pl.when((n & size) != 0)
            def _():
                action(make_copy(loc0 + off, dst0 + off, size))
        return c
    lax.fori_loop(0, N_EXPERTS, expert, 0)


def _dispatch_kernel(tab_ref, xs_ref, buf_in_ref, buf_ref, sem):
    del buf_in_ref

    def make_copy(local, dst, n_groups):
        return pltpu.make_async_copy(xs_ref.at[_rows(local, n_groups)], buf_ref.at[_rows(dst, n_groups)], sem)

    _for_each_segment_copy(tab_ref, make_copy, lambda cp: cp.start())
    _for_each_segment_copy(tab_ref, make_copy, lambda cp: cp.wait())


def _dispatch(table, xs, buf0):
    n_tiles = table.shape[0]
    return pl.pallas_call(
        _dispatch_kernel,
        grid=(n_tiles,),
        in_specs=[pl.BlockSpec((1, 1, table.shape[2]), lambda i: (i, 0, 0), memory_space=pltpu.SMEM),
                  pl.BlockSpec((LOCAL_ROWS, xs.shape[1]), lambda i: (i, 0)),
                  pl.BlockSpec(memory_space=pl.ANY)],
        out_specs=pl.BlockSpec(memory_space=pl.ANY),
        out_shape=jax.ShapeDtypeStruct(buf0.shape, buf0.dtype),
        scratch_shapes=[pltpu.SemaphoreType.DMA],
        input_output_aliases={2: 0},
        compiler_params=pltpu.CompilerParams(
            dimension_semantics=("arbitrary",), vmem_limit_bytes=VMEM_LIMIT_BYTES),
        name="dispatch",
    )(table, xs, buf0)


def _expert_kernel(be_ref, nused_ref, xb_ref, wg_ref, wu_ref, wd_ref, y_ref, wgu_s, wd_s):
    i = pl.program_id(0)
    used = i < nused_ref[0]

    @pl.when(jnp.logical_and(used, jnp.logical_or(i == 0, be_ref[i] != be_ref[jnp.maximum(i - 1, 0)])))
    def _():
        wgu_s[:, :D_EXPERT] = wg_ref[0].astype(BF16)
        wgu_s[:, D_EXPERT:] = wu_ref[0].astype(BF16)
        wd_s[...] = wd_ref[0].astype(BF16)

    @pl.when(used)
    def _():
        half = xb_ref.shape[1] - LANES
        x_lo, x_hi = _unpack_halves(xb_ref[:, :half])
        gu = _dot(x_lo, wgu_s[:half, :]) + _dot(x_hi, wgu_s[half:, :])
        mid = (_silu(gu[:, :D_EXPERT]) * gu[:, D_EXPERT:]).astype(BF16)
        comb = pltpu.bitcast(xb_ref[:, half:half + LANES], F32)[:, 0:1]
        y = _dot(mid, wd_s[...]) * comb
        y_ref[...] = _pack_halves(y.astype(BF16).astype(F32))

    @pl.when(jnp.logical_not(used))
    def _():
        y_ref[...] = jnp.zeros(y_ref.shape, y_ref.dtype)


def _experts(block_expert, n_used, xbuf, w_gate, w_up, w_down):
    w = xbuf.shape[1]
    bm = EXPERT_BLOCK
    n_blocks = block_expert.shape[0]
    n_rows = n_blocks * bm
    d, de = w_gate.shape[1], w_gate.shape[2]
    grid_spec = pltpu.PrefetchScalarGridSpec(
        num_scalar_prefetch=2,
        grid=(n_blocks,),
        in_specs=[pl.BlockSpec((bm, w), lambda i, be, nu: (i, 0)),
                  pl.BlockSpec((1, d, de), lambda i, be, nu: (be[i], 0, 0)),
                  pl.BlockSpec((1, d, de), lambda i, be, nu: (be[i], 0, 0)),
                  pl.BlockSpec((1, de, d), lambda i, be, nu: (be[i], 0, 0))],
        out_specs=pl.BlockSpec((bm, d // 2), lambda i, be, nu: (i, 0)),
        scratch_shapes=[pltpu.VMEM((d, 2 * de), BF16), pltpu.VMEM((de, d), BF16)],
    )
    return pl.pallas_call(
        _expert_kernel,
        grid_spec=grid_spec,
        out_shape=jax.ShapeDtypeStruct((n_rows, d // 2), jnp.uint32),
        compiler_params=pltpu.CompilerParams(
            dimension_semantics=("arbitrary",), vmem_limit_bytes=VMEM_LIMIT_BYTES),
        name="experts",
    )(block_expert, n_used, xbuf, w_gate, w_up, w_down)


def _combine_kernel(tab_ref, tab_next_ref, h2_ref, rinfo_ref, gf_ref, ybuf_ref, out_ref, ys_ref, sem):
    tr = h2_ref.shape[0]
    n_local = ys_ref.shape[1]
    i = pl.program_id(0)
    slot = i % 2

    def copier(s):
        def make_copy(local, src, n_groups):
            return pltpu.make_async_copy(ybuf_ref.at[_rows(src, n_groups)],
                                         ys_ref.at[s, _rows(local, n_groups)], sem.at[s])
        return make_copy

    def gather(tab, s):
        covered = tab[0, 0, TAB_LOCAL + N_EXPERTS - 1] + tab[0, 0, TAB_COUNT + N_EXPERTS - 1]

        def zero(g, c):
            ys_ref[s, _rows(g, 1), :] = jnp.zeros((ROW_GROUP, ys_ref.shape[2]), ys_ref.dtype)
            return c
        lax.fori_loop(covered, n_local // ROW_GROUP, zero, 0)
        _for_each_segment_copy(tab, copier(s), lambda cp: cp.start())

    @pl.when(i == 0)
    def _():
        gather(tab_ref, 0)

    @pl.when(i + 1 < pl.num_programs(0))
    def _():
        gather(tab_next_ref, 1 - slot)

    _for_each_segment_copy(tab_ref, copier(slot), lambda cp: cp.wait())

    info = rinfo_ref[...]
    ciota = lax.broadcasted_iota(jnp.int32, (tr, n_local), 1)
    pick = (jnp.where(ciota == info[:, 0:1].astype(jnp.int32), 1.0, 0.0)
            + jnp.where(ciota == info[:, 1:2].astype(jnp.int32), 1.0, 0.0)).astype(BF16)
    y_lo, y_hi = _unpack_halves(ys_ref[slot])
    h = h2_ref[...] + jnp.concatenate([_dot(pick, y_lo), _dot(pick, y_hi)], axis=-1)
    out_ref[...] = _rms(h, gf_ref[...])


def _combine(table, h2, rinfo, gf, ybuf):
    t, d = h2.shape
    tr = SEQ_TILE
    n_tiles = t // tr
    tab_spec = (1, 1, table.shape[2])
    return pl.pallas_call(
        _combine_kernel,
        grid=(n_tiles,),
        in_specs=[pl.BlockSpec(tab_spec, lambda i: (i, 0, 0), memory_space=pltpu.SMEM),
                  pl.BlockSpec(tab_spec, lambda i: (jnp.minimum(i + 1, n_tiles - 1), 0, 0),
                               memory_space=pltpu.SMEM),
                  pl.BlockSpec((tr, d), lambda i: (i, 0)),
                  pl.BlockSpec((tr, LANES), lambda i: (i, 0)),
                  _const_spec(gf.shape),
                  pl.BlockSpec(memory_space=pl.ANY)],
        out_specs=pl.BlockSpec((tr, d), lambda i: (i, 0)),
        out_shape=jax.ShapeDtypeStruct((t, d), F32),
        scratch_shapes=[pltpu.VMEM((2, LOCAL_ROWS, d // 2), jnp.uint32), pltpu.SemaphoreType.DMA((2,))],
        compiler_params=pltpu.CompilerParams(
            dimension_semantics=("arbitrary",), vmem_limit_bytes=VMEM_LIMIT_BYTES),
        name="combine",
    )(table, table, h2, rinfo, gf, ybuf)


def _pad_lanes(v, fill=0.0):
    return jnp.pad(v, (0, LANES - v.shape[0]), constant_values=fill)[None, :]


def kernel(x, mem, norm_mem, norm_mix, w_in, conv_ssd_w, conv_ssd_b, dt_bias, a_log, d_skip, norm_ssd_gate,
           conv_short_w, w_out, norm_xattn, w_q, w_kv, w_o, norm_moe, w_router_group, b_router_group,
           w_router_expert, b_router_expert, w_gate, w_up, w_down, norm_final):
    bsz, seq, d = x.shape
    t = bsz * seq
    assert w_in.shape[0] == 1, "single-layer model"
    assert d == D_MODEL and seq % SEQ_TILE == 0 and t % ROW_TILE == 0

    tri = jnp.tril(jnp.ones((SSD_CHUNK, SSD_CHUNK), F32))
    upper = jnp.triu(jnp.ones((SEQ_TILE, SEQ_TILE), F32), k=1).astype(BF16)

    o2 = SSD_WIDTH + XBC_WIDTH
    o3 = o2 + SSD_HEADS
    wl = w_in[0]
    win = jnp.concatenate(
        [wl[:, :o2], wl[:, o3:], wl[:, o2:o3], jnp.zeros((d, LANES - SSD_HEADS), F32)], axis=1).astype(BF16)
    h1 = _mixer(x, norm_mix, win, conv_ssd_w[0], conv_ssd_b, _pad_lanes(dt_bias[0]),
                _pad_lanes(-jnp.exp(a_log[0])), jnp.repeat(d_skip[0], SSD_HEAD_DIM)[None, :], norm_ssd_gate,
                conv_short_w[0], w_out[0].astype(BF16), tri)

    kt, v = _memory_kv(mem, norm_mem[None, :], w_kv[0].astype(BF16))
    wr = jnp.zeros((d, LANES), F32)
    wr = wr.at[:, ROUTER_GROUP_ROW:ROUTER_GROUP_ROW + N_GROUPS_MOE].set(w_router_group[0])
    wr = wr.at[:, ROUTER_EXPERT_ROW:ROUTER_EXPERT_ROW + N_EXPERTS].set(w_router_expert[0])
    wr_hi = wr.astype(BF16)
    wr_lo = (wr - wr_hi.astype(F32)).astype(BF16)
    br = jnp.zeros((LANES,), F32)
    br = br.at[ROUTER_GROUP_ROW:ROUTER_GROUP_ROW + N_GROUPS_MOE].set(b_router_group[0])
    br = br.at[ROUTER_EXPERT_ROW:ROUTER_EXPERT_ROW + N_EXPERTS].set(b_router_expert[0])
    lower = jnp.tril(jnp.ones((N_EXPERTS, N_EXPERTS), F32), k=-1).astype(BF16)
    n_tiles = t // SEQ_TILE
    bgroups = EXPERT_BLOCK // ROW_GROUP
    max_groups = (2 * t + n_tiles * N_EXPERTS * (ROW_GROUP - 1)) // ROW_GROUP
    n_blocks = -(-max_groups // bgroups) + N_EXPERTS
    even = max(1, (n_tiles * SUBLANES) // math.gcd(n_tiles * SUBLANES, EXPERT_BLOCK))
    n_blocks = -(-n_blocks // even) * even
    h2, xs, rinfo, seg, counts, xbuf0 = _xattn_router(
        h1, norm_xattn, w_q[0].astype(BF16), kt, v, w_o[0].astype(BF16), norm_moe,
        jnp.concatenate([wr_hi, wr_lo], axis=1), br[None, :], upper, lower, n_blocks * EXPERT_BLOCK)

    seg = seg.reshape(n_tiles, N_EXPERTS, LANES)[:, :, 0:3].astype(jnp.int32)
    seg_groups, seg_local, seg_before = seg[:, :, 0], seg[:, :, 1], seg[:, :, 2]
    sizes = counts[:, 0].astype(jnp.int32)
    padded = ((sizes + bgroups - 1) // bgroups) * bgroups
    pad_end = jnp.cumsum(padded)
    pad_start = pad_end - padded
    seg_dst = pad_start[None, :] + seg_before
    table = jnp.concatenate([seg_groups, seg_dst, seg_local], axis=1)[:, None, :]
    block_group = jnp.arange(n_blocks, dtype=jnp.int32) * bgroups
    block_expert = jnp.minimum(jnp.sum((pad_end[None, :] <= block_group[:, None]).astype(jnp.int32), axis=-1),
                               N_EXPERTS - 1)
    n_used = (pad_end[-1:] // bgroups).astype(jnp.int32)

    xbuf = _dispatch(table, xs, xbuf0)
    ybuf = _experts(block_expert, n_used, xbuf, w_gate[0], w_up[0], w_down[0])
    out = _combine(table, h2.reshape(t, d), rinfo, norm_final[None, :], ybuf)
    return out.reshape(bsz, seq, d)
```

```python
import math

import jax
import jax.numpy as jnp
from jax import lax
from jax.experimental import pallas as pl
from jax.experimental.pallas import tpu as pltpu

F32 = jnp.float32
BF16 = jnp.bfloat16
EPS = 1e-6

LANES = 128
SUBLANES = 8
VMEM_LIMIT_BYTES = 56 * 1024 * 1024

D_MODEL = 1024
SSD_WIDTH = 512
SSD_HEAD_DIM = 64
SSD_HEADS = 8
SSD_GROUPS = 2
SSD_HEADS_PER_GROUP = 4
SSD_STATE = 64
SSD_CHUNK = 128
XBC_WIDTH = SSD_WIDTH + 2 * SSD_GROUPS * SSD_STATE
SC_WIDTH = 512
XA_HEADS = 4
XA_HEAD_DIM = 256
N_GROUPS_MOE = 4
EXPERTS_PER_GROUP = 8
N_EXPERTS = 32
D_EXPERT = 512

COL_Z = 0
COL_XBC = COL_Z + SSD_WIDTH
COL_GB = COL_XBC + XBC_WIDTH
COL_GC = COL_GB + SC_WIDTH
COL_V = COL_GC + SC_WIDTH
COL_DT = COL_V + SC_WIDTH
PROJ_WIDTH = COL_DT + LANES

SEQ_TILE = 512
ROW_TILE = 512
EXPERT_BLOCK = 512
HALO = SUBLANES
ROW_GROUP = SUBLANES
LOCAL_ROWS = 2 * SEQ_TILE + N_EXPERTS * ROW_GROUP
ROW_WORDS = D_MODEL // 2 + LANES

ROUTER_GROUP_ROW = 0
ROUTER_EXPERT_ROW = SUBLANES


def _rms(x, g):
    return x * lax.rsqrt(jnp.mean(x * x, axis=-1, keepdims=True) + EPS) * g


def _silu(x):
    return x * jax.nn.sigmoid(x)


def _dot(a, b):
    return jnp.dot(a, b, preferred_element_type=F32)


def _pack_halves(v):
    w = v.shape[1] // 2
    lo = lax.shift_right_logical(pltpu.bitcast(v[:, :w], jnp.uint32), jnp.uint32(16))
    hi = pltpu.bitcast(v[:, w:], jnp.uint32) & jnp.uint32(0xFFFF0000)
    return lo | hi


def _unpack_halves(words):
    lo = pltpu.bitcast(lax.shift_left(words, jnp.uint32(16)), F32).astype(BF16)
    hi = pltpu.bitcast(words & jnp.uint32(0xFFFF0000), F32).astype(BF16)
    return lo, hi


def _mixer_kernel(x_ref, gmix_ref, win_ref, cw_ref, cb_ref, dtb_ref, a_ref, dskip_ref, gate_ref,
                  scw_ref, wout_ref, tri_ref, wg_ref, wu_ref, wd_ref, h_ref, wgu_ref, wdb_ref,
                  conv_ref, cv_ref, act_ref, dt_ref, y_ref, state_ref, sinc_ref, loff_ref, elast_ref):
    ts = x_ref.shape[1]
    wgu_ref[0, :, :D_EXPERT] = wg_ref[0].astype(BF16)
    wgu_ref[0, :, D_EXPERT:] = wu_ref[0].astype(BF16)
    wdb_ref[0] = wd_ref[0].astype(BF16)
    s = pl.program_id(1)
    x = x_ref[0]
    nb = _rms(x, gmix_ref[...]).astype(BF16)

    @pl.when(s == 0)
    def _():
        conv_ref[0:HALO, :] = jnp.zeros((HALO, XBC_WIDTH), F32)
        cv_ref[0:HALO, :] = jnp.zeros((HALO, SC_WIDTH), F32)
        state_ref[...] = jnp.zeros(state_ref.shape, F32)

    @pl.when(s > 0)
    def _():
        conv_ref[0:HALO, :] = conv_ref[ts:ts + HALO, :]
        cv_ref[0:HALO, :] = cv_ref[ts:ts + HALO, :]

    xbc = _dot(nb, win_ref[:, COL_XBC:COL_XBC + XBC_WIDTH])
    conv_ref[HALO:HALO + ts, :] = xbc
    acc = cb_ref[...] + cw_ref[3:4, :] * xbc
    for i in range(3):
        acc = acc + cw_ref[i:i + 1, :] * conv_ref[HALO - 3 + i:HALO - 3 + i + ts, :]
    act_ref[...] = _silu(acc)

    dtr = _dot(nb, win_ref[:, COL_DT:COL_DT + LANES]) + dtb_ref[...]
    dt_ref[...] = jnp.maximum(dtr, 0.0) + jnp.log1p(jnp.exp(-jnp.abs(dtr)))

    row = lax.broadcasted_iota(jnp.int32, (SSD_CHUNK, SSD_CHUNK), 0)
    col = lax.broadcasted_iota(jnp.int32, (SSD_CHUNK, SSD_CHUNK), 1)
    causal = row >= col
    lo_q = col < SSD_HEAD_DIM
    lo_n = lax.broadcasted_iota(jnp.int32, (SSD_STATE, SSD_CHUNK), 1) < SSD_HEAD_DIM
    lo_e = lax.broadcasted_iota(jnp.int32, (SUBLANES, SSD_CHUNK), 1) < SSD_HEAD_DIM

    def chunk_prologue(c):
        rows = slice(c * SSD_CHUNK, (c + 1) * SSD_CHUNK)
        dt = dt_ref[rows, :]
        adt = dt * a_ref[...]
        cs = jnp.dot(tri_ref[...], adt, preferred_element_type=F32,
                     precision=lax.Precision.HIGHEST)
        bm = act_ref[rows, SSD_WIDTH:SSD_WIDTH + SSD_GROUPS * SSD_STATE]
        cm = act_ref[rows, SSD_WIDTH + SSD_GROUPS * SSD_STATE:XBC_WIDTH]
        cbs = []
        for g in range(SSD_GROUPS):
            cg = cm[:, g * SSD_STATE:(g + 1) * SSD_STATE]
            bg = bm[:, g * SSD_STATE:(g + 1) * SSD_STATE]
            cbs.append(lax.dot_general(cg.astype(BF16), bg.astype(BF16), (((1,), (1,)), ((), ())),
                                       preferred_element_type=F32))
        return cs, cs.T, dt.T, bm.T, cm, pltpu.roll(cm, SSD_STATE, axis=1), cbs

    def chunk_body(c, prologue):
        rows = slice(c * SSD_CHUNK, (c + 1) * SSD_CHUNK)
        cs, cs_t, dt_t, bm_t, cm, cm_sw, cbs = prologue
        for g in range(SSD_GROUPS):
            bg_t = bm_t[g * SSD_STATE:(g + 1) * SSD_STATE, :]
            cb = cbs[g]
            c2 = jnp.where(lo_q, cm, cm_sw) if g == 0 else jnp.where(lo_q, cm_sw, cm)
            for jj in range(SSD_HEADS_PER_GROUP // 2):
                j = g * (SSD_HEADS_PER_GROUP // 2) + jj
                idx = c * (SSD_HEADS // 2) + j
                xp = act_ref[rows, j * LANES:(j + 1) * LANES]
                rhs = jnp.concatenate(
                    [jnp.where(lo_q, xp, 0.0), jnp.where(lo_q, 0.0, xp)], axis=0).astype(BF16)
                diag, off, w_rows, e_last = [], [], [], []
                for h in (2 * j, 2 * j + 1):
                    cs_col = jnp.broadcast_to(cs[:, h:h + 1], (SSD_CHUNK, SSD_CHUNK))
                    cs_row = cs_t[h:h + 1, :]
                    dt_row = dt_t[h:h + 1, :]
                    decay = jnp.exp(jnp.where(causal, cs_col - cs_row, -jnp.inf))
                    diag.append((cb * dt_row) * decay)
                    off.append(jnp.exp(cs_col))
                    cl = cs_t[h:h + 1, SSD_CHUNK - 1:SSD_CHUNK]
                    w_rows.append(jnp.exp(cl - cs_row) * dt_row)
                    e_last.append(jnp.exp(cl))
                y_ref[rows, j * LANES:(j + 1) * LANES] = _dot(jnp.concatenate(diag, axis=1).astype(BF16), rhs)
                lhs2 = jnp.concatenate([bg_t * w_rows[0], bg_t * w_rows[1]], axis=1).astype(BF16)
                sinc_ref[idx] = _dot(lhs2, rhs)
                loff_ref[idx] = (c2 * jnp.where(lo_q, off[0], off[1])).astype(BF16)
                elast_ref[idx] = jnp.where(lo_e, e_last[0], e_last[1])

    n_chunks = ts // SSD_CHUNK
    prologues = [chunk_prologue(c) for c in range(n_chunks)]

    g_c = _dot(nb, win_ref[:, COL_GC:COL_GC + SC_WIDTH])
    v = _dot(nb, win_ref[:, COL_V:COL_V + SC_WIDTH])
    u = g_c * v
    cv_ref[HALO:HALO + ts, :] = u
    conv3 = scw_ref[2:3, :] * u
    for i in range(2):
        conv3 = conv3 + scw_ref[i:i + 1, :] * cv_ref[HALO - 2 + i:HALO - 2 + i + ts, :]
    g_b = _dot(nb, win_ref[:, COL_GB:COL_GB + SC_WIDTH])
    out_sc = _dot((g_b * conv3).astype(BF16), wout_ref[SSD_WIDTH:, :])
    gate_z = _silu(_dot(nb, win_ref[:, COL_Z:COL_Z + SSD_WIDTH]))

    for c in range(n_chunks):
        chunk_body(c, prologues[c])

    for c in range(n_chunks):
        for j in range(SSD_HEADS // 2):
            idx = c * (SSD_HEADS // 2) + j
            stp = state_ref[j]
            rhs_st = jnp.concatenate(
                [jnp.where(lo_n, stp, 0.0), jnp.where(lo_n, 0.0, stp)], axis=0).astype(BF16)
            y_ref[c * SSD_CHUNK:(c + 1) * SSD_CHUNK, j * LANES:(j + 1) * LANES] += _dot(loff_ref[idx], rhs_st)
            state_ref[j] = elast_ref[idx][0:1, :] * stp + sinc_ref[idx]

    y = y_ref[...] + act_ref[:, 0:SSD_WIDTH] * dskip_ref[...]
    gated = y * gate_z
    gw = SSD_WIDTH // SSD_GROUPS
    y_ssd = jnp.concatenate(
        [_rms(gated[:, g * gw:(g + 1) * gw], gate_ref[:, g * gw:(g + 1) * gw]) for g in range(SSD_GROUPS)],
        axis=-1).astype(BF16)
    h_ref[0] = x + out_sc + _dot(y_ssd, wout_ref[0:SSD_WIDTH, :])


def _const_spec(shape):
    nd = len(shape)
    return pl.BlockSpec(shape, lambda *_: (0,) * nd)


def _mixer(x, gmix, win, cw, cb, dtb, a_row, dskip, gate, scw, wout, tri, w_gate, w_up, w_down):
    bsz, seq, d = x.shape
    ts = SEQ_TILE
    nt = seq // ts
    consts = (gmix, win, cw, cb, dtb, a_row, dskip, gate, scw, wout, tri)
    n_exp, _, de = w_gate.shape
    per_expert = (bsz * nt) // n_exp
    assert per_expert * n_exp == bsz * nt and d % per_expert == 0 and de % per_expert == 0
    rg, rd = d // per_expert, de // per_expert

    def slice_map(b, s):
        i = b * nt + s
        return (i // per_expert, i % per_expert, 0)
    return pl.pallas_call(
        _mixer_kernel,
        grid=(bsz, nt),
        in_specs=([pl.BlockSpec((1, ts, d), lambda b, s: (b, s, 0))] + [_const_spec(c.shape) for c in consts]
                  + [pl.BlockSpec((1, rg, de), slice_map), pl.BlockSpec((1, rg, de), slice_map),
                     pl.BlockSpec((1, rd, d), slice_map)]),
        out_specs=[pl.BlockSpec((1, ts, d), lambda b, s: (b, s, 0)),
                   pl.BlockSpec((1, rg, 2 * de), slice_map), pl.BlockSpec((1, rd, d), slice_map)],
        out_shape=[jax.ShapeDtypeStruct((bsz, seq, d), F32),
                   jax.ShapeDtypeStruct((n_exp, d, 2 * de), BF16), jax.ShapeDtypeStruct((n_exp, de, d), BF16)],
        scratch_shapes=[
            pltpu.VMEM((ts + HALO, XBC_WIDTH), F32),
            pltpu.VMEM((ts + HALO, SC_WIDTH), F32),
            pltpu.VMEM((ts, XBC_WIDTH), F32),
            pltpu.VMEM((ts, LANES), F32),
            pltpu.VMEM((ts, SSD_WIDTH), F32),
            pltpu.VMEM((SSD_HEADS // 2, SSD_STATE, 2 * SSD_HEAD_DIM), F32),
            pltpu.VMEM((ts // SSD_CHUNK * SSD_HEADS // 2, SSD_STATE, 2 * SSD_HEAD_DIM), F32),
            pltpu.VMEM((ts // SSD_CHUNK * SSD_HEADS // 2, SSD_CHUNK, 2 * SSD_STATE), BF16),
            pltpu.VMEM((ts // SSD_CHUNK * SSD_HEADS // 2, SUBLANES, 2 * SSD_HEAD_DIM), F32),
        ],
        compiler_params=pltpu.CompilerParams(
            dimension_semantics=("arbitrary", "arbitrary"), vmem_limit_bytes=VMEM_LIMIT_BYTES),
        name="mixer",
    )(x, *consts, w_gate, w_up, w_down)


def _kv_kernel(mem_ref, g_ref, wkv_ref, kt_ref, v_ref):
    d = mem_ref.shape[2]
    mn = _rms(mem_ref[0], g_ref[...]).astype(BF16)
    kv = _dot(mn, wkv_ref[...])
    kt_ref[0] = (kv[:, :d] * (XA_HEAD_DIM ** -0.5)).T.astype(BF16)
    v_ref[0] = kv[:, d:].astype(BF16)


def _memory_kv(mem, g, wkv):
    bsz, m, d = mem.shape
    return pl.pallas_call(
        _kv_kernel,
        grid=(bsz,),
        in_specs=[pl.BlockSpec((1, m, d), lambda b: (b, 0, 0)), _const_spec(g.shape), _const_spec(wkv.shape)],
        out_specs=[pl.BlockSpec((1, d, m), lambda b: (b, 0, 0)), pl.BlockSpec((1, m, d), lambda b: (b, 0, 0))],
        out_shape=[jax.ShapeDtypeStruct((bsz, d, m), BF16), jax.ShapeDtypeStruct((bsz, m, d), BF16)],
        compiler_params=pltpu.CompilerParams(
            dimension_semantics=("arbitrary",), vmem_limit_bytes=VMEM_LIMIT_BYTES),
        name="memory_kv",
    )(mem, g, wkv)


def _xattn_router_kernel(h1_ref, gx_ref, wq_ref, kt_ref, v_ref, wo_ref, gm_ref, wr_ref, br_ref, upper_ref,
                         lower_ref, h2_ref, xs_ref, rinfo_ref, seg_ref, counts_ref, zero_ref, carry_ref):
    ts = h1_ref.shape[1]
    first = jnp.logical_and(pl.program_id(0) == 0, pl.program_id(1) == 0)

    @pl.when(first)
    def _():
        carry_ref[...] = jnp.zeros(carry_ref.shape, F32)

    h1 = h1_ref[0]
    q = _dot(_rms(h1, gx_ref[...]).astype(BF16), wq_ref[...]).astype(BF16)
    outs = []
    for h in range(XA_HEADS):
        hs = slice(h * XA_HEAD_DIM, (h + 1) * XA_HEAD_DIM)
        sc = _dot(q[:, hs], kt_ref[0, hs, :])
        p = jnp.exp(sc - jnp.max(sc, axis=-1, keepdims=True))
        o = _dot(p.astype(BF16), v_ref[0, :, hs]) / jnp.sum(p, axis=-1, keepdims=True)
        outs.append(o.astype(BF16))
    h2 = h1 + _dot(jnp.concatenate(outs, axis=-1), wo_ref[...])
    h2_ref[0] = h2

    n3 = _rms(h2, gm_ref[...])
    n3b = n3.astype(BF16)
    half = n3.shape[1] // 2
    n3r = n3b.astype(F32)
    zero_ref[...] = jnp.zeros(zero_ref.shape, zero_ref.dtype)

    n3_lo = (n3 - n3r).astype(BF16)
    lg2 = _dot(n3b, wr_ref[...])
    logits = lg2[:, :LANES] + lg2[:, LANES:] + _dot(n3_lo, wr_ref[:, :LANES]) + br_ref[...]
    lt = logits.T

    l0, l1, l2, l3 = (lt[ROUTER_GROUP_ROW + i:ROUTER_GROUP_ROW + i + 1, :] for i in range(N_GROUPS_MOE))
    m = jnp.maximum(jnp.maximum(l0, l1), jnp.maximum(l2, l3))
    gidx = jnp.where(l0 == m, 0, jnp.where(l1 == m, 1, jnp.where(l2 == m, 2, 3)))
    g_w = 1.0 / (jnp.exp(l0 - m) + jnp.exp(l1 - m) + jnp.exp(l2 - m) + jnp.exp(l3 - m))

    def group_rows(g):
        r = ROUTER_EXPERT_ROW + g * EXPERTS_PER_GROUP
        return lt[r:r + EXPERTS_PER_GROUP, :]

    el = jnp.where(gidx == 0, group_rows(0),
                   jnp.where(gidx == 1, group_rows(1), jnp.where(gidx == 2, group_rows(2), group_rows(3))))
    sub = lax.broadcasted_iota(jnp.int32, el.shape, 0)
    m1 = jnp.max(el, axis=0, keepdims=True)
    i1 = jnp.min(jnp.where(el == m1, sub, EXPERTS_PER_GROUP), axis=0, keepdims=True)
    el2 = jnp.where(sub == i1, -jnp.inf, el)
    m2 = jnp.max(el2, axis=0, keepdims=True)
    i2 = jnp.min(jnp.where(el2 == m2, sub, EXPERTS_PER_GROUP), axis=0, keepdims=True)
    e2 = jnp.exp(m2 - m1)
    w1 = 1.0 / (1.0 + e2)
    w2 = e2 / (1.0 + e2)
    e_a = gidx * EXPERTS_PER_GROUP + i1
    e_b = gidx * EXPERTS_PER_GROUP + i2

    eio = lax.broadcasted_iota(jnp.int32, (N_EXPERTS, ts), 0)
    oh_a = (eio == e_a).astype(F32)
    oh_b = (eio == e_b).astype(F32)
    prefix = _dot(jnp.concatenate([oh_a, oh_b], axis=0).astype(BF16), upper_ref[...])
    tot_a = jnp.sum(oh_a, axis=1, keepdims=True)
    tot_b = jnp.sum(oh_b, axis=1, keepdims=True)
    groups = jnp.broadcast_to(jnp.floor((tot_a + tot_b + (ROW_GROUP - 1)) * (1.0 / ROW_GROUP)),
                              (N_EXPERTS, LANES))
    first_group = _dot(lower_ref[...], groups.astype(BF16))
    base = ROW_GROUP * first_group[:, 0:1]
    pos_a = jnp.sum(oh_a * (base + prefix[:N_EXPERTS]), axis=0, keepdims=True)
    pos_b = jnp.sum(oh_b * (base + tot_a + prefix[N_EXPERTS:]), axis=0, keepdims=True)

    carry = carry_ref[...]
    lane = lax.broadcasted_iota(jnp.int32, (N_EXPERTS, LANES), 1)
    seg_ref[...] = jnp.where(lane == 0, groups, jnp.where(lane == 1, first_group,
                                                          jnp.where(lane == 2, carry, 0.0)))
    carry_ref[...] = carry + groups
    counts_ref[...] = carry + groups

    n_local = xs_ref.shape[0]
    riota = lax.broadcasted_iota(jnp.int32, (n_local, ts), 0)
    sel_a = riota == pos_a.astype(jnp.int32)
    sel_b = riota == pos_b.astype(jnp.int32)
    perm = (jnp.where(sel_a, 1.0, 0.0) + jnp.where(sel_b, 1.0, 0.0)).astype(BF16)
    xs = _dot(perm, n3b)
    xs_ref[:, :half] = _pack_halves(xs)
    w_sorted = jnp.sum(jnp.where(sel_a, g_w * w1, 0.0) + jnp.where(sel_b, g_w * w2, 0.0),
                       axis=1, keepdims=True)
    xs_ref[:, half:] = pltpu.bitcast(jnp.broadcast_to(w_sorted, (n_local, LANES)), jnp.uint32)

    res = jnp.concatenate([pos_a, pos_b, jnp.zeros((LANES - 2, ts), F32)], axis=0)
    rinfo_ref[...] = res.T


def _xattn_router(h1, gx, wq, kt, v, wo, gm, wr, br, upper, lower, n_rows):
    bsz, seq, d = h1.shape
    ts = SEQ_TILE
    m = v.shape[1]
    nt = seq // ts
    n_tiles = bsz * nt
    zero_rows = n_rows // n_tiles
    assert zero_rows * n_tiles == n_rows and zero_rows % SUBLANES == 0
    tok_spec = pl.BlockSpec((1, ts, d), lambda b, s: (b, s, 0))
    return pl.pallas_call(
        _xattn_router_kernel,
        grid=(bsz, nt),
        in_specs=[tok_spec, _const_spec(gx.shape), _const_spec(wq.shape),
                  pl.BlockSpec((1, d, m), lambda b, s: (b, 0, 0)),
                  pl.BlockSpec((1, m, d), lambda b, s: (b, 0, 0)),
                  _const_spec(wo.shape), _const_spec(gm.shape), _const_spec(wr.shape), _const_spec(br.shape),
                  _const_spec(upper.shape), _const_spec(lower.shape)],
        out_specs=[tok_spec,
                   pl.BlockSpec((LOCAL_ROWS, ROW_WORDS), lambda b, s: (b * nt + s, 0)),
                   pl.BlockSpec((ts, LANES), lambda b, s: (b * nt + s, 0)),
                   pl.BlockSpec((N_EXPERTS, LANES), lambda b, s: (b * nt + s, 0)),
                   pl.BlockSpec((N_EXPERTS, LANES), lambda b, s: (0, 0)),
                   pl.BlockSpec((zero_rows, ROW_WORDS), lambda b, s: (b * nt + s, 0))],
        out_shape=[jax.ShapeDtypeStruct((bsz, seq, d), F32),
                   jax.ShapeDtypeStruct((n_tiles * LOCAL_ROWS, ROW_WORDS), jnp.uint32),
                   jax.ShapeDtypeStruct((bsz * seq, LANES), F32),
                   jax.ShapeDtypeStruct((n_tiles * N_EXPERTS, LANES), F32),
                   jax.ShapeDtypeStruct((N_EXPERTS, LANES), F32),
                   jax.ShapeDtypeStruct((n_rows, ROW_WORDS), jnp.uint32)],
        scratch_shapes=[pltpu.VMEM((N_EXPERTS, LANES), F32)],
        compiler_params=pltpu.CompilerParams(
            dimension_semantics=("arbitrary", "arbitrary"), vmem_limit_bytes=VMEM_LIMIT_BYTES),
        name="xattn_router",
    )(h1, gx, wq, kt, v, wo, gm, wr, br, upper, lower)


TAB_COUNT, TAB_DST, TAB_LOCAL = (k * N_EXPERTS for k in range(3))


SEGMENT_RUN_LOG2 = 3
SEGMENT_CHUNKS = (4, 2, 1)


def _rows(first_group, n_groups):
    return pl.ds(pl.multiple_of(first_group * ROW_GROUP, ROW_GROUP), n_groups * ROW_GROUP)


def _for_each_segment_copy(tab_ref, make_copy, action):
    run = 1 << SEGMENT_RUN_LOG2

    def expert(e, c):
        n = tab_ref[0, 0, TAB_COUNT + e]
        dst0 = tab_ref[0, 0, TAB_DST + e]
        loc0 = tab_ref[0, 0, TAB_LOCAL + e]

        def full_run(k, c2):
            action(make_copy(loc0 + k * run, dst0 + k * run, run))
            return c2
        lax.fori_loop(0, lax.shift_right_logical(n, SEGMENT_RUN_LOG2), full_run, 0)
        for size in SEGMENT_CHUNKS:
            off = n & (-2 * size)

            @pl.when((n & size) != 0)
            def _():
                action(make_copy(loc0 + off, dst0 + off, size))
        return c
    lax.fori_loop(0, N_EXPERTS, expert, 0)


def _dispatch_kernel(tab_ref, xs_ref, buf_in_ref, buf_ref, sem):
    del buf_in_ref

    def make_copy(local, dst, n_groups):
        return pltpu.make_async_copy(xs_ref.at[_rows(local, n_groups)], buf_ref.at[_rows(dst, n_groups)], sem)

    _for_each_segment_copy(tab_ref, make_copy, lambda cp: cp.start())
    _for_each_segment_copy(tab_ref, make_copy, lambda cp: cp.wait())


def _dispatch(table, xs, buf0):
    n_tiles = table.shape[0]
    return pl.pallas_call(
        _dispatch_kernel,
        grid=(n_tiles,),
        in_specs=[pl.BlockSpec((1, 1, table.shape[2]), lambda i: (i, 0, 0), memory_space=pltpu.SMEM),
                  pl.BlockSpec((LOCAL_ROWS, xs.shape[1]), lambda i: (i, 0)),
                  pl.BlockSpec(memory_space=pl.ANY)],
        out_specs=pl.BlockSpec(memory_space=pl.ANY),
        out_shape=jax.ShapeDtypeStruct(buf0.shape, buf0.dtype),
        scratch_shapes=[pltpu.SemaphoreType.DMA],
        input_output_aliases={2: 0},
        compiler_params=pltpu.CompilerParams(
            dimension_semantics=("arbitrary",), vmem_limit_bytes=VMEM_LIMIT_BYTES),
        name="dispatch",
    )(table, xs, buf0)


def _expert_kernel(be_ref, nused_ref, xb_ref, wgu_ref, wd_ref, y_ref):
    del be_ref
    i = pl.program_id(0)
    used = i < nused_ref[0]

    @pl.when(used)
    def _():
        half = xb_ref.shape[1] - LANES
        x_lo, x_hi = _unpack_halves(xb_ref[:, :half])
        gu = _dot(x_lo, wgu_ref[0, :half, :]) + _dot(x_hi, wgu_ref[0, half:, :])
        mid = (_silu(gu[:, :D_EXPERT]) * gu[:, D_EXPERT:]).astype(BF16)
        comb = pltpu.bitcast(xb_ref[:, half:half + LANES], F32)[:, 0:1]
        y = _dot(mid, wd_ref[0]) * comb
        y_ref[...] = _pack_halves(y.astype(BF16).astype(F32))

    @pl.when(jnp.logical_not(used))
    def _():
        y_ref[...] = jnp.zeros(y_ref.shape, y_ref.dtype)


def _experts(block_expert, n_used, xbuf, wgu, wd):
    w = xbuf.shape[1]
    bm = EXPERT_BLOCK
    n_blocks = block_expert.shape[0]
    n_rows = n_blocks * bm
    d = wd.shape[2]
    grid_spec = pltpu.PrefetchScalarGridSpec(
        num_scalar_prefetch=2,
        grid=(n_blocks,),
        in_specs=[pl.BlockSpec((bm, w), lambda i, be, nu: (i, 0)),
                  pl.BlockSpec((1,) + wgu.shape[1:], lambda i, be, nu: (be[i], 0, 0)),
                  pl.BlockSpec((1,) + wd.shape[1:], lambda i, be, nu: (be[i], 0, 0))],
        out_specs=pl.BlockSpec((bm, d // 2), lambda i, be, nu: (i, 0)),
    )
    return pl.pallas_call(
        _expert_kernel,
        grid_spec=grid_spec,
        out_shape=jax.ShapeDtypeStruct((n_rows, d // 2), jnp.uint32),
        compiler_params=pltpu.CompilerParams(
            dimension_semantics=("arbitrary",), vmem_limit_bytes=VMEM_LIMIT_BYTES),
        name="experts",
    )(block_expert, n_used, xbuf, wgu, wd)


def _combine_kernel(tab_ref, tab_next_ref, h2_ref, rinfo_ref, gf_ref, ybuf_ref, out_ref, ys_ref, sem):
    tr = h2_ref.shape[0]
    n_local = ys_ref.shape[1]
    i = pl.program_id(0)
    slot = i % 2

    def copier(s):
        def make_copy(local, src, n_groups):
            return pltpu.make_async_copy(ybuf_ref.at[_rows(src, n_groups)],
                                         ys_ref.at[s, _rows(local, n_groups)], sem.at[s])
        return make_copy

    def gather(tab, s):
        covered = tab[0, 0, TAB_LOCAL + N_EXPERTS - 1] + tab[0, 0, TAB_COUNT + N_EXPERTS - 1]

        def zero(g, c):
            ys_ref[s, _rows(g, 1), :] = jnp.zeros((ROW_GROUP, ys_ref.shape[2]), ys_ref.dtype)
            return c
        lax.fori_loop(covered, n_local // ROW_GROUP, zero, 0)
        _for_each_segment_copy(tab, copier(s), lambda cp: cp.start())

    @pl.when(i == 0)
    def _():
        gather(tab_ref, 0)

    @pl.when(i + 1 < pl.num_programs(0))
    def _():
        gather(tab_next_ref, 1 - slot)

    _for_each_segment_copy(tab_ref, copier(slot), lambda cp: cp.wait())

    info = rinfo_ref[...]
    ciota = lax.broadcasted_iota(jnp.int32, (tr, n_local), 1)
    pick = (jnp.where(ciota == info[:, 0:1].astype(jnp.int32), 1.0, 0.0)
            + jnp.where(ciota == info[:, 1:2].astype(jnp.int32), 1.0, 0.0)).astype(BF16)
    y_lo, y_hi = _unpack_halves(ys_ref[slot])
    h = h2_ref[...] + jnp.concatenate([_dot(pick, y_lo), _dot(pick, y_hi)], axis=-1)
    out_ref[...] = _rms(h, gf_ref[...])


def _combine(table, h2, rinfo, gf, ybuf):
    t, d = h2.shape
    tr = SEQ_TILE
    n_tiles = t // tr
    tab_spec = (1, 1, table.shape[2])
    return pl.pallas_call(
        _combine_kernel,
        grid=(n_tiles,),
        in_specs=[pl.BlockSpec(tab_spec, lambda i: (i, 0, 0), memory_space=pltpu.SMEM),
                  pl.BlockSpec(tab_spec, lambda i: (jnp.minimum(i + 1, n_tiles - 1), 0, 0),
                               memory_space=pltpu.SMEM),
                  pl.BlockSpec((tr, d), lambda i: (i, 0)),
                  pl.BlockSpec((tr, LANES), lambda i: (i, 0)),
                  _const_spec(gf.shape),
                  pl.BlockSpec(memory_space=pl.ANY)],
        out_specs=pl.BlockSpec((tr, d), lambda i: (i, 0)),
        out_shape=jax.ShapeDtypeStruct((t, d), F32),
        scratch_shapes=[pltpu.VMEM((2, LOCAL_ROWS, d // 2), jnp.uint32), pltpu.SemaphoreType.DMA((2,))],
        compiler_params=pltpu.CompilerParams(
            dimension_semantics=("arbitrary",), vmem_limit_bytes=VMEM_LIMIT_BYTES),
        name="combine",
    )(table, table, h2, rinfo, gf, ybuf)


def _pad_lanes(v, fill=0.0):
    return jnp.pad(v, (0, LANES - v.shape[0]), constant_values=fill)[None, :]


def kernel(x, mem, norm_mem, norm_mix, w_in, conv_ssd_w, conv_ssd_b, dt_bias, a_log, d_skip, norm_ssd_gate,
           conv_short_w, w_out, norm_xattn, w_q, w_kv, w_o, norm_moe, w_router_group, b_router_group,
           w_router_expert, b_router_expert, w_gate, w_up, w_down, norm_final):
    bsz, seq, d = x.shape
    t = bsz * seq
    assert w_in.shape[0] == 1, "single-layer model"
    assert d == D_MODEL and seq % SEQ_TILE == 0 and t % ROW_TILE == 0

    tri = jnp.tril(jnp.ones((SSD_CHUNK, SSD_CHUNK), F32))
    upper = jnp.triu(jnp.ones((SEQ_TILE, SEQ_TILE), F32), k=1).astype(BF16)

    o2 = SSD_WIDTH + XBC_WIDTH
    o3 = o2 + SSD_HEADS
    wl = w_in[0]
    win = jnp.concatenate(
        [wl[:, :o2], wl[:, o3:], wl[:, o2:o3], jnp.zeros((d, LANES - SSD_HEADS), F32)], axis=1).astype(BF16)
    h1, wgu, wdb = _mixer(x, norm_mix, win, conv_ssd_w[0], conv_ssd_b, _pad_lanes(dt_bias[0]),
                _pad_lanes(-jnp.exp(a_log[0])), jnp.repeat(d_skip[0], SSD_HEAD_DIM)[None, :], norm_ssd_gate,
                conv_short_w[0], w_out[0].astype(BF16), tri, w_gate[0], w_up[0], w_down[0])

    kt, v = _memory_kv(mem, norm_mem[None, :], w_kv[0].astype(BF16))
    wr = jnp.zeros((d, LANES), F32)
    wr = wr.at[:, ROUTER_GROUP_ROW:ROUTER_GROUP_ROW + N_GROUPS_MOE].set(w_router_group[0])
    wr = wr.at[:, ROUTER_EXPERT_ROW:ROUTER_EXPERT_ROW + N_EXPERTS].set(w_router_expert[0])
    wr_hi = wr.astype(BF16)
    wr_lo = (wr - wr_hi.astype(F32)).astype(BF16)
    br = jnp.zeros((LANES,), F32)
    br = br.at[ROUTER_GROUP_ROW:ROUTER_GROUP_ROW + N_GROUPS_MOE].set(b_router_group[0])
    br = br.at[ROUTER_EXPERT_ROW:ROUTER_EXPERT_ROW + N_EXPERTS].set(b_router_expert[0])
    lower = jnp.tril(jnp.ones((N_EXPERTS, N_EXPERTS), F32), k=-1).astype(BF16)
    n_tiles = t // SEQ_TILE
    bgroups = EXPERT_BLOCK // ROW_GROUP
    max_groups = (2 * t + n_tiles * N_EXPERTS * (ROW_GROUP - 1)) // ROW_GROUP
    n_blocks = -(-max_groups // bgroups) + N_EXPERTS
    even = max(1, (n_tiles * SUBLANES) // math.gcd(n_tiles * SUBLANES, EXPERT_BLOCK))
    n_blocks = -(-n_blocks // even) * even
    h2, xs, rinfo, seg, counts, xbuf0 = _xattn_router(
        h1, norm_xattn, w_q[0].astype(BF16), kt, v, w_o[0].astype(BF16), norm_moe,
        jnp.concatenate([wr_hi, wr_lo], axis=1), br[None, :], upper, lower, n_blocks * EXPERT_BLOCK)

    seg = seg.reshape(n_tiles, N_EXPERTS, LANES)[:, :, 0:3].astype(jnp.int32)
    seg_groups, seg_local, seg_before = seg[:, :, 0], seg[:, :, 1], seg[:, :, 2]
    sizes = counts[:, 0].astype(jnp.int32)
    padded = ((sizes + bgroups - 1) // bgroups) * bgroups
    pad_end = jnp.cumsum(padded)
    pad_start = pad_end - padded
    seg_dst = pad_start[None, :] + seg_before
    table = jnp.concatenate([seg_groups, seg_dst, seg_local], axis=1)[:, None, :]
    block_group = jnp.arange(n_blocks, dtype=jnp.int32) * bgroups
    block_expert = jnp.minimum(jnp.sum((pad_end[None, :] <= block_group[:, None]).astype(jnp.int32), axis=-1),
                               N_EXPERTS - 1)
    n_used = (pad_end[-1:] // bgroups).astype(jnp.int32)

    xbuf = _dispatch(table, xs, xbuf0)
    ybuf = _experts(block_expert, n_used, xbuf, wgu, wdb)
    out = _combine(table, h2.reshape(t, d), rinfo, norm_final[None, :], ybuf)
    return out.reshape(bsz, seq, d)
```

```python
import math

import jax
import jax.numpy as jnp
from jax import lax
from jax.experimental import pallas as pl
from jax.experimental.pallas import tpu as pltpu

F32 = jnp.float32
BF16 = jnp.bfloat16
EPS = 1e-6

LANES = 128
SUBLANES = 8
VMEM_LIMIT_BYTES = 56 * 1024 * 1024

D_MODEL = 1024
SSD_WIDTH = 512
SSD_HEAD_DIM = 64
SSD_HEADS = 8
SSD_GROUPS = 2
SSD_HEADS_PER_GROUP = 4
SSD_STATE = 64
SSD_CHUNK = 128
XBC_WIDTH = SSD_WIDTH + 2 * SSD_GROUPS * SSD_STATE
SC_WIDTH = 512
XA_HEADS = 4
XA_HEAD_DIM = 256
N_GROUPS_MOE = 4
EXPERTS_PER_GROUP = 8
N_EXPERTS = 32
D_EXPERT = 512

COL_Z = 0
COL_XBC = COL_Z + SSD_WIDTH
COL_GB = COL_XBC + XBC_WIDTH
COL_GC = COL_GB + SC_WIDTH
COL_V = COL_GC + SC_WIDTH
COL_DT = COL_V + SC_WIDTH
PROJ_WIDTH = COL_DT + LANES

SEQ_TILE = 512
ROW_TILE = 512
EXPERT_BLOCK = 512
HALO = SUBLANES
ROW_GROUP = SUBLANES
LOCAL_ROWS = 2 * SEQ_TILE + N_EXPERTS * ROW_GROUP
ROW_WORDS = D_MODEL // 2 + LANES
PERM_CHUNKS = 4

ROUTER_GROUP_ROW = 0
ROUTER_EXPERT_ROW = SUBLANES


def _rms(x, g):
    return x * lax.rsqrt(jnp.mean(x * x, axis=-1, keepdims=True) + EPS) * g


def _silu(x):
    return x * jax.nn.sigmoid(x)


def _dot(a, b):
    return jnp.dot(a, b, preferred_element_type=F32)


def _pack_halves(v):
    w = v.shape[1] // 2
    lo = lax.shift_right_logical(pltpu.bitcast(v[:, :w], jnp.uint32), jnp.uint32(16))
    hi = pltpu.bitcast(v[:, w:], jnp.uint32) & jnp.uint32(0xFFFF0000)
    return lo | hi


def _unpack_halves(words):
    lo = pltpu.bitcast(lax.shift_left(words, jnp.uint32(16)), F32).astype(BF16)
    hi = pltpu.bitcast(words & jnp.uint32(0xFFFF0000), F32).astype(BF16)
    return lo, hi


def _mixer_kernel(x_ref, gmix_ref, win_ref, cw_ref, cb_ref, dtb_ref, a_ref, dskip_ref, gate_ref,
                  scw_ref, wout_ref, tri_ref, wg_ref, wu_ref, wd_ref, h_ref, wgu_ref, wdb_ref,
                  conv_ref, cv_ref, act_ref, dt_ref, y_ref, state_ref, sinc_ref, loff_ref, elast_ref):
    ts = x_ref.shape[1]
    wgu_ref[0, :, :D_EXPERT] = wg_ref[0].astype(BF16)
    wgu_ref[0, :, D_EXPERT:] = wu_ref[0].astype(BF16)
    wdb_ref[0] = wd_ref[0].astype(BF16)
    s = pl.program_id(1)
    x = x_ref[0]
    nb = _rms(x, gmix_ref[...]).astype(BF16)

    @pl.when(s == 0)
    def _():
        conv_ref[0:HALO, :] = jnp.zeros((HALO, XBC_WIDTH), F32)
        cv_ref[0:HALO, :] = jnp.zeros((HALO, SC_WIDTH), F32)
        state_ref[...] = jnp.zeros(state_ref.shape, F32)

    @pl.when(s > 0)
    def _():
        conv_ref[0:HALO, :] = conv_ref[ts:ts + HALO, :]
        cv_ref[0:HALO, :] = cv_ref[ts:ts + HALO, :]

    xbc = _dot(nb, win_ref[:, COL_XBC:COL_XBC + XBC_WIDTH])
    conv_ref[HALO:HALO + ts, :] = xbc
    acc = cb_ref[...] + cw_ref[3:4, :] * xbc
    for i in range(3):
        acc = acc + cw_ref[i:i + 1, :] * conv_ref[HALO - 3 + i:HALO - 3 + i + ts, :]
    act_ref[...] = _silu(acc)

    dtr = _dot(nb, win_ref[:, COL_DT:COL_DT + LANES]) + dtb_ref[...]
    dt_ref[...] = jnp.maximum(dtr, 0.0) + jnp.log1p(jnp.exp(-jnp.abs(dtr)))

    row = lax.broadcasted_iota(jnp.int32, (SSD_CHUNK, SSD_CHUNK), 0)
    col = lax.broadcasted_iota(jnp.int32, (SSD_CHUNK, SSD_CHUNK), 1)
    causal = row >= col
    lo_q = col < SSD_HEAD_DIM
    lo_n = lax.broadcasted_iota(jnp.int32, (SSD_STATE, SSD_CHUNK), 1) < SSD_HEAD_DIM
    lo_e = lax.broadcasted_iota(jnp.int32, (SUBLANES, SSD_CHUNK), 1) < SSD_HEAD_DIM

    def chunk_prologue(c):
        rows = slice(c * SSD_CHUNK, (c + 1) * SSD_CHUNK)
        dt = dt_ref[rows, :]
        adt = dt * a_ref[...]
        cs = jnp.dot(tri_ref[...], adt, preferred_element_type=F32,
                     precision=lax.Precision.HIGHEST)
        bm = act_ref[rows, SSD_WIDTH:SSD_WIDTH + SSD_GROUPS * SSD_STATE]
        cm = act_ref[rows, SSD_WIDTH + SSD_GROUPS * SSD_STATE:XBC_WIDTH]
        cbs = []
        for g in range(SSD_GROUPS):
            cg = cm[:, g * SSD_STATE:(g + 1) * SSD_STATE]
            bg = bm[:, g * SSD_STATE:(g + 1) * SSD_STATE]
            cbs.append(lax.dot_general(cg.astype(BF16), bg.astype(BF16), (((1,), (1,)), ((), ())),
                                       preferred_element_type=F32))
        return cs, cs.T, dt.T, bm.T, cm, pltpu.roll(cm, SSD_STATE, axis=1), cbs

    def chunk_body(c, prologue):
        rows = slice(c * SSD_CHUNK, (c + 1) * SSD_CHUNK)
        cs, cs_t, dt_t, bm_t, cm, cm_sw, cbs = prologue
        for g in range(SSD_GROUPS):
            bg_t = bm_t[g * SSD_STATE:(g + 1) * SSD_STATE, :]
            cb = cbs[g]
            c2 = jnp.where(lo_q, cm, cm_sw) if g == 0 else jnp.where(lo_q, cm_sw, cm)
            for jj in range(SSD_HEADS_PER_GROUP // 2):
                j = g * (SSD_HEADS_PER_GROUP // 2) + jj
                idx = c * (SSD_HEADS // 2) + j
                xp = act_ref[rows, j * LANES:(j + 1) * LANES]
                rhs = jnp.concatenate(
                    [jnp.where(lo_q, xp, 0.0), jnp.where(lo_q, 0.0, xp)], axis=0).astype(BF16)
                diag, off, w_rows, e_last = [], [], [], []
                for h in (2 * j, 2 * j + 1):
                    cs_col = jnp.broadcast_to(cs[:, h:h + 1], (SSD_CHUNK, SSD_CHUNK))
                    cs_row = cs_t[h:h + 1, :]
                    dt_row = dt_t[h:h + 1, :]
                    decay = jnp.exp(jnp.where(causal, cs_col - cs_row, -jnp.inf))
                    diag.append((cb * dt_row) * decay)
                    off.append(jnp.exp(cs_col))
                    cl = cs_t[h:h + 1, SSD_CHUNK - 1:SSD_CHUNK]
                    w_rows.append(jnp.exp(cl - cs_row) * dt_row)
                    e_last.append(jnp.exp(cl))
                y_ref[rows, j * LANES:(j + 1) * LANES] = _dot(jnp.concatenate(diag, axis=1).astype(BF16), rhs)
                lhs2 = jnp.concatenate([bg_t * w_rows[0], bg_t * w_rows[1]], axis=1).astype(BF16)
                sinc_ref[idx] = _dot(lhs2, rhs)
                loff_ref[idx] = (c2 * jnp.where(lo_q, off[0], off[1])).astype(BF16)
                elast_ref[idx] = jnp.where(lo_e, e_last[0], e_last[1])

    n_chunks = ts // SSD_CHUNK
    prologues = [chunk_prologue(c) for c in range(n_chunks)]

    g_c = _dot(nb, win_ref[:, COL_GC:COL_GC + SC_WIDTH])
    v = _dot(nb, win_ref[:, COL_V:COL_V + SC_WIDTH])
    u = g_c * v
    cv_ref[HALO:HALO + ts, :] = u
    conv3 = scw_ref[2:3, :] * u
    for i in range(2):
        conv3 = conv3 + scw_ref[i:i + 1, :] * cv_ref[HALO - 2 + i:HALO - 2 + i + ts, :]
    g_b = _dot(nb, win_ref[:, COL_GB:COL_GB + SC_WIDTH])
    out_sc = _dot((g_b * conv3).astype(BF16), wout_ref[SSD_WIDTH:, :])
    gate_z = _silu(_dot(nb, win_ref[:, COL_Z:COL_Z + SSD_WIDTH]))

    for c in range(n_chunks):
        chunk_body(c, prologues[c])

    for c in range(n_chunks):
        for j in range(SSD_HEADS // 2):
            idx = c * (SSD_HEADS // 2) + j
            stp = state_ref[j]
            rhs_st = jnp.concatenate(
                [jnp.where(lo_n, stp, 0.0), jnp.where(lo_n, 0.0, stp)], axis=0).astype(BF16)
            y_ref[c * SSD_CHUNK:(c + 1) * SSD_CHUNK, j * LANES:(j + 1) * LANES] += _dot(loff_ref[idx], rhs_st)
            state_ref[j] = elast_ref[idx][0:1, :] * stp + sinc_ref[idx]

    y = y_ref[...] + act_ref[:, 0:SSD_WIDTH] * dskip_ref[...]
    gated = y * gate_z
    gw = SSD_WIDTH // SSD_GROUPS
    y_ssd = jnp.concatenate(
        [_rms(gated[:, g * gw:(g + 1) * gw], gate_ref[:, g * gw:(g + 1) * gw]) for g in range(SSD_GROUPS)],
        axis=-1).astype(BF16)
    h_ref[0] = x + out_sc + _dot(y_ssd, wout_ref[0:SSD_WIDTH, :])


def _const_spec(shape):
    nd = len(shape)
    return pl.BlockSpec(shape, lambda *_: (0,) * nd)


def _mixer(x, gmix, win, cw, cb, dtb, a_row, dskip, gate, scw, wout, tri, w_gate, w_up, w_down):
    bsz, seq, d = x.shape
    ts = SEQ_TILE
    nt = seq // ts
    consts = (gmix, win, cw, cb, dtb, a_row, dskip, gate, scw, wout, tri)
    n_exp, _, de = w_gate.shape
    per_expert = (bsz * nt) // n_exp
    assert per_expert * n_exp == bsz * nt and d % per_expert == 0 and de % per_expert == 0
    rg, rd = d // per_expert, de // per_expert

    def slice_map(b, s):
        i = b * nt + s
        return (i // per_expert, i % per_expert, 0)
    return pl.pallas_call(
        _mixer_kernel,
        grid=(bsz, nt),
        in_specs=([pl.BlockSpec((1, ts, d), lambda b, s: (b, s, 0))] + [_const_spec(c.shape) for c in consts]
                  + [pl.BlockSpec((1, rg, de), slice_map), pl.BlockSpec((1, rg, de), slice_map),
                     pl.BlockSpec((1, rd, d), slice_map)]),
        out_specs=[pl.BlockSpec((1, ts, d), lambda b, s: (b, s, 0)),
                   pl.BlockSpec((1, rg, 2 * de), slice_map), pl.BlockSpec((1, rd, d), slice_map)],
        out_shape=[jax.ShapeDtypeStruct((bsz, seq, d), F32),
                   jax.ShapeDtypeStruct((n_exp, d, 2 * de), BF16), jax.ShapeDtypeStruct((n_exp, de, d), BF16)],
        scratch_shapes=[
            pltpu.VMEM((ts + HALO, XBC_WIDTH), F32),
            pltpu.VMEM((ts + HALO, SC_WIDTH), F32),
            pltpu.VMEM((ts, XBC_WIDTH), F32),
            pltpu.VMEM((ts, LANES), F32),
            pltpu.VMEM((ts, SSD_WIDTH), F32),
            pltpu.VMEM((SSD_HEADS // 2, SSD_STATE, 2 * SSD_HEAD_DIM), F32),
            pltpu.VMEM((ts // SSD_CHUNK * SSD_HEADS // 2, SSD_STATE, 2 * SSD_HEAD_DIM), F32),
            pltpu.VMEM((ts // SSD_CHUNK * SSD_HEADS // 2, SSD_CHUNK, 2 * SSD_STATE), BF16),
            pltpu.VMEM((ts // SSD_CHUNK * SSD_HEADS // 2, SUBLANES, 2 * SSD_HEAD_DIM), F32),
        ],
        compiler_params=pltpu.CompilerParams(
            dimension_semantics=("arbitrary", "arbitrary"), vmem_limit_bytes=VMEM_LIMIT_BYTES),
        name="mixer",
    )(x, *consts, w_gate, w_up, w_down)


def _kv_kernel(mem_ref, g_ref, wkv_ref, kt_ref, v_ref):
    d = mem_ref.shape[2]
    mn = _rms(mem_ref[0], g_ref[...]).astype(BF16)
    kv = _dot(mn, wkv_ref[...])
    kt_ref[0] = (kv[:, :d] * (XA_HEAD_DIM ** -0.5)).T.astype(BF16)
    v_ref[0] = kv[:, d:].astype(BF16)


def _memory_kv(mem, g, wkv):
    bsz, m, d = mem.shape
    return pl.pallas_call(
        _kv_kernel,
        grid=(bsz,),
        in_specs=[pl.BlockSpec((1, m, d), lambda b: (b, 0, 0)), _const_spec(g.shape), _const_spec(wkv.shape)],
        out_specs=[pl.BlockSpec((1, d, m), lambda b: (b, 0, 0)), pl.BlockSpec((1, m, d), lambda b: (b, 0, 0))],
        out_shape=[jax.ShapeDtypeStruct((bsz, d, m), BF16), jax.ShapeDtypeStruct((bsz, m, d), BF16)],
        compiler_params=pltpu.CompilerParams(
            dimension_semantics=("arbitrary",), vmem_limit_bytes=VMEM_LIMIT_BYTES),
        name="memory_kv",
    )(mem, g, wkv)


def _xattn_router_kernel(h1_ref, gx_ref, wq_ref, kt_ref, v_ref, wo_ref, gm_ref, wr_ref, br_ref, upper_ref,
                         lower_ref, h2_ref, xs_ref, rinfo_ref, seg_ref, counts_ref, zero_ref, carry_ref):
    ts = h1_ref.shape[1]
    first = jnp.logical_and(pl.program_id(0) == 0, pl.program_id(1) == 0)

    @pl.when(first)
    def _():
        carry_ref[...] = jnp.zeros(carry_ref.shape, F32)

    h1 = h1_ref[0]
    q = _dot(_rms(h1, gx_ref[...]).astype(BF16), wq_ref[...]).astype(BF16)
    outs = []
    for h in range(XA_HEADS):
        hs = slice(h * XA_HEAD_DIM, (h + 1) * XA_HEAD_DIM)
        sc = _dot(q[:, hs], kt_ref[0, hs, :])
        p = jnp.exp(sc - jnp.max(sc, axis=-1, keepdims=True))
        o = _dot(p.astype(BF16), v_ref[0, :, hs]) / jnp.sum(p, axis=-1, keepdims=True)
        outs.append(o.astype(BF16))
    h2 = h1 + _dot(jnp.concatenate(outs, axis=-1), wo_ref[...])
    h2_ref[0] = h2

    n3 = _rms(h2, gm_ref[...])
    n3b = n3.astype(BF16)
    half = n3.shape[1] // 2
    n3r = n3b.astype(F32)
    zero_ref[...] = jnp.zeros(zero_ref.shape, zero_ref.dtype)

    n3_lo = (n3 - n3r).astype(BF16)
    lg2 = _dot(n3b, wr_ref[...])
    logits = lg2[:, :LANES] + lg2[:, LANES:] + _dot(n3_lo, wr_ref[:, :LANES]) + br_ref[...]
    lt = logits.T

    l0, l1, l2, l3 = (lt[ROUTER_GROUP_ROW + i:ROUTER_GROUP_ROW + i + 1, :] for i in range(N_GROUPS_MOE))
    m = jnp.maximum(jnp.maximum(l0, l1), jnp.maximum(l2, l3))
    gidx = jnp.where(l0 == m, 0, jnp.where(l1 == m, 1, jnp.where(l2 == m, 2, 3)))
    g_w = 1.0 / (jnp.exp(l0 - m) + jnp.exp(l1 - m) + jnp.exp(l2 - m) + jnp.exp(l3 - m))

    def group_rows(g):
        r = ROUTER_EXPERT_ROW + g * EXPERTS_PER_GROUP
        return lt[r:r + EXPERTS_PER_GROUP, :]

    el = jnp.where(gidx == 0, group_rows(0),
                   jnp.where(gidx == 1, group_rows(1), jnp.where(gidx == 2, group_rows(2), group_rows(3))))
    sub = lax.broadcasted_iota(jnp.int32, el.shape, 0)
    m1 = jnp.max(el, axis=0, keepdims=True)
    i1 = jnp.min(jnp.where(el == m1, sub, EXPERTS_PER_GROUP), axis=0, keepdims=True)
    el2 = jnp.where(sub == i1, -jnp.inf, el)
    m2 = jnp.max(el2, axis=0, keepdims=True)
    i2 = jnp.min(jnp.where(el2 == m2, sub, EXPERTS_PER_GROUP), axis=0, keepdims=True)
    e2 = jnp.exp(m2 - m1)
    w1 = 1.0 / (1.0 + e2)
    w2 = e2 / (1.0 + e2)
    e_a = gidx * EXPERTS_PER_GROUP + i1
    e_b = gidx * EXPERTS_PER_GROUP + i2

    eio = lax.broadcasted_iota(jnp.int32, (N_EXPERTS, ts), 0)
    oh_a = (eio == e_a).astype(F32)
    oh_b = (eio == e_b).astype(F32)
    prefix = _dot(jnp.concatenate([oh_a, oh_b], axis=0).astype(BF16), upper_ref[...])
    tot_a = jnp.sum(oh_a, axis=1, keepdims=True)
    tot_b = jnp.sum(oh_b, axis=1, keepdims=True)
    groups = jnp.broadcast_to(jnp.floor((tot_a + tot_b + (ROW_GROUP - 1)) * (1.0 / ROW_GROUP)),
                              (N_EXPERTS, LANES))
    first_group = _dot(lower_ref[...], groups.astype(BF16))
    base = ROW_GROUP * first_group[:, 0:1]
    pos_a = jnp.sum(oh_a * (base + prefix[:N_EXPERTS]), axis=0, keepdims=True)
    pos_b = jnp.sum(oh_b * (base + tot_a + prefix[N_EXPERTS:]), axis=0, keepdims=True)

    carry = carry_ref[...]
    lane = lax.broadcasted_iota(jnp.int32, (N_EXPERTS, LANES), 1)
    seg_ref[...] = jnp.where(lane == 0, groups, jnp.where(lane == 1, first_group,
                                                          jnp.where(lane == 2, carry, 0.0)))
    carry_ref[...] = carry + groups
    counts_ref[...] = carry + groups

    n_local = xs_ref.shape[0]
    chunk = n_local // PERM_CHUNKS
    ipos_a, ipos_b = pos_a.astype(jnp.int32), pos_b.astype(jnp.int32)
    c_a, c_b = g_w * w1, g_w * w2
    for k in range(PERM_CHUNKS):
        rows = slice(k * chunk, (k + 1) * chunk)
        riota = lax.broadcasted_iota(jnp.int32, (chunk, ts), 0) + k * chunk
        sel_a = riota == ipos_a
        sel_b = riota == ipos_b
        perm = jnp.where(sel_a, 1.0, jnp.where(sel_b, 1.0, 0.0)).astype(BF16)
        xs_ref[rows, :half] = _pack_halves(_dot(perm, n3b))
        w_sorted = jnp.sum(jnp.where(sel_a, c_a, jnp.where(sel_b, c_b, 0.0)),
                           axis=1, keepdims=True)
        xs_ref[rows, half:] = pltpu.bitcast(jnp.broadcast_to(w_sorted, (chunk, LANES)), jnp.uint32)

    res = jnp.concatenate([pos_a, pos_b, jnp.zeros((LANES - 2, ts), F32)], axis=0)
    rinfo_ref[...] = res.T


def _xattn_router(h1, gx, wq, kt, v, wo, gm, wr, br, upper, lower, n_rows):
    bsz, seq, d = h1.shape
    ts = SEQ_TILE
    m = v.shape[1]
    nt = seq // ts
    n_tiles = bsz * nt
    zero_rows = n_rows // n_tiles
    assert zero_rows * n_tiles == n_rows and zero_rows % SUBLANES == 0
    tok_spec = pl.BlockSpec((1, ts, d), lambda b, s: (b, s, 0))
    return pl.pallas_call(
        _xattn_router_kernel,
        grid=(bsz, nt),
        in_specs=[tok_spec, _const_spec(gx.shape), _const_spec(wq.shape),
                  pl.BlockSpec((1, d, m), lambda b, s: (b, 0, 0)),
                  pl.BlockSpec((1, m, d), lambda b, s: (b, 0, 0)),
                  _const_spec(wo.shape), _const_spec(gm.shape), _const_spec(wr.shape), _const_spec(br.shape),
                  _const_spec(upper.shape), _const_spec(lower.shape)],
        out_specs=[tok_spec,
                   pl.BlockSpec((LOCAL_ROWS, ROW_WORDS), lambda b, s: (b * nt + s, 0)),
                   pl.BlockSpec((ts, LANES), lambda b, s: (b * nt + s, 0)),
                   pl.BlockSpec((N_EXPERTS, LANES), lambda b, s: (b * nt + s, 0)),
                   pl.BlockSpec((N_EXPERTS, LANES), lambda b, s: (0, 0)),
                   pl.BlockSpec((zero_rows, ROW_WORDS), lambda b, s: (b * nt + s, 0))],
        out_shape=[jax.ShapeDtypeStruct((bsz, seq, d), F32),
                   jax.ShapeDtypeStruct((n_tiles * LOCAL_ROWS, ROW_WORDS), jnp.uint32),
                   jax.ShapeDtypeStruct((bsz * seq, LANES), F32),
                   jax.ShapeDtypeStruct((n_tiles * N_EXPERTS, LANES), F32),
                   jax.ShapeDtypeStruct((N_EXPERTS, LANES), F32),
                   jax.ShapeDtypeStruct((n_rows, ROW_WORDS), jnp.uint32)],
        scratch_shapes=[pltpu.VMEM((N_EXPERTS, LANES), F32)],
        compiler_params=pltpu.CompilerParams(
            dimension_semantics=("arbitrary", "arbitrary"), vmem_limit_bytes=VMEM_LIMIT_BYTES),
        name="xattn_router",
    )(h1, gx, wq, kt, v, wo, gm, wr, br, upper, lower)


TAB_COUNT, TAB_DST, TAB_LOCAL = (k * N_EXPERTS for k in range(3))


SEGMENT_RUN_LOG2 = 3
SEGMENT_CHUNKS = (4, 2, 1)


def _rows(first_group, n_groups):
    return pl.ds(pl.multiple_of(first_group * ROW_GROUP, ROW_GROUP), n_groups * ROW_GROUP)


def _for_each_segment_copy(tab_ref, make_copy, action):
    run = 1 << SEGMENT_RUN_LOG2

    def expert(e, c):
        n = tab_ref[0, 0, TAB_COUNT + e]
        dst0 = tab_ref[0, 0, TAB_DST + e]
        loc0 = tab_ref[0, 0, TAB_LOCAL + e]

        def full_run(k, c2):
            action(make_copy(loc0 + k * run, dst0 + k * run, run))
            return c2
        lax.fori_loop(0, lax.shift_right_logical(n, SEGMENT_RUN_LOG2), full_run, 0)
        for size in SEGMENT_CHUNKS:
            off = n & (-2 * size)

            @pl.when((n & size) != 0)
            def _():
                action(make_copy(loc0 + off, dst0 + off, size))
        return c
    lax.fori_loop(0, N_EXPERTS, expert, 0)


def _dispatch_kernel(tab_ref, xs_ref, buf_in_ref, buf_ref, sem):
    del buf_in_ref

    def make_copy(local, dst, n_groups):
        return pltpu.make_async_copy(xs_ref.at[_rows(local, n_groups)], buf_ref.at[_rows(dst, n_groups)], sem)

    _for_each_segment_copy(tab_ref, make_copy, lambda cp: cp.start())
    _for_each_segment_copy(tab_ref, make_copy, lambda cp: cp.wait())


def _dispatch(table, xs, buf0):
    n_tiles = table.shape[0]
    return pl.pallas_call(
        _dispatch_kernel,
        grid=(n_tiles,),
        in_specs=[pl.BlockSpec((1, 1, table.shape[2]), lambda i: (i, 0, 0), memory_space=pltpu.SMEM),
                  pl.BlockSpec((LOCAL_ROWS, xs.shape[1]), lambda i: (i, 0)),
                  pl.BlockSpec(memory_space=pl.ANY)],
        out_specs=pl.BlockSpec(memory_space=pl.ANY),
        out_shape=jax.ShapeDtypeStruct(buf0.shape, buf0.dtype),
        scratch_shapes=[pltpu.SemaphoreType.DMA],
        input_output_aliases={2: 0},
        compiler_params=pltpu.CompilerParams(
            dimension_semantics=("arbitrary",), vmem_limit_bytes=VMEM_LIMIT_BYTES),
        name="dispatch",
    )(table, xs, buf0)


def _expert_kernel(be_ref, nused_ref, xb_ref, wgu_ref, wd_ref, y_ref):
    del be_ref
    i = pl.program_id(0)
    used = i < nused_ref[0]

    @pl.when(used)
    def _():
        half = xb_ref.shape[1] - LANES
        x_lo, x_hi = _unpack_halves(xb_ref[:, :half])
        gu = _dot(x_lo, wgu_ref[0, :half, :]) + _dot(x_hi, wgu_ref[0, half:, :])
        mid = (_silu(gu[:, :D_EXPERT]) * gu[:, D_EXPERT:]).astype(BF16)
        comb = pltpu.bitcast(xb_ref[:, half:half + LANES], F32)[:, 0:1]
        y = _dot(mid, wd_ref[0]) * comb
        y_ref[...] = _pack_halves(y.astype(BF16).astype(F32))

    @pl.when(jnp.logical_not(used))
    def _():
        y_ref[...] = jnp.zeros(y_ref.shape, y_ref.dtype)


def _experts(block_expert, n_used, xbuf, wgu, wd):
    w = xbuf.shape[1]
    bm = EXPERT_BLOCK
    n_blocks = block_expert.shape[0]
    n_rows = n_blocks * bm
    d = wd.shape[2]
    grid_spec = pltpu.PrefetchScalarGridSpec(
        num_scalar_prefetch=2,
        grid=(n_blocks,),
        in_specs=[pl.BlockSpec((bm, w), lambda i, be, nu: (i, 0)),
                  pl.BlockSpec((1,) + wgu.shape[1:], lambda i, be, nu: (be[i], 0, 0)),
                  pl.BlockSpec((1,) + wd.shape[1:], lambda i, be, nu: (be[i], 0, 0))],
        out_specs=pl.BlockSpec((bm, d // 2), lambda i, be, nu: (i, 0)),
    )
    return pl.pallas_call(
        _expert_kernel,
        grid_spec=grid_spec,
        out_shape=jax.ShapeDtypeStruct((n_rows, d // 2), jnp.uint32),
        compiler_params=pltpu.CompilerParams(
            dimension_semantics=("arbitrary",), vmem_limit_bytes=VMEM_LIMIT_BYTES),
        name="experts",
    )(block_expert, n_used, xbuf, wgu, wd)


def _combine_kernel(tab_ref, tab_next_ref, h2_ref, rinfo_ref, gf_ref, ybuf_ref, out_ref, ys_ref, sem):
    tr = h2_ref.shape[0]
    n_local = ys_ref.shape[1]
    i = pl.program_id(0)
    slot = i % 2

    def copier(s):
        def make_copy(local, src, n_groups):
            return pltpu.make_async_copy(ybuf_ref.at[_rows(src, n_groups)],
                                         ys_ref.at[s, _rows(local, n_groups)], sem.at[s])
        return make_copy

    def gather(tab, s):
        covered = tab[0, 0, TAB_LOCAL + N_EXPERTS - 1] + tab[0, 0, TAB_COUNT + N_EXPERTS - 1]

        def zero(g, c):
            ys_ref[s, _rows(g, 1), :] = jnp.zeros((ROW_GROUP, ys_ref.shape[2]), ys_ref.dtype)
            return c
        lax.fori_loop(covered, n_local // ROW_GROUP, zero, 0)
        _for_each_segment_copy(tab, copier(s), lambda cp: cp.start())

    @pl.when(i == 0)
    def _():
        gather(tab_ref, 0)

    @pl.when(i + 1 < pl.num_programs(0))
    def _():
        gather(tab_next_ref, 1 - slot)

    _for_each_segment_copy(tab_ref, copier(slot), lambda cp: cp.wait())

    info = rinfo_ref[...]
    ciota = lax.broadcasted_iota(jnp.int32, (tr, n_local), 1)
    pick = jnp.where(ciota == info[:, 0:1].astype(jnp.int32), 1.0,
                     jnp.where(ciota == info[:, 1:2].astype(jnp.int32), 1.0, 0.0)).astype(BF16)
    y_lo, y_hi = _unpack_halves(ys_ref[slot])
    h = h2_ref[...] + jnp.concatenate([_dot(pick, y_lo), _dot(pick, y_hi)], axis=-1)
    out_ref[...] = _rms(h, gf_ref[...])


def _combine(table, h2, rinfo, gf, ybuf):
    t, d = h2.shape
    tr = SEQ_TILE
    n_tiles = t // tr
    tab_spec = (1, 1, table.shape[2])
    return pl.pallas_call(
        _combine_kernel,
        grid=(n_tiles,),
        in_specs=[pl.BlockSpec(tab_spec, lambda i: (i, 0, 0), memory_space=pltpu.SMEM),
                  pl.BlockSpec(tab_spec, lambda i: (jnp.minimum(i + 1, n_tiles - 1), 0, 0),
                               memory_space=pltpu.SMEM),
                  pl.BlockSpec((tr, d), lambda i: (i, 0)),
                  pl.BlockSpec((tr, LANES), lambda i: (i, 0)),
                  _const_spec(gf.shape),
                  pl.BlockSpec(memory_space=pl.ANY)],
        out_specs=pl.BlockSpec((tr, d), lambda i: (i, 0)),
        out_shape=jax.ShapeDtypeStruct((t, d), F32),
        scratch_shapes=[pltpu.VMEM((2, LOCAL_ROWS, d // 2), jnp.uint32), pltpu.SemaphoreType.DMA((2,))],
        compiler_params=pltpu.CompilerParams(
            dimension_semantics=("arbitrary",), vmem_limit_bytes=VMEM_LIMIT_BYTES),
        name="combine",
    )(table, table, h2, rinfo, gf, ybuf)


def _pad_lanes(v, fill=0.0):
    return jnp.pad(v, (0, LANES - v.shape[0]), constant_values=fill)[None, :]


def kernel(x, mem, norm_mem, norm_mix, w_in, conv_ssd_w, conv_ssd_b, dt_bias, a_log, d_skip, norm_ssd_gate,
           conv_short_w, w_out, norm_xattn, w_q, w_kv, w_o, norm_moe, w_router_group, b_router_group,
           w_router_expert, b_router_expert, w_gate, w_up, w_down, norm_final):
    bsz, seq, d = x.shape
    t = bsz * seq
    assert w_in.shape[0] == 1, "single-layer model"
    assert d == D_MODEL and seq % SEQ_TILE == 0 and t % ROW_TILE == 0

    tri = jnp.tril(jnp.ones((SSD_CHUNK, SSD_CHUNK), F32))
    upper = jnp.triu(jnp.ones((SEQ_TILE, SEQ_TILE), F32), k=1).astype(BF16)

    o2 = SSD_WIDTH + XBC_WIDTH
    o3 = o2 + SSD_HEADS
    wl = w_in[0]
    win = jnp.concatenate(
        [wl[:, :o2], wl[:, o3:], wl[:, o2:o3], jnp.zeros((d, LANES - SSD_HEADS), F32)], axis=1).astype(BF16)
    h1, wgu, wdb = _mixer(x, norm_mix, win, conv_ssd_w[0], conv_ssd_b, _pad_lanes(dt_bias[0]),
                _pad_lanes(-jnp.exp(a_log[0])), jnp.repeat(d_skip[0], SSD_HEAD_DIM)[None, :], norm_ssd_gate,
                conv_short_w[0], w_out[0].astype(BF16), tri, w_gate[0], w_up[0], w_down[0])

    kt, v = _memory_kv(mem, norm_mem[None, :], w_kv[0].astype(BF16))
    wr = jnp.zeros((d, LANES), F32)
    wr = wr.at[:, ROUTER_GROUP_ROW:ROUTER_GROUP_ROW + N_GROUPS_MOE].set(w_router_group[0])
    wr = wr.at[:, ROUTER_EXPERT_ROW:ROUTER_EXPERT_ROW + N_EXPERTS].set(w_router_expert[0])
    wr_hi = wr.astype(BF16)
    wr_lo = (wr - wr_hi.astype(F32)).astype(BF16)
    br = jnp.zeros((LANES,), F32)
    br = br.at[ROUTER_GROUP_ROW:ROUTER_GROUP_ROW + N_GROUPS_MOE].set(b_router_group[0])
    br = br.at[ROUTER_EXPERT_ROW:ROUTER_EXPERT_ROW + N_EXPERTS].set(b_router_expert[0])
    lower = jnp.tril(jnp.ones((N_EXPERTS, N_EXPERTS), F32), k=-1).astype(BF16)
    n_tiles = t // SEQ_TILE
    bgroups = EXPERT_BLOCK // ROW_GROUP
    max_groups = (2 * t + n_tiles * N_EXPERTS * (ROW_GROUP - 1)) // ROW_GROUP
    n_blocks = -(-max_groups // bgroups) + N_EXPERTS
    even = max(1, (n_tiles * SUBLANES) // math.gcd(n_tiles * SUBLANES, EXPERT_BLOCK))
    n_blocks = -(-n_blocks // even) * even
    h2, xs, rinfo, seg, counts, xbuf0 = _xattn_router(
        h1, norm_xattn, w_q[0].astype(BF16), kt, v, w_o[0].astype(BF16), norm_moe,
        jnp.concatenate([wr_hi, wr_lo], axis=1), br[None, :], upper, lower, n_blocks * EXPERT_BLOCK)

    seg = seg.reshape(n_tiles, N_EXPERTS, LANES)[:, :, 0:3].astype(jnp.int32)
    seg_groups, seg_local, seg_before = seg[:, :, 0], seg[:, :, 1], seg[:, :, 2]
    sizes = counts[:, 0].astype(jnp.int32)
    padded = ((sizes + bgroups - 1) // bgroups) * bgroups
    pad_end = jnp.cumsum(padded)
    pad_start = pad_end - padded
    seg_dst = pad_start[None, :] + seg_before
    table = jnp.concatenate([seg_groups, seg_dst, seg_local], axis=1)[:, None, :]
    block_group = jnp.arange(n_blocks, dtype=jnp.int32) * bgroups
    block_expert = jnp.minimum(jnp.sum((pad_end[None, :] <= block_group[:, None]).astype(jnp.int32), axis=-1),
                               N_EXPERTS - 1)
    n_used = (pad_end[-1:] // bgroups).astype(jnp.int32)

    xbuf = _dispatch(table, xs, xbuf0)
    ybuf = _experts(block_expert, n_used, xbuf, wgu, wdb)
    out = _combine(table, h2.reshape(t, d), rinfo, norm_final[None, :], ybuf)
    return out.reshape(bsz, seq, d)
```

```python
import math

import jax
import jax.numpy as jnp
from jax import lax
from jax.experimental import pallas as pl
from jax.experimental.pallas import tpu as pltpu

F32 = jnp.float32
BF16 = jnp.bfloat16
EPS = 1e-6

LANES = 128
SUBLANES = 8
VMEM_LIMIT_BYTES = 56 * 1024 * 1024

D_MODEL = 1024
SSD_WIDTH = 512
SSD_HEAD_DIM = 64
SSD_HEADS = 8
SSD_GROUPS = 2
SSD_HEADS_PER_GROUP = 4
SSD_STATE = 64
SSD_CHUNK = 128
XBC_WIDTH = SSD_WIDTH + 2 * SSD_GROUPS * SSD_STATE
SC_WIDTH = 512
XA_HEADS = 4
XA_HEAD_DIM = 256
N_GROUPS_MOE = 4
EXPERTS_PER_GROUP = 8
N_EXPERTS = 32
D_EXPERT = 512

COL_Z = 0
COL_XBC = COL_Z + SSD_WIDTH
COL_GB = COL_XBC + XBC_WIDTH
COL_GC = COL_GB + SC_WIDTH
COL_V = COL_GC + SC_WIDTH
COL_DT = COL_V + SC_WIDTH
PROJ_WIDTH = COL_DT + LANES

SEQ_TILE = 512
EXPERT_BLOCK = 512
HALO = SUBLANES
ROW_GROUP = SUBLANES
LOCAL_ROWS = 2 * SEQ_TILE + N_EXPERTS * ROW_GROUP
ROW_WORDS = D_MODEL // 2 + LANES
PERM_CHUNKS = 4

ROUTER_GROUP_ROW = 0
ROUTER_EXPERT_ROW = SUBLANES


def _rms(x, g):
    return x * lax.rsqrt(jnp.mean(x * x, axis=-1, keepdims=True) + EPS) * g


def _silu(x):
    return x * jax.nn.sigmoid(x)


def _dot(a, b):
    return jnp.dot(a, b, preferred_element_type=F32)


def _pack_halves(v):
    w = v.shape[1] // 2
    lo = lax.shift_right_logical(pltpu.bitcast(v[:, :w], jnp.uint32), jnp.uint32(16))
    hi = pltpu.bitcast(v[:, w:], jnp.uint32) & jnp.uint32(0xFFFF0000)
    return lo | hi


def _unpack_halves(words):
    lo = pltpu.bitcast(lax.shift_left(words, jnp.uint32(16)), F32).astype(BF16)
    hi = pltpu.bitcast(words & jnp.uint32(0xFFFF0000), F32).astype(BF16)
    return lo, hi


def _mixer_kernel(x_ref, gmix_ref, win_ref, cw_ref, cb_ref, dtb_ref, a_ref, dskip_ref, gate_ref,
                  scw_ref, wout_ref, tri_ref, wg_ref, wu_ref, wd_ref, h_ref, wgu_ref, wdb_ref,
                  conv_ref, cv_ref, act_ref, dt_ref, y_ref, state_ref, sinc_ref, loff_ref, elast_ref):
    ts = x_ref.shape[1]
    wgu_ref[0, :, :D_EXPERT] = wg_ref[0].astype(BF16)
    wgu_ref[0, :, D_EXPERT:] = wu_ref[0].astype(BF16)
    wdb_ref[0] = wd_ref[0].astype(BF16)
    s = pl.program_id(1)
    x = x_ref[0]
    nb = _rms(x, gmix_ref[...]).astype(BF16)

    @pl.when(s == 0)
    def _():
        conv_ref[0:HALO, :] = jnp.zeros((HALO, XBC_WIDTH), F32)
        cv_ref[0:HALO, :] = jnp.zeros((HALO, SC_WIDTH), F32)
        state_ref[...] = jnp.zeros(state_ref.shape, F32)

    @pl.when(s > 0)
    def _():
        conv_ref[0:HALO, :] = conv_ref[ts:ts + HALO, :]
        cv_ref[0:HALO, :] = cv_ref[ts:ts + HALO, :]

    xbc = _dot(nb, win_ref[:, COL_XBC:COL_XBC + XBC_WIDTH])
    conv_ref[HALO:HALO + ts, :] = xbc
    acc = cb_ref[...] + cw_ref[3:4, :] * xbc
    for i in range(3):
        acc = acc + cw_ref[i:i + 1, :] * conv_ref[HALO - 3 + i:HALO - 3 + i + ts, :]
    act_ref[...] = _silu(acc)

    dtr = _dot(nb, win_ref[:, COL_DT:COL_DT + LANES]) + dtb_ref[...]
    dt_ref[...] = jnp.maximum(dtr, 0.0) + jnp.log1p(jnp.exp(-jnp.abs(dtr)))

    row = lax.broadcasted_iota(jnp.int32, (SSD_CHUNK, SSD_CHUNK), 0)
    col = lax.broadcasted_iota(jnp.int32, (SSD_CHUNK, SSD_CHUNK), 1)
    causal = row >= col
    lo_q = col < SSD_HEAD_DIM
    lo_n = lax.broadcasted_iota(jnp.int32, (SSD_STATE, SSD_CHUNK), 1) < SSD_HEAD_DIM
    lo_e = lax.broadcasted_iota(jnp.int32, (SUBLANES, SSD_CHUNK), 1) < SSD_HEAD_DIM

    def chunk_prologue(c):
        rows = slice(c * SSD_CHUNK, (c + 1) * SSD_CHUNK)
        dt = dt_ref[rows, :]
        adt = dt * a_ref[...]
        cs = jnp.dot(tri_ref[...], adt, preferred_element_type=F32,
                     precision=lax.Precision.HIGHEST)
        bm = act_ref[rows, SSD_WIDTH:SSD_WIDTH + SSD_GROUPS * SSD_STATE]
        cm = act_ref[rows, SSD_WIDTH + SSD_GROUPS * SSD_STATE:XBC_WIDTH]
        cbs = []
        for g in range(SSD_GROUPS):
            cg = cm[:, g * SSD_STATE:(g + 1) * SSD_STATE]
            bg = bm[:, g * SSD_STATE:(g + 1) * SSD_STATE]
            cbs.append(lax.dot_general(cg.astype(BF16), bg.astype(BF16), (((1,), (1,)), ((), ())),
                                       preferred_element_type=F32))
        return cs, cs.T, dt.T, bm.T, cm, pltpu.roll(cm, SSD_STATE, axis=1), cbs

    def chunk_body(c, prologue):
        rows = slice(c * SSD_CHUNK, (c + 1) * SSD_CHUNK)
        cs, cs_t, dt_t, bm_t, cm, cm_sw, cbs = prologue
        for g in range(SSD_GROUPS):
            bg_t = bm_t[g * SSD_STATE:(g + 1) * SSD_STATE, :]
            cb = cbs[g]
            c2 = jnp.where(lo_q, cm, cm_sw) if g == 0 else jnp.where(lo_q, cm_sw, cm)
            for jj in range(SSD_HEADS_PER_GROUP // 2):
                j = g * (SSD_HEADS_PER_GROUP // 2) + jj
                idx = c * (SSD_HEADS // 2) + j
                xp = act_ref[rows, j * LANES:(j + 1) * LANES]
                rhs = jnp.concatenate(
                    [jnp.where(lo_q, xp, 0.0), jnp.where(lo_q, 0.0, xp)], axis=0).astype(BF16)
                diag, off, w_rows, e_last = [], [], [], []
                for h in (2 * j, 2 * j + 1):
                    cs_col = jnp.broadcast_to(cs[:, h:h + 1], (SSD_CHUNK, SSD_CHUNK))
                    cs_row = cs_t[h:h + 1, :]
                    dt_row = dt_t[h:h + 1, :]
                    decay = jnp.exp(jnp.where(causal, cs_col - cs_row, -jnp.inf))
                    diag.append((cb * dt_row) * decay)
                    off.append(jnp.exp(cs_col))
                    cl = cs_t[h:h + 1, SSD_CHUNK - 1:SSD_CHUNK]
                    w_rows.append(jnp.exp(cl - cs_row) * dt_row)
                    e_last.append(jnp.exp(cl))
                y_ref[rows, j * LANES:(j + 1) * LANES] = _dot(jnp.concatenate(diag, axis=1).astype(BF16), rhs)
                lhs2 = jnp.concatenate([bg_t * w_rows[0], bg_t * w_rows[1]], axis=1).astype(BF16)
                sinc_ref[idx] = _dot(lhs2, rhs)
                loff_ref[idx] = (c2 * jnp.where(lo_q, off[0], off[1])).astype(BF16)
                elast_ref[idx] = jnp.where(lo_e, e_last[0], e_last[1])

    n_chunks = ts // SSD_CHUNK
    prologues = [chunk_prologue(c) for c in range(n_chunks)]

    g_c = _dot(nb, win_ref[:, COL_GC:COL_GC + SC_WIDTH])
    v = _dot(nb, win_ref[:, COL_V:COL_V + SC_WIDTH])
    u = g_c * v
    cv_ref[HALO:HALO + ts, :] = u
    conv3 = scw_ref[2:3, :] * u
    for i in range(2):
        conv3 = conv3 + scw_ref[i:i + 1, :] * cv_ref[HALO - 2 + i:HALO - 2 + i + ts, :]
    g_b = _dot(nb, win_ref[:, COL_GB:COL_GB + SC_WIDTH])
    out_sc = _dot((g_b * conv3).astype(BF16), wout_ref[SSD_WIDTH:, :])
    gate_z = _silu(_dot(nb, win_ref[:, COL_Z:COL_Z + SSD_WIDTH]))

    for c in range(n_chunks):
        chunk_body(c, prologues[c])

    for c in range(n_chunks):
        for j in range(SSD_HEADS // 2):
            idx = c * (SSD_HEADS // 2) + j
            stp = state_ref[j]
            rhs_st = jnp.concatenate(
                [jnp.where(lo_n, stp, 0.0), jnp.where(lo_n, 0.0, stp)], axis=0).astype(BF16)
            y_ref[c * SSD_CHUNK:(c + 1) * SSD_CHUNK, j * LANES:(j + 1) * LANES] += _dot(loff_ref[idx], rhs_st)
            state_ref[j] = elast_ref[idx][0:1, :] * stp + sinc_ref[idx]

    y = y_ref[...] + act_ref[:, 0:SSD_WIDTH] * dskip_ref[...]
    gated = y * gate_z
    gw = SSD_WIDTH // SSD_GROUPS
    y_ssd = jnp.concatenate(
        [_rms(gated[:, g * gw:(g + 1) * gw], gate_ref[:, g * gw:(g + 1) * gw]) for g in range(SSD_GROUPS)],
        axis=-1).astype(BF16)
    h_ref[0] = x + out_sc + _dot(y_ssd, wout_ref[0:SSD_WIDTH, :])


def _const_spec(shape):
    nd = len(shape)
    return pl.BlockSpec(shape, lambda *_: (0,) * nd)


def _mixer(x, gmix, win, cw, cb, dtb, a_row, dskip, gate, scw, wout, tri, w_gate, w_up, w_down):
    bsz, seq, d = x.shape
    ts = SEQ_TILE
    nt = seq // ts
    consts = (gmix, win, cw, cb, dtb, a_row, dskip, gate, scw, wout, tri)
    n_exp, _, de = w_gate.shape
    per_expert = (bsz * nt) // n_exp
    assert per_expert * n_exp == bsz * nt and d % per_expert == 0 and de % per_expert == 0
    rg, rd = d // per_expert, de // per_expert

    def slice_map(b, s):
        i = b * nt + s
        return (i // per_expert, i % per_expert, 0)
    return pl.pallas_call(
        _mixer_kernel,
        grid=(bsz, nt),
        in_specs=([pl.BlockSpec((1, ts, d), lambda b, s: (b, s, 0))] + [_const_spec(c.shape) for c in consts]
                  + [pl.BlockSpec((1, rg, de), slice_map), pl.BlockSpec((1, rg, de), slice_map),
                     pl.BlockSpec((1, rd, d), slice_map)]),
        out_specs=[pl.BlockSpec((1, ts, d), lambda b, s: (b, s, 0)),
                   pl.BlockSpec((1, rg, 2 * de), slice_map), pl.BlockSpec((1, rd, d), slice_map)],
        out_shape=[jax.ShapeDtypeStruct((bsz, seq, d), F32),
                   jax.ShapeDtypeStruct((n_exp, d, 2 * de), BF16), jax.ShapeDtypeStruct((n_exp, de, d), BF16)],
        scratch_shapes=[
            pltpu.VMEM((ts + HALO, XBC_WIDTH), F32),
            pltpu.VMEM((ts + HALO, SC_WIDTH), F32),
            pltpu.VMEM((ts, XBC_WIDTH), F32),
            pltpu.VMEM((ts, LANES), F32),
            pltpu.VMEM((ts, SSD_WIDTH), F32),
            pltpu.VMEM((SSD_HEADS // 2, SSD_STATE, 2 * SSD_HEAD_DIM), F32),
            pltpu.VMEM((ts // SSD_CHUNK * SSD_HEADS // 2, SSD_STATE, 2 * SSD_HEAD_DIM), F32),
            pltpu.VMEM((ts // SSD_CHUNK * SSD_HEADS // 2, SSD_CHUNK, 2 * SSD_STATE), BF16),
            pltpu.VMEM((ts // SSD_CHUNK * SSD_HEADS // 2, SUBLANES, 2 * SSD_HEAD_DIM), F32),
        ],
        compiler_params=pltpu.CompilerParams(
            dimension_semantics=("arbitrary", "arbitrary"), vmem_limit_bytes=VMEM_LIMIT_BYTES),
        name="mixer",
    )(x, *consts, w_gate, w_up, w_down)


def _kv_kernel(mem_ref, g_ref, wkv_ref, kt_ref, v_ref):
    d = mem_ref.shape[2]
    mn = _rms(mem_ref[0], g_ref[...]).astype(BF16)
    kv = _dot(mn, wkv_ref[...])
    kt_ref[0] = (kv[:, :d] * (XA_HEAD_DIM ** -0.5)).T.astype(BF16)
    v_ref[0] = kv[:, d:].astype(BF16)


def _memory_kv(mem, g, wkv):
    bsz, m, d = mem.shape
    return pl.pallas_call(
        _kv_kernel,
        grid=(bsz,),
        in_specs=[pl.BlockSpec((1, m, d), lambda b: (b, 0, 0)), _const_spec(g.shape), _const_spec(wkv.shape)],
        out_specs=[pl.BlockSpec((1, d, m), lambda b: (b, 0, 0)), pl.BlockSpec((1, m, d), lambda b: (b, 0, 0))],
        out_shape=[jax.ShapeDtypeStruct((bsz, d, m), BF16), jax.ShapeDtypeStruct((bsz, m, d), BF16)],
        compiler_params=pltpu.CompilerParams(
            dimension_semantics=("arbitrary",), vmem_limit_bytes=VMEM_LIMIT_BYTES),
        name="memory_kv",
    )(mem, g, wkv)


def _xattn_router_kernel(h1_ref, gx_ref, wq_ref, kt_ref, v_ref, wo_ref, gm_ref, wr_ref, br_ref, upper_ref,
                         lower_ref, h2_ref, xs_ref, rinfo_ref, seg_ref, counts_ref, zero_ref, carry_ref):
    ts = h1_ref.shape[1]
    first = jnp.logical_and(pl.program_id(0) == 0, pl.program_id(1) == 0)

    @pl.when(first)
    def _():
        carry_ref[...] = jnp.zeros(carry_ref.shape, F32)

    h1 = h1_ref[0]
    q = _dot(_rms(h1, gx_ref[...]).astype(BF16), wq_ref[...]).astype(BF16)
    outs = []
    for h in range(XA_HEADS):
        hs = slice(h * XA_HEAD_DIM, (h + 1) * XA_HEAD_DIM)
        sc = _dot(q[:, hs], kt_ref[0, hs, :])
        p = jnp.exp(sc - jnp.max(sc, axis=-1, keepdims=True))
        o = _dot(p.astype(BF16), v_ref[0, :, hs]) / jnp.sum(p, axis=-1, keepdims=True)
        outs.append(o.astype(BF16))
    h2 = h1 + _dot(jnp.concatenate(outs, axis=-1), wo_ref[...])
    h2_ref[0] = h2

    n3 = _rms(h2, gm_ref[...])
    n3b = n3.astype(BF16)
    half = n3.shape[1] // 2
    n3r = n3b.astype(F32)
    zero_ref[...] = jnp.zeros(zero_ref.shape, zero_ref.dtype)

    n3_lo = (n3 - n3r).astype(BF16)
    lg2 = _dot(n3b, wr_ref[...])
    logits = lg2[:, :LANES] + lg2[:, LANES:] + _dot(n3_lo, wr_ref[:, :LANES]) + br_ref[...]
    lt = logits.T

    l0, l1, l2, l3 = (lt[ROUTER_GROUP_ROW + i:ROUTER_GROUP_ROW + i + 1, :] for i in range(N_GROUPS_MOE))
    m = jnp.maximum(jnp.maximum(l0, l1), jnp.maximum(l2, l3))
    gidx = jnp.where(l0 == m, 0, jnp.where(l1 == m, 1, jnp.where(l2 == m, 2, 3)))
    g_w = 1.0 / (jnp.exp(l0 - m) + jnp.exp(l1 - m) + jnp.exp(l2 - m) + jnp.exp(l3 - m))

    def group_rows(g):
        r = ROUTER_EXPERT_ROW + g * EXPERTS_PER_GROUP
        return lt[r:r + EXPERTS_PER_GROUP, :]

    el = jnp.where(gidx == 0, group_rows(0),
                   jnp.where(gidx == 1, group_rows(1), jnp.where(gidx == 2, group_rows(2), group_rows(3))))
    sub = lax.broadcasted_iota(jnp.int32, el.shape, 0)
    m1 = jnp.max(el, axis=0, keepdims=True)
    i1 = jnp.min(jnp.where(el == m1, sub, EXPERTS_PER_GROUP), axis=0, keepdims=True)
    el2 = jnp.where(sub == i1, -jnp.inf, el)
    m2 = jnp.max(el2, axis=0, keepdims=True)
    i2 = jnp.min(jnp.where(el2 == m2, sub, EXPERTS_PER_GROUP), axis=0, keepdims=True)
    e2 = jnp.exp(m2 - m1)
    w1 = 1.0 / (1.0 + e2)
    w2 = e2 / (1.0 + e2)
    e_a = gidx * EXPERTS_PER_GROUP + i1
    e_b = gidx * EXPERTS_PER_GROUP + i2

    eio = lax.broadcasted_iota(jnp.int32, (N_EXPERTS, ts), 0)
    oh_a = (eio == e_a).astype(F32)
    oh_b = (eio == e_b).astype(F32)
    prefix = _dot(jnp.concatenate([oh_a, oh_b], axis=0).astype(BF16), upper_ref[...])
    tot_a = jnp.sum(oh_a, axis=1, keepdims=True)
    tot_b = jnp.sum(oh_b, axis=1, keepdims=True)
    groups = jnp.broadcast_to(jnp.floor((tot_a + tot_b + (ROW_GROUP - 1)) * (1.0 / ROW_GROUP)),
                              (N_EXPERTS, LANES))
    first_group = _dot(lower_ref[...], groups.astype(BF16))
    base = ROW_GROUP * first_group[:, 0:1]
    pos_a = jnp.sum(oh_a * (base + prefix[:N_EXPERTS]), axis=0, keepdims=True)
    pos_b = jnp.sum(oh_b * (base + tot_a + prefix[N_EXPERTS:]), axis=0, keepdims=True)

    carry = carry_ref[...]
    lane = lax.broadcasted_iota(jnp.int32, (N_EXPERTS, LANES), 1)
    seg_ref[...] = jnp.where(lane == 0, groups, jnp.where(lane == 1, first_group,
                                                          jnp.where(lane == 2, carry, 0.0)))
    carry_ref[...] = carry + groups
    counts_ref[...] = carry + groups

    n_local = xs_ref.shape[0]
    chunk = n_local // PERM_CHUNKS
    ipos_a, ipos_b = pos_a.astype(jnp.int32), pos_b.astype(jnp.int32)
    c_a, c_b = g_w * w1, g_w * w2
    for k in range(PERM_CHUNKS):
        rows = slice(k * chunk, (k + 1) * chunk)
        riota = lax.broadcasted_iota(jnp.int32, (chunk, ts), 0) + k * chunk
        sel_a = riota == ipos_a
        sel_b = riota == ipos_b
        perm = jnp.where(sel_a, 1.0, jnp.where(sel_b, 1.0, 0.0)).astype(BF16)
        xs_ref[rows, :half] = _pack_halves(_dot(perm, n3b))
        w_sorted = jnp.sum(jnp.where(sel_a, c_a, jnp.where(sel_b, c_b, 0.0)),
                           axis=1, keepdims=True)
        xs_ref[rows, half:] = pltpu.bitcast(jnp.broadcast_to(w_sorted, (chunk, LANES)), jnp.uint32)

    res = jnp.concatenate([pos_a, pos_b, jnp.zeros((LANES - 2, ts), F32)], axis=0)
    rinfo_ref[...] = res.T


def _xattn_router(h1, gx, wq, kt, v, wo, gm, wr, br, upper, lower, n_rows):
    bsz, seq, d = h1.shape
    ts = SEQ_TILE
    m = v.shape[1]
    nt = seq // ts
    n_tiles = bsz * nt
    zero_rows = n_rows // n_tiles
    assert zero_rows * n_tiles == n_rows and zero_rows % SUBLANES == 0
    tok_spec = pl.BlockSpec((1, ts, d), lambda b, s: (b, s, 0))
    return pl.pallas_call(
        _xattn_router_kernel,
        grid=(bsz, nt),
        in_specs=[tok_spec, _const_spec(gx.shape), _const_spec(wq.shape),
                  pl.BlockSpec((1, d, m), lambda b, s: (b, 0, 0)),
                  pl.BlockSpec((1, m, d), lambda b, s: (b, 0, 0)),
                  _const_spec(wo.shape), _const_spec(gm.shape), _const_spec(wr.shape), _const_spec(br.shape),
                  _const_spec(upper.shape), _const_spec(lower.shape)],
        out_specs=[tok_spec,
                   pl.BlockSpec((LOCAL_ROWS, ROW_WORDS), lambda b, s: (b * nt + s, 0)),
                   pl.BlockSpec((ts, LANES), lambda b, s: (b * nt + s, 0)),
                   pl.BlockSpec((N_EXPERTS, LANES), lambda b, s: (b * nt + s, 0)),
                   pl.BlockSpec((N_EXPERTS, LANES), lambda b, s: (0, 0)),
                   pl.BlockSpec((zero_rows, ROW_WORDS), lambda b, s: (b * nt + s, 0))],
        out_shape=[jax.ShapeDtypeStruct((bsz, seq, d), F32),
                   jax.ShapeDtypeStruct((n_tiles * LOCAL_ROWS, ROW_WORDS), jnp.uint32),
                   jax.ShapeDtypeStruct((bsz * seq, LANES), F32),
                   jax.ShapeDtypeStruct((n_tiles * N_EXPERTS, LANES), F32),
                   jax.ShapeDtypeStruct((N_EXPERTS, LANES), F32),
                   jax.ShapeDtypeStruct((n_rows, ROW_WORDS), jnp.uint32)],
        scratch_shapes=[pltpu.VMEM((N_EXPERTS, LANES), F32)],
        compiler_params=pltpu.CompilerParams(
            dimension_semantics=("arbitrary", "arbitrary"), vmem_limit_bytes=VMEM_LIMIT_BYTES),
        name="xattn_router",
    )(h1, gx, wq, kt, v, wo, gm, wr, br, upper, lower)


TAB_COUNT, TAB_DST, TAB_LOCAL = (k * N_EXPERTS for k in range(3))


SEGMENT_RUN_LOG2 = 3
SEGMENT_CHUNKS = (4, 2, 1)


def _rows(first_group, n_groups):
    return pl.ds(pl.multiple_of(first_group * ROW_GROUP, ROW_GROUP), n_groups * ROW_GROUP)


def _for_each_segment_copy(tab_ref, make_copy, action):
    run = 1 << SEGMENT_RUN_LOG2

    def expert(e, c):
        n = tab_ref[0, 0, TAB_COUNT + e]
        dst0 = tab_ref[0, 0, TAB_DST + e]
        loc0 = tab_ref[0, 0, TAB_LOCAL + e]

        def full_run(k, c2):
            action(make_copy(loc0 + k * run, dst0 + k * run, run))
            return c2
        lax.fori_loop(0, lax.shift_right_logical(n, SEGMENT_RUN_LOG2), full_run, 0)
        for size in SEGMENT_CHUNKS:
            off = n & (-2 * size)

            @pl.when((n & size) != 0)
            def _():
                action(make_copy(loc0 + off, dst0 + off, size))
        return c
    lax.fori_loop(0, N_EXPERTS, expert, 0)


def _dispatch_kernel(tab_ref, xs_ref, buf_in_ref, buf_ref, sem):
    del buf_in_ref

    def make_copy(local, dst, n_groups):
        return pltpu.make_async_copy(xs_ref.at[_rows(local, n_groups)], buf_ref.at[_rows(dst, n_groups)], sem)

    _for_each_segment_copy(tab_ref, make_copy, lambda cp: cp.start())
    _for_each_segment_copy(tab_ref, make_copy, lambda cp: cp.wait())


def _dispatch(table, xs, buf0):
    n_tiles = table.shape[0]
    return pl.pallas_call(
        _dispatch_kernel,
        grid=(n_tiles,),
        in_specs=[pl.BlockSpec((1, 1, table.shape[2]), lambda i: (i, 0, 0), memory_space=pltpu.SMEM),
                  pl.BlockSpec((LOCAL_ROWS, xs.shape[1]), lambda i: (i, 0)),
                  pl.BlockSpec(memory_space=pl.ANY)],
        out_specs=pl.BlockSpec(memory_space=pl.ANY),
        out_shape=jax.ShapeDtypeStruct(buf0.shape, buf0.dtype),
        scratch_shapes=[pltpu.SemaphoreType.DMA],
        input_output_aliases={2: 0},
        compiler_params=pltpu.CompilerParams(
            dimension_semantics=("arbitrary",), vmem_limit_bytes=VMEM_LIMIT_BYTES),
        name="dispatch",
    )(table, xs, buf0)


def _expert_kernel(be_ref, nused_ref, xb_ref, wgu_ref, wd_ref, y_ref):
    del be_ref
    i = pl.program_id(0)
    used = i < nused_ref[0]

    @pl.when(used)
    def _():
        half = xb_ref.shape[1] - LANES
        x_lo, x_hi = _unpack_halves(xb_ref[:, :half])
        gu = _dot(x_lo, wgu_ref[0, :half, :]) + _dot(x_hi, wgu_ref[0, half:, :])
        mid = (_silu(gu[:, :D_EXPERT]) * gu[:, D_EXPERT:]).astype(BF16)
        comb = pltpu.bitcast(xb_ref[:, half:half + LANES], F32)[:, 0:1]
        y = _dot(mid, wd_ref[0]) * comb
        y_ref[...] = _pack_halves(y.astype(BF16).astype(F32))

    @pl.when(jnp.logical_not(used))
    def _():
        y_ref[...] = jnp.zeros(y_ref.shape, y_ref.dtype)


def _experts(block_expert, n_used, xbuf, wgu, wd):
    w = xbuf.shape[1]
    bm = EXPERT_BLOCK
    n_blocks = block_expert.shape[0]
    n_rows = n_blocks * bm
    d = wd.shape[2]
    grid_spec = pltpu.PrefetchScalarGridSpec(
        num_scalar_prefetch=2,
        grid=(n_blocks,),
        in_specs=[pl.BlockSpec((bm, w), lambda i, be, nu: (i, 0)),
                  pl.BlockSpec((1,) + wgu.shape[1:], lambda i, be, nu: (be[i], 0, 0)),
                  pl.BlockSpec((1,) + wd.shape[1:], lambda i, be, nu: (be[i], 0, 0))],
        out_specs=pl.BlockSpec((bm, d // 2), lambda i, be, nu: (i, 0)),
    )
    return pl.pallas_call(
        _expert_kernel,
        grid_spec=grid_spec,
        out_shape=jax.ShapeDtypeStruct((n_rows, d // 2), jnp.uint32),
        compiler_params=pltpu.CompilerParams(
            dimension_semantics=("arbitrary",), vmem_limit_bytes=VMEM_LIMIT_BYTES),
        name="experts",
    )(block_expert, n_used, xbuf, wgu, wd)


def _combine_kernel(tab_ref, tab_next_ref, h2_ref, rinfo_ref, gf_ref, ybuf_ref, out_ref, ys_ref, sem):
    tr = h2_ref.shape[0]
    n_local = ys_ref.shape[1]
    i = pl.program_id(0)
    slot = i % 2

    def copier(s):
        def make_copy(local, src, n_groups):
            return pltpu.make_async_copy(ybuf_ref.at[_rows(src, n_groups)],
                                         ys_ref.at[s, _rows(local, n_groups)], sem.at[s])
        return make_copy

    def gather(tab, s):
        covered = tab[0, 0, TAB_LOCAL + N_EXPERTS - 1] + tab[0, 0, TAB_COUNT + N_EXPERTS - 1]

        def zero(g, c):
            ys_ref[s, _rows(g, 1), :] = jnp.zeros((ROW_GROUP, ys_ref.shape[2]), ys_ref.dtype)
            return c
        lax.fori_loop(covered, n_local // ROW_GROUP, zero, 0)
        _for_each_segment_copy(tab, copier(s), lambda cp: cp.start())

    @pl.when(i == 0)
    def _():
        gather(tab_ref, 0)

    @pl.when(i + 1 < pl.num_programs(0))
    def _():
        gather(tab_next_ref, 1 - slot)

    _for_each_segment_copy(tab_ref, copier(slot), lambda cp: cp.wait())

    info = rinfo_ref[...]
    ciota = lax.broadcasted_iota(jnp.int32, (tr, n_local), 1)
    pick = jnp.where(ciota == info[:, 0:1].astype(jnp.int32), 1.0,
                     jnp.where(ciota == info[:, 1:2].astype(jnp.int32), 1.0, 0.0)).astype(BF16)
    y_lo, y_hi = _unpack_halves(ys_ref[slot])
    h = h2_ref[...] + jnp.concatenate([_dot(pick, y_lo), _dot(pick, y_hi)], axis=-1)
    out_ref[...] = _rms(h, gf_ref[...])


def _combine(table, h2, rinfo, gf, ybuf):
    t, d = h2.shape
    tr = SEQ_TILE
    n_tiles = t // tr
    tab_spec = (1, 1, table.shape[2])
    return pl.pallas_call(
        _combine_kernel,
        grid=(n_tiles,),
        in_specs=[pl.BlockSpec(tab_spec, lambda i: (i, 0, 0), memory_space=pltpu.SMEM),
                  pl.BlockSpec(tab_spec, lambda i: (jnp.minimum(i + 1, n_tiles - 1), 0, 0),
                               memory_space=pltpu.SMEM),
                  pl.BlockSpec((tr, d), lambda i: (i, 0)),
                  pl.BlockSpec((tr, LANES), lambda i: (i, 0)),
                  _const_spec(gf.shape),
                  pl.BlockSpec(memory_space=pl.ANY)],
        out_specs=pl.BlockSpec((tr, d), lambda i: (i, 0)),
        out_shape=jax.ShapeDtypeStruct((t, d), F32),
        scratch_shapes=[pltpu.VMEM((2, LOCAL_ROWS, d // 2), jnp.uint32), pltpu.SemaphoreType.DMA((2,))],
        compiler_params=pltpu.CompilerParams(
            dimension_semantics=("arbitrary",), vmem_limit_bytes=VMEM_LIMIT_BYTES),
        name="combine",
    )(table, table, h2, rinfo, gf, ybuf)


def _pad_lanes(v):
    return jnp.pad(v, (0, LANES - v.shape[0]))[None, :]


def kernel(x, mem, norm_mem, norm_mix, w_in, conv_ssd_w, conv_ssd_b, dt_bias, a_log, d_skip, norm_ssd_gate,
           conv_short_w, w_out, norm_xattn, w_q, w_kv, w_o, norm_moe, w_router_group, b_router_group,
           w_router_expert, b_router_expert, w_gate, w_up, w_down, norm_final):
    bsz, seq, d = x.shape
    t = bsz * seq
    assert w_in.shape[0] == 1, "single-layer model"
    assert d == D_MODEL and seq % SEQ_TILE == 0

    tri = jnp.tril(jnp.ones((SSD_CHUNK, SSD_CHUNK), F32))
    upper = jnp.triu(jnp.ones((SEQ_TILE, SEQ_TILE), F32), k=1).astype(BF16)

    o2 = SSD_WIDTH + XBC_WIDTH
    o3 = o2 + SSD_HEADS
    wl = w_in[0]
    win = jnp.concatenate(
        [wl[:, :o2], wl[:, o3:], wl[:, o2:o3], jnp.zeros((d, LANES - SSD_HEADS), F32)], axis=1).astype(BF16)
    h1, wgu, wdb = _mixer(x, norm_mix, win, conv_ssd_w[0], conv_ssd_b, _pad_lanes(dt_bias[0]),
                _pad_lanes(-jnp.exp(a_log[0])), jnp.repeat(d_skip[0], SSD_HEAD_DIM)[None, :], norm_ssd_gate,
                conv_short_w[0], w_out[0].astype(BF16), tri, w_gate[0], w_up[0], w_down[0])

    kt, v = _memory_kv(mem, norm_mem[None, :], w_kv[0].astype(BF16))
    wr = jnp.zeros((d, LANES), F32)
    wr = wr.at[:, ROUTER_GROUP_ROW:ROUTER_GROUP_ROW + N_GROUPS_MOE].set(w_router_group[0])
    wr = wr.at[:, ROUTER_EXPERT_ROW:ROUTER_EXPERT_ROW + N_EXPERTS].set(w_router_expert[0])
    wr_hi = wr.astype(BF16)
    wr_lo = (wr - wr_hi.astype(F32)).astype(BF16)
    br = jnp.zeros((LANES,), F32)
    br = br.at[ROUTER_GROUP_ROW:ROUTER_GROUP_ROW + N_GROUPS_MOE].set(b_router_group[0])
    br = br.at[ROUTER_EXPERT_ROW:ROUTER_EXPERT_ROW + N_EXPERTS].set(b_router_expert[0])
    lower = jnp.tril(jnp.ones((N_EXPERTS, N_EXPERTS), F32), k=-1).astype(BF16)
    n_tiles = t // SEQ_TILE
    bgroups = EXPERT_BLOCK // ROW_GROUP
    max_groups = (2 * t + n_tiles * N_EXPERTS * (ROW_GROUP - 1)) // ROW_GROUP
    n_blocks = -(-max_groups // bgroups) + N_EXPERTS
    even = max(1, (n_tiles * SUBLANES) // math.gcd(n_tiles * SUBLANES, EXPERT_BLOCK))
    n_blocks = -(-n_blocks // even) * even
    h2, xs, rinfo, seg, counts, xbuf0 = _xattn_router(
        h1, norm_xattn, w_q[0].astype(BF16), kt, v, w_o[0].astype(BF16), norm_moe,
        jnp.concatenate([wr_hi, wr_lo], axis=1), br[None, :], upper, lower, n_blocks * EXPERT_BLOCK)

    seg = seg.reshape(n_tiles, N_EXPERTS, LANES)[:, :, 0:3].astype(jnp.int32)
    seg_groups, seg_local, seg_before = seg[:, :, 0], seg[:, :, 1], seg[:, :, 2]
    sizes = counts[:, 0].astype(jnp.int32)
    padded = ((sizes + bgroups - 1) // bgroups) * bgroups
    pad_end = jnp.cumsum(padded)
    pad_start = pad_end - padded
    seg_dst = pad_start[None, :] + seg_before
    table = jnp.concatenate([seg_groups, seg_dst, seg_local], axis=1)[:, None, :]
    block_group = jnp.arange(n_blocks, dtype=jnp.int32) * bgroups
    block_expert = jnp.minimum(jnp.sum((pad_end[None, :] <= block_group[:, None]).astype(jnp.int32), axis=-1),
                               N_EXPERTS - 1)
    n_used = (pad_end[-1:] // bgroups).astype(jnp.int32)

    xbuf = _dispatch(table, xs, xbuf0)
    ybuf = _experts(block_expert, n_used, xbuf, wgu, wdb)
    out = _combine(table, h2.reshape(t, d), rinfo, norm_final[None, :], ybuf)
    return out.reshape(bsz, seq, d)
```

```python
import math

import jax
import jax.numpy as jnp
from jax import lax
from jax.experimental import pallas as pl
from jax.experimental.pallas import tpu as pltpu

F32 = jnp.float32
BF16 = jnp.bfloat16
EPS = 1e-6

LANES = 128
SUBLANES = 8
VMEM_LIMIT_BYTES = 56 * 1024 * 1024

D_MODEL = 1024
SSD_WIDTH = 512
SSD_HEAD_DIM = 64
SSD_HEADS = 8
SSD_GROUPS = 2
SSD_HEADS_PER_GROUP = 4
SSD_STATE = 64
SSD_CHUNK = 128
XBC_WIDTH = SSD_WIDTH + 2 * SSD_GROUPS * SSD_STATE
SC_WIDTH = 512
XA_HEADS = 4
XA_HEAD_DIM = 256
N_GROUPS_MOE = 4
EXPERTS_PER_GROUP = 8
N_EXPERTS = 32
D_EXPERT = 512

COL_Z = 0
COL_XBC = COL_Z + SSD_WIDTH
COL_GB = COL_XBC + XBC_WIDTH
COL_GC = COL_GB + SC_WIDTH
COL_V = COL_GC + SC_WIDTH
COL_DT = COL_V + SC_WIDTH
PROJ_WIDTH = COL_DT + LANES

SEQ_TILE = 512
EXPERT_BLOCK = 512
BLOCKS_PER_STEP = 2
HALO = SUBLANES
ROW_GROUP = SUBLANES
LOCAL_ROWS = 2 * SEQ_TILE + N_EXPERTS * ROW_GROUP
ROW_WORDS = D_MODEL // 2 + LANES
PERM_CHUNKS = 4

ROUTER_GROUP_ROW = 0
ROUTER_EXPERT_ROW = SUBLANES


def _rms(x, g):
    return x * lax.rsqrt(jnp.mean(x * x, axis=-1, keepdims=True) + EPS) * g


def _silu(x):
    return x * jax.nn.sigmoid(x)


def _dot(a, b):
    return jnp.dot(a, b, preferred_element_type=F32)


def _pack_halves(v):
    w = v.shape[1] // 2
    lo = lax.shift_right_logical(pltpu.bitcast(v[:, :w], jnp.uint32), jnp.uint32(16))
    hi = pltpu.bitcast(v[:, w:], jnp.uint32) & jnp.uint32(0xFFFF0000)
    return lo | hi


def _unpack_halves(words):
    lo = pltpu.bitcast(lax.shift_left(words, jnp.uint32(16)), F32).astype(BF16)
    hi = pltpu.bitcast(words & jnp.uint32(0xFFFF0000), F32).astype(BF16)
    return lo, hi


def _mixer_kernel(x_ref, gmix_ref, win_ref, cw_ref, cb_ref, dtb_ref, a_ref, dskip_ref, gate_ref,
                  scw_ref, wout_ref, tri_ref, wg_ref, wu_ref, wd_ref, h_ref, wgu_ref, wdb_ref,
                  conv_ref, cv_ref, act_ref, dt_ref, y_ref, state_ref, sinc_ref, loff_ref, elast_ref):
    ts = x_ref.shape[1]
    wgu_ref[0, :, :D_EXPERT] = wg_ref[0].astype(BF16)
    wgu_ref[0, :, D_EXPERT:] = wu_ref[0].astype(BF16)
    wdb_ref[0] = wd_ref[0].astype(BF16)
    s = pl.program_id(1)
    x = x_ref[0]
    nb = _rms(x, gmix_ref[...]).astype(BF16)

    @pl.when(s == 0)
    def _():
        conv_ref[0:HALO, :] = jnp.zeros((HALO, XBC_WIDTH), F32)
        cv_ref[0:HALO, :] = jnp.zeros((HALO, SC_WIDTH), F32)
        state_ref[...] = jnp.zeros(state_ref.shape, F32)

    @pl.when(s > 0)
    def _():
        conv_ref[0:HALO, :] = conv_ref[ts:ts + HALO, :]
        cv_ref[0:HALO, :] = cv_ref[ts:ts + HALO, :]

    xbc = _dot(nb, win_ref[:, COL_XBC:COL_XBC + XBC_WIDTH])
    conv_ref[HALO:HALO + ts, :] = xbc
    acc = cb_ref[...] + cw_ref[3:4, :] * xbc
    for i in range(3):
        acc = acc + cw_ref[i:i + 1, :] * conv_ref[HALO - 3 + i:HALO - 3 + i + ts, :]
    act_ref[...] = _silu(acc)

    dtr = _dot(nb, win_ref[:, COL_DT:COL_DT + LANES]) + dtb_ref[...]
    dt_ref[...] = jnp.maximum(dtr, 0.0) + jnp.log1p(jnp.exp(-jnp.abs(dtr)))

    row = lax.broadcasted_iota(jnp.int32, (SSD_CHUNK, SSD_CHUNK), 0)
    col = lax.broadcasted_iota(jnp.int32, (SSD_CHUNK, SSD_CHUNK), 1)
    causal = row >= col
    lo_q = col < SSD_HEAD_DIM
    lo_n = lax.broadcasted_iota(jnp.int32, (SSD_STATE, SSD_CHUNK), 1) < SSD_HEAD_DIM
    lo_e = lax.broadcasted_iota(jnp.int32, (SUBLANES, SSD_CHUNK), 1) < SSD_HEAD_DIM

    def chunk_prologue(c):
        rows = slice(c * SSD_CHUNK, (c + 1) * SSD_CHUNK)
        dt = dt_ref[rows, :]
        adt = dt * a_ref[...]
        cs = jnp.dot(tri_ref[...], adt, preferred_element_type=F32,
                     precision=lax.Precision.HIGHEST)
        bm = act_ref[rows, SSD_WIDTH:SSD_WIDTH + SSD_GROUPS * SSD_STATE]
        cm = act_ref[rows, SSD_WIDTH + SSD_GROUPS * SSD_STATE:XBC_WIDTH]
        cbs = []
        for g in range(SSD_GROUPS):
            cg = cm[:, g * SSD_STATE:(g + 1) * SSD_STATE]
            bg = bm[:, g * SSD_STATE:(g + 1) * SSD_STATE]
            cbs.append(lax.dot_general(cg.astype(BF16), bg.astype(BF16), (((1,), (1,)), ((), ())),
                                       preferred_element_type=F32))
        return cs, cs.T, dt.T, bm.T, cm, pltpu.roll(cm, SSD_STATE, axis=1), cbs

    def chunk_body(c, prologue):
        rows = slice(c * SSD_CHUNK, (c + 1) * SSD_CHUNK)
        cs, cs_t, dt_t, bm_t, cm, cm_sw, cbs = prologue
        for g in range(SSD_GROUPS):
            bg_t = bm_t[g * SSD_STATE:(g + 1) * SSD_STATE, :]
            cb = cbs[g]
            c2 = jnp.where(lo_q, cm, cm_sw) if g == 0 else jnp.where(lo_q, cm_sw, cm)
            for jj in range(SSD_HEADS_PER_GROUP // 2):
                j = g * (SSD_HEADS_PER_GROUP // 2) + jj
                idx = c * (SSD_HEADS // 2) + j
                xp = act_ref[rows, j * LANES:(j + 1) * LANES]
                rhs = jnp.concatenate(
                    [jnp.where(lo_q, xp, 0.0), jnp.where(lo_q, 0.0, xp)], axis=0).astype(BF16)
                diag, off, w_rows, e_last = [], [], [], []
                for h in (2 * j, 2 * j + 1):
                    cs_col = jnp.broadcast_to(cs[:, h:h + 1], (SSD_CHUNK, SSD_CHUNK))
                    cs_row = cs_t[h:h + 1, :]
                    dt_row = dt_t[h:h + 1, :]
                    decay = jnp.exp(jnp.where(causal, cs_col - cs_row, -jnp.inf))
                    diag.append((cb * dt_row) * decay)
                    off.append(jnp.exp(cs_col))
                    cl = cs_t[h:h + 1, SSD_CHUNK - 1:SSD_CHUNK]
                    w_rows.append(jnp.exp(cl - cs_row) * dt_row)
                    e_last.append(jnp.exp(cl))
                y_ref[rows, j * LANES:(j + 1) * LANES] = _dot(jnp.concatenate(diag, axis=1).astype(BF16), rhs)
                lhs2 = jnp.concatenate([bg_t * w_rows[0], bg_t * w_rows[1]], axis=1).astype(BF16)
                sinc_ref[idx] = _dot(lhs2, rhs)
                loff_ref[idx] = (c2 * jnp.where(lo_q, off[0], off[1])).astype(BF16)
                elast_ref[idx] = jnp.where(lo_e, e_last[0], e_last[1])

    n_chunks = ts // SSD_CHUNK
    prologues = [chunk_prologue(c) for c in range(n_chunks)]

    g_c = _dot(nb, win_ref[:, COL_GC:COL_GC + SC_WIDTH])
    v = _dot(nb, win_ref[:, COL_V:COL_V + SC_WIDTH])
    u = g_c * v
    cv_ref[HALO:HALO + ts, :] = u
    conv3 = scw_ref[2:3, :] * u
    for i in range(2):
        conv3 = conv3 + scw_ref[i:i + 1, :] * cv_ref[HALO - 2 + i:HALO - 2 + i + ts, :]
    g_b = _dot(nb, win_ref[:, COL_GB:COL_GB + SC_WIDTH])
    out_sc = _dot((g_b * conv3).astype(BF16), wout_ref[SSD_WIDTH:, :])
    gate_z = _silu(_dot(nb, win_ref[:, COL_Z:COL_Z + SSD_WIDTH]))

    for c in range(n_chunks):
        chunk_body(c, prologues[c])

    for c in range(n_chunks):
        for j in range(SSD_HEADS // 2):
            idx = c * (SSD_HEADS // 2) + j
            stp = state_ref[j]
            rhs_st = jnp.concatenate(
                [jnp.where(lo_n, stp, 0.0), jnp.where(lo_n, 0.0, stp)], axis=0).astype(BF16)
            y_ref[c * SSD_CHUNK:(c + 1) * SSD_CHUNK, j * LANES:(j + 1) * LANES] += _dot(loff_ref[idx], rhs_st)
            state_ref[j] = elast_ref[idx][0:1, :] * stp + sinc_ref[idx]

    y = y_ref[...] + act_ref[:, 0:SSD_WIDTH] * dskip_ref[...]
    gated = y * gate_z
    gw = SSD_WIDTH // SSD_GROUPS
    y_ssd = jnp.concatenate(
        [_rms(gated[:, g * gw:(g + 1) * gw], gate_ref[:, g * gw:(g + 1) * gw]) for g in range(SSD_GROUPS)],
        axis=-1).astype(BF16)
    h_ref[0] = x + out_sc + _dot(y_ssd, wout_ref[0:SSD_WIDTH, :])


def _const_spec(shape):
    nd = len(shape)
    return pl.BlockSpec(shape, lambda *_: (0,) * nd)


def _mixer(x, gmix, win, cw, cb, dtb, a_row, dskip, gate, scw, wout, tri, w_gate, w_up, w_down):
    bsz, seq, d = x.shape
    ts = SEQ_TILE
    nt = seq // ts
    consts = (gmix, win, cw, cb, dtb, a_row, dskip, gate, scw, wout, tri)
    n_exp, _, de = w_gate.shape
    per_expert = (bsz * nt) // n_exp
    assert per_expert * n_exp == bsz * nt and d % per_expert == 0 and de % per_expert == 0
    rg, rd = d // per_expert, de // per_expert

    def slice_map(b, s):
        i = b * nt + s
        return (i // per_expert, i % per_expert, 0)
    return pl.pallas_call(
        _mixer_kernel,
        grid=(bsz, nt),
        in_specs=([pl.BlockSpec((1, ts, d), lambda b, s: (b, s, 0))] + [_const_spec(c.shape) for c in consts]
                  + [pl.BlockSpec((1, rg, de), slice_map), pl.BlockSpec((1, rg, de), slice_map),
                     pl.BlockSpec((1, rd, d), slice_map)]),
        out_specs=[pl.BlockSpec((1, ts, d), lambda b, s: (b, s, 0)),
                   pl.BlockSpec((1, rg, 2 * de), slice_map), pl.BlockSpec((1, rd, d), slice_map)],
        out_shape=[jax.ShapeDtypeStruct((bsz, seq, d), F32),
                   jax.ShapeDtypeStruct((n_exp, d, 2 * de), BF16), jax.ShapeDtypeStruct((n_exp, de, d), BF16)],
        scratch_shapes=[
            pltpu.VMEM((ts + HALO, XBC_WIDTH), F32),
            pltpu.VMEM((ts + HALO, SC_WIDTH), F32),
            pltpu.VMEM((ts, XBC_WIDTH), F32),
            pltpu.VMEM((ts, LANES), F32),
            pltpu.VMEM((ts, SSD_WIDTH), F32),
            pltpu.VMEM((SSD_HEADS // 2, SSD_STATE, 2 * SSD_HEAD_DIM), F32),
            pltpu.VMEM((ts // SSD_CHUNK * SSD_HEADS // 2, SSD_STATE, 2 * SSD_HEAD_DIM), F32),
            pltpu.VMEM((ts // SSD_CHUNK * SSD_HEADS // 2, SSD_CHUNK, 2 * SSD_STATE), BF16),
            pltpu.VMEM((ts // SSD_CHUNK * SSD_HEADS // 2, SUBLANES, 2 * SSD_HEAD_DIM), F32),
        ],
        compiler_params=pltpu.CompilerParams(
            dimension_semantics=("arbitrary", "arbitrary"), vmem_limit_bytes=VMEM_LIMIT_BYTES),
        name="mixer",
    )(x, *consts, w_gate, w_up, w_down)


def _kv_kernel(mem_ref, g_ref, wkv_ref, kt_ref, v_ref):
    d = mem_ref.shape[2]
    mn = _rms(mem_ref[0], g_ref[...]).astype(BF16)
    kv = _dot(mn, wkv_ref[...])
    kt_ref[0] = (kv[:, :d] * (XA_HEAD_DIM ** -0.5)).T.astype(BF16)
    v_ref[0] = kv[:, d:].astype(BF16)


def _memory_kv(mem, g, wkv):
    bsz, m, d = mem.shape
    return pl.pallas_call(
        _kv_kernel,
        grid=(bsz,),
        in_specs=[pl.BlockSpec((1, m, d), lambda b: (b, 0, 0)), _const_spec(g.shape), _const_spec(wkv.shape)],
        out_specs=[pl.BlockSpec((1, d, m), lambda b: (b, 0, 0)), pl.BlockSpec((1, m, d), lambda b: (b, 0, 0))],
        out_shape=[jax.ShapeDtypeStruct((bsz, d, m), BF16), jax.ShapeDtypeStruct((bsz, m, d), BF16)],
        compiler_params=pltpu.CompilerParams(
            dimension_semantics=("arbitrary",), vmem_limit_bytes=VMEM_LIMIT_BYTES),
        name="memory_kv",
    )(mem, g, wkv)


def _xattn_router_kernel(h1_ref, gx_ref, wq_ref, kt_ref, v_ref, wo_ref, gm_ref, wr_ref, br_ref, upper_ref,
                         lower_ref, h2_ref, xs_ref, rinfo_ref, seg_ref, counts_ref, zero_ref, carry_ref):
    ts = h1_ref.shape[1]
    first = jnp.logical_and(pl.program_id(0) == 0, pl.program_id(1) == 0)

    @pl.when(first)
    def _():
        carry_ref[...] = jnp.zeros(carry_ref.shape, F32)

    h1 = h1_ref[0]
    q = _dot(_rms(h1, gx_ref[...]).astype(BF16), wq_ref[...]).astype(BF16)
    outs = []
    for h in range(XA_HEADS):
        hs = slice(h * XA_HEAD_DIM, (h + 1) * XA_HEAD_DIM)
        sc = _dot(q[:, hs], kt_ref[0, hs, :])
        p = jnp.exp(sc - jnp.max(sc, axis=-1, keepdims=True))
        o = _dot(p.astype(BF16), v_ref[0, :, hs]) / jnp.sum(p, axis=-1, keepdims=True)
        outs.append(o.astype(BF16))
    h2 = h1 + _dot(jnp.concatenate(outs, axis=-1), wo_ref[...])
    h2_ref[0] = h2

    n3 = _rms(h2, gm_ref[...])
    n3b = n3.astype(BF16)
    half = n3.shape[1] // 2
    n3r = n3b.astype(F32)
    zero_ref[...] = jnp.zeros(zero_ref.shape, zero_ref.dtype)

    n3_lo = (n3 - n3r).astype(BF16)
    lg2 = _dot(n3b, wr_ref[...])
    logits = lg2[:, :LANES] + lg2[:, LANES:] + _dot(n3_lo, wr_ref[:, :LANES]) + br_ref[...]
    lt = logits.T

    l0, l1, l2, l3 = (lt[ROUTER_GROUP_ROW + i:ROUTER_GROUP_ROW + i + 1, :] for i in range(N_GROUPS_MOE))
    m = jnp.maximum(jnp.maximum(l0, l1), jnp.maximum(l2, l3))
    gidx = jnp.where(l0 == m, 0, jnp.where(l1 == m, 1, jnp.where(l2 == m, 2, 3)))
    g_w = 1.0 / (jnp.exp(l0 - m) + jnp.exp(l1 - m) + jnp.exp(l2 - m) + jnp.exp(l3 - m))

    def group_rows(g):
        r = ROUTER_EXPERT_ROW + g * EXPERTS_PER_GROUP
        return lt[r:r + EXPERTS_PER_GROUP, :]

    el = jnp.where(gidx == 0, group_rows(0),
                   jnp.where(gidx == 1, group_rows(1), jnp.where(gidx == 2, group_rows(2), group_rows(3))))
    sub = lax.broadcasted_iota(jnp.int32, el.shape, 0)
    m1 = jnp.max(el, axis=0, keepdims=True)
    i1 = jnp.min(jnp.where(el == m1, sub, EXPERTS_PER_GROUP), axis=0, keepdims=True)
    el2 = jnp.where(sub == i1, -jnp.inf, el)
    m2 = jnp.max(el2, axis=0, keepdims=True)
    i2 = jnp.min(jnp.where(el2 == m2, sub, EXPERTS_PER_GROUP), axis=0, keepdims=True)
    e2 = jnp.exp(m2 - m1)
    w1 = 1.0 / (1.0 + e2)
    w2 = e2 / (1.0 + e2)
    e_a = gidx * EXPERTS_PER_GROUP + i1
    e_b = gidx * EXPERTS_PER_GROUP + i2

    eio = lax.broadcasted_iota(jnp.int32, (N_EXPERTS, ts), 0)
    oh_a = (eio == e_a).astype(F32)
    oh_b = (eio == e_b).astype(F32)
    prefix = _dot(jnp.concatenate([oh_a, oh_b], axis=0).astype(BF16), upper_ref[...])
    tot_a = jnp.sum(oh_a, axis=1, keepdims=True)
    tot_b = jnp.sum(oh_b, axis=1, keepdims=True)
    groups = jnp.broadcast_to(jnp.floor((tot_a + tot_b + (ROW_GROUP - 1)) * (1.0 / ROW_GROUP)),
                              (N_EXPERTS, LANES))
    first_group = _dot(lower_ref[...], groups.astype(BF16))
    base = ROW_GROUP * first_group[:, 0:1]
    pos_a = jnp.sum(oh_a * (base + prefix[:N_EXPERTS]), axis=0, keepdims=True)
    pos_b = jnp.sum(oh_b * (base + tot_a + prefix[N_EXPERTS:]), axis=0, keepdims=True)

    carry = carry_ref[...]
    lane = lax.broadcasted_iota(jnp.int32, (N_EXPERTS, LANES), 1)
    seg_ref[...] = jnp.where(lane == 0, groups, jnp.where(lane == 1, first_group,
                                                          jnp.where(lane == 2, carry, 0.0)))
    carry_ref[...] = carry + groups
    counts_ref[...] = carry + groups

    n_local = xs_ref.shape[0]
    chunk = n_local // PERM_CHUNKS
    ipos_a, ipos_b = pos_a.astype(jnp.int32), pos_b.astype(jnp.int32)
    c_a, c_b = g_w * w1, g_w * w2
    for k in range(PERM_CHUNKS):
        rows = slice(k * chunk, (k + 1) * chunk)
        riota = lax.broadcasted_iota(jnp.int32, (chunk, ts), 0) + k * chunk
        sel_a = riota == ipos_a
        sel_b = riota == ipos_b
        perm = jnp.where(sel_a, 1.0, jnp.where(sel_b, 1.0, 0.0)).astype(BF16)
        xs_ref[rows, :half] = _pack_halves(_dot(perm, n3b))
        w_sorted = jnp.sum(jnp.where(sel_a, c_a, jnp.where(sel_b, c_b, 0.0)),
                           axis=1, keepdims=True)
        xs_ref[rows, half:] = pltpu.bitcast(jnp.broadcast_to(w_sorted, (chunk, LANES)), jnp.uint32)

    res = jnp.concatenate([pos_a, pos_b, jnp.zeros((LANES - 2, ts), F32)], axis=0)
    rinfo_ref[...] = res.T


def _xattn_router(h1, gx, wq, kt, v, wo, gm, wr, br, upper, lower, n_rows):
    bsz, seq, d = h1.shape
    ts = SEQ_TILE
    m = v.shape[1]
    nt = seq // ts
    n_tiles = bsz * nt
    zero_rows = n_rows // n_tiles
    assert zero_rows * n_tiles == n_rows and zero_rows % SUBLANES == 0
    tok_spec = pl.BlockSpec((1, ts, d), lambda b, s: (b, s, 0))
    return pl.pallas_call(
        _xattn_router_kernel,
        grid=(bsz, nt),
        in_specs=[tok_spec, _const_spec(gx.shape), _const_spec(wq.shape),
                  pl.BlockSpec((1, d, m), lambda b, s: (b, 0, 0)),
                  pl.BlockSpec((1, m, d), lambda b, s: (b, 0, 0)),
                  _const_spec(wo.shape), _const_spec(gm.shape), _const_spec(wr.shape), _const_spec(br.shape),
                  _const_spec(upper.shape), _const_spec(lower.shape)],
        out_specs=[tok_spec,
                   pl.BlockSpec((LOCAL_ROWS, ROW_WORDS), lambda b, s: (b * nt + s, 0)),
                   pl.BlockSpec((ts, LANES), lambda b, s: (b * nt + s, 0)),
                   pl.BlockSpec((N_EXPERTS, LANES), lambda b, s: (b * nt + s, 0)),
                   pl.BlockSpec((N_EXPERTS, LANES), lambda b, s: (0, 0)),
                   pl.BlockSpec((zero_rows, ROW_WORDS), lambda b, s: (b * nt + s, 0))],
        out_shape=[jax.ShapeDtypeStruct((bsz, seq, d), F32),
                   jax.ShapeDtypeStruct((n_tiles * LOCAL_ROWS, ROW_WORDS), jnp.uint32),
                   jax.ShapeDtypeStruct((bsz * seq, LANES), F32),
                   jax.ShapeDtypeStruct((n_tiles * N_EXPERTS, LANES), F32),
                   jax.ShapeDtypeStruct((N_EXPERTS, LANES), F32),
                   jax.ShapeDtypeStruct((n_rows, ROW_WORDS), jnp.uint32)],
        scratch_shapes=[pltpu.VMEM((N_EXPERTS, LANES), F32)],
        compiler_params=pltpu.CompilerParams(
            dimension_semantics=("arbitrary", "arbitrary"), vmem_limit_bytes=VMEM_LIMIT_BYTES),
        name="xattn_router",
    )(h1, gx, wq, kt, v, wo, gm, wr, br, upper, lower)


TAB_COUNT, TAB_DST, TAB_LOCAL = (k * N_EXPERTS for k in range(3))


SEGMENT_RUN_LOG2 = 3
SEGMENT_CHUNKS = (4, 2, 1)


def _rows(first_group, n_groups):
    return pl.ds(pl.multiple_of(first_group * ROW_GROUP, ROW_GROUP), n_groups * ROW_GROUP)


def _for_each_segment_copy(tab_ref, make_copy, action):
    run = 1 << SEGMENT_RUN_LOG2

    def expert(e, c):
        n = tab_ref[0, 0, TAB_COUNT + e]
        dst0 = tab_ref[0, 0, TAB_DST + e]
        loc0 = tab_ref[0, 0, TAB_LOCAL + e]

        def full_run(k, c2):
            action(make_copy(loc0 + k * run, dst0 + k * run, run))
            return c2
        lax.fori_loop(0, lax.shift_right_logical(n, SEGMENT_RUN_LOG2), full_run, 0)
        for size in SEGMENT_CHUNKS:
            off = n & (-2 * size)

            @pl.when((n & size) != 0)
            def _():
                action(make_copy(loc0 + off, dst0 + off, size))
        return c
    lax.fori_loop(0, N_EXPERTS, expert, 0)


def _dispatch_kernel(tab_ref, xs_ref, buf_in_ref, buf_ref, sem):
    del buf_in_ref

    def make_copy(local, dst, n_groups):
        return pltpu.make_async_copy(xs_ref.at[_rows(local, n_groups)], buf_ref.at[_rows(dst, n_groups)], sem)

    _for_each_segment_copy(tab_ref, make_copy, lambda cp: cp.start())
    _for_each_segment_copy(tab_ref, make_copy, lambda cp: cp.wait())


def _dispatch(table, xs, buf0):
    n_tiles = table.shape[0]
    return pl.pallas_call(
        _dispatch_kernel,
        grid=(n_tiles,),
        in_specs=[pl.BlockSpec((1, 1, table.shape[2]), lambda i: (i, 0, 0), memory_space=pltpu.SMEM),
                  pl.BlockSpec((LOCAL_ROWS, xs.shape[1]), lambda i: (i, 0)),
                  pl.BlockSpec(memory_space=pl.ANY)],
        out_specs=pl.BlockSpec(memory_space=pl.ANY),
        out_shape=jax.ShapeDtypeStruct(buf0.shape, buf0.dtype),
        scratch_shapes=[pltpu.SemaphoreType.DMA],
        input_output_aliases={2: 0},
        compiler_params=pltpu.CompilerParams(
            dimension_semantics=("arbitrary",), vmem_limit_bytes=VMEM_LIMIT_BYTES),
        name="dispatch",
    )(table, xs, buf0)


def _expert_kernel(be_ref, nused_ref, xb_ref, wgu_a_ref, wd_a_ref, wgu_b_ref, wd_b_ref, y_ref):
    del be_ref
    i = pl.program_id(0)
    bm = xb_ref.shape[0] // BLOCKS_PER_STEP
    half = xb_ref.shape[1] - LANES

    @pl.when(BLOCKS_PER_STEP * i < nused_ref[0])
    def _():
        for k, (wgu_ref, wd_ref) in enumerate(((wgu_a_ref, wd_a_ref), (wgu_b_ref, wd_b_ref))):
            rows = slice(k * bm, (k + 1) * bm)
            x_lo, x_hi = _unpack_halves(xb_ref[rows, :half])
            gu = _dot(x_lo, wgu_ref[0, :half, :]) + _dot(x_hi, wgu_ref[0, half:, :])
            mid = (_silu(gu[:, :D_EXPERT]) * gu[:, D_EXPERT:]).astype(BF16)
            comb = pltpu.bitcast(xb_ref[rows, half:half + LANES], F32)[:, 0:1]
            y = _dot(mid, wd_ref[0]) * comb
            y_ref[rows, :] = _pack_halves(y.astype(BF16).astype(F32))

    @pl.when(BLOCKS_PER_STEP * i >= nused_ref[0])
    def _():
        y_ref[...] = jnp.zeros(y_ref.shape, y_ref.dtype)


def _experts(block_expert, n_used, xbuf, wgu, wd):
    w = xbuf.shape[1]
    bm = EXPERT_BLOCK
    n_blocks = block_expert.shape[0]
    assert n_blocks % BLOCKS_PER_STEP == 0
    n_rows = n_blocks * bm
    d = wd.shape[2]

    def weights_of(k):
        return lambda i, be, nu: (be[BLOCKS_PER_STEP * i + k], 0, 0)
    weight_specs = []
    for k in range(BLOCKS_PER_STEP):
        weight_specs += [pl.BlockSpec((1,) + wgu.shape[1:], weights_of(k)),
                         pl.BlockSpec((1,) + wd.shape[1:], weights_of(k))]
    grid_spec = pltpu.PrefetchScalarGridSpec(
        num_scalar_prefetch=2,
        grid=(n_blocks // BLOCKS_PER_STEP,),
        in_specs=[pl.BlockSpec((BLOCKS_PER_STEP * bm, w), lambda i, be, nu: (i, 0))] + weight_specs,
        out_specs=pl.BlockSpec((BLOCKS_PER_STEP * bm, d // 2), lambda i, be, nu: (i, 0)),
    )
    return pl.pallas_call(
        _expert_kernel,
        grid_spec=grid_spec,
        out_shape=jax.ShapeDtypeStruct((n_rows, d // 2), jnp.uint32),
        compiler_params=pltpu.CompilerParams(
            dimension_semantics=("arbitrary",), vmem_limit_bytes=VMEM_LIMIT_BYTES),
        name="experts",
    )(block_expert, n_used, xbuf, *([wgu, wd] * BLOCKS_PER_STEP))


def _combine_kernel(tab_ref, tab_next_ref, h2_ref, rinfo_ref, gf_ref, ybuf_ref, out_ref, ys_ref, sem):
    tr = h2_ref.shape[0]
    n_local = ys_ref.shape[1]
    i = pl.program_id(0)
    slot = i % 2

    def copier(s):
        def make_copy(local, src, n_groups):
            return pltpu.make_async_copy(ybuf_ref.at[_rows(src, n_groups)],
                                         ys_ref.at[s, _rows(local, n_groups)], sem.at[s])
        return make_copy

    def gather(tab, s):
        covered = tab[0, 0, TAB_LOCAL + N_EXPERTS - 1] + tab[0, 0, TAB_COUNT + N_EXPERTS - 1]

        def zero(g, c):
            ys_ref[s, _rows(g, 1), :] = jnp.zeros((ROW_GROUP, ys_ref.shape[2]), ys_ref.dtype)
            return c
        lax.fori_loop(covered, n_local // ROW_GROUP, zero, 0)
        _for_each_segment_copy(tab, copier(s), lambda cp: cp.start())

    @pl.when(i == 0)
    def _():
        gather(tab_ref, 0)

    @pl.when(i + 1 < pl.num_programs(0))
    def _():
        gather(tab_next_ref, 1 - slot)

    _for_each_segment_copy(tab_ref, copier(slot), lambda cp: cp.wait())

    info = rinfo_ref[...]
    ciota = lax.broadcasted_iota(jnp.int32, (tr, n_local), 1)
    pick = jnp.where(ciota == info[:, 0:1].astype(jnp.int32), 1.0,
                     jnp.where(ciota == info[:, 1:2].astype(jnp.int32), 1.0, 0.0)).astype(BF16)
    y_lo, y_hi = _unpack_halves(ys_ref[slot])
    h = h2_ref[...] + jnp.concatenate([_dot(pick, y_lo), _dot(pick, y_hi)], axis=-1)
    out_ref[...] = _rms(h, gf_ref[...])


def _combine(table, h2, rinfo, gf, ybuf):
    t, d = h2.shape
    tr = SEQ_TILE
    n_tiles = t // tr
    tab_spec = (1, 1, table.shape[2])
    return pl.pallas_call(
        _combine_kernel,
        grid=(n_tiles,),
        in_specs=[pl.BlockSpec(tab_spec, lambda i: (i, 0, 0), memory_space=pltpu.SMEM),
                  pl.BlockSpec(tab_spec, lambda i: (jnp.minimum(i + 1, n_tiles - 1), 0, 0),
                               memory_space=pltpu.SMEM),
                  pl.BlockSpec((tr, d), lambda i: (i, 0)),
                  pl.BlockSpec((tr, LANES), lambda i: (i, 0)),
                  _const_spec(gf.shape),
                  pl.BlockSpec(memory_space=pl.ANY)],
        out_specs=pl.BlockSpec((tr, d), lambda i: (i, 0)),
        out_shape=jax.ShapeDtypeStruct((t, d), F32),
        scratch_shapes=[pltpu.VMEM((2, LOCAL_ROWS, d // 2), jnp.uint32), pltpu.SemaphoreType.DMA((2,))],
        compiler_params=pltpu.CompilerParams(
            dimension_semantics=("arbitrary",), vmem_limit_bytes=VMEM_LIMIT_BYTES),
        name="combine",
    )(table, table, h2, rinfo, gf, ybuf)


def _pad_lanes(v):
    return jnp.pad(v, (0, LANES - v.shape[0]))[None, :]


def kernel(x, mem, norm_mem, norm_mix, w_in, conv_ssd_w, conv_ssd_b, dt_bias, a_log, d_skip, norm_ssd_gate,
           conv_short_w, w_out, norm_xattn, w_q, w_kv, w_o, norm_moe, w_router_group, b_router_group,
           w_router_expert, b_router_expert, w_gate, w_up, w_down, norm_final):
    bsz, seq, d = x.shape
    t = bsz * seq
    assert w_in.shape[0] == 1, "single-layer model"
    assert d == D_MODEL and seq % SEQ_TILE == 0

    tri = jnp.tril(jnp.ones((SSD_CHUNK, SSD_CHUNK), F32))
    upper = jnp.triu(jnp.ones((SEQ_TILE, SEQ_TILE), F32), k=1).astype(BF16)

    o2 = SSD_WIDTH + XBC_WIDTH
    o3 = o2 + SSD_HEADS
    wl = w_in[0]
    win = jnp.concatenate(
        [wl[:, :o2], wl[:, o3:], wl[:, o2:o3], jnp.zeros((d, LANES - SSD_HEADS), F32)], axis=1).astype(BF16)
    h1, wgu, wdb = _mixer(x, norm_mix, win, conv_ssd_w[0], conv_ssd_b, _pad_lanes(dt_bias[0]),
                _pad_lanes(-jnp.exp(a_log[0])), jnp.repeat(d_skip[0], SSD_HEAD_DIM)[None, :], norm_ssd_gate,
                conv_short_w[0], w_out[0].astype(BF16), tri, w_gate[0], w_up[0], w_down[0])

    kt, v = _memory_kv(mem, norm_mem[None, :], w_kv[0].astype(BF16))
    wr = jnp.zeros((d, LANES), F32)
    wr = wr.at[:, ROUTER_GROUP_ROW:ROUTER_GROUP_ROW + N_GROUPS_MOE].set(w_router_group[0])
    wr = wr.at[:, ROUTER_EXPERT_ROW:ROUTER_EXPERT_ROW + N_EXPERTS].set(w_router_expert[0])
    wr_hi = wr.astype(BF16)
    wr_lo = (wr - wr_hi.astype(F32)).astype(BF16)
    br = jnp.zeros((LANES,), F32)
    br = br.at[ROUTER_GROUP_ROW:ROUTER_GROUP_ROW + N_GROUPS_MOE].set(b_router_group[0])
    br = br.at[ROUTER_EXPERT_ROW:ROUTER_EXPERT_ROW + N_EXPERTS].set(b_router_expert[0])
    lower = jnp.tril(jnp.ones((N_EXPERTS, N_EXPERTS), F32), k=-1).astype(BF16)
    n_tiles = t // SEQ_TILE
    bgroups = EXPERT_BLOCK // ROW_GROUP
    max_groups = (2 * t + n_tiles * N_EXPERTS * (ROW_GROUP - 1)) // ROW_GROUP
    n_blocks = -(-max_groups // bgroups) + N_EXPERTS
    even = math.lcm((n_tiles * SUBLANES) // math.gcd(n_tiles * SUBLANES, EXPERT_BLOCK), BLOCKS_PER_STEP)
    n_blocks = -(-n_blocks // even) * even
    h2, xs, rinfo, seg, counts, xbuf0 = _xattn_router(
        h1, norm_xattn, w_q[0].astype(BF16), kt, v, w_o[0].astype(BF16), norm_moe,
        jnp.concatenate([wr_hi, wr_lo], axis=1), br[None, :], upper, lower, n_blocks * EXPERT_BLOCK)

    seg = seg.reshape(n_tiles, N_EXPERTS, LANES)[:, :, 0:3].astype(jnp.int32)
    seg_groups, seg_local, seg_before = seg[:, :, 0], seg[:, :, 1], seg[:, :, 2]
    sizes = counts[:, 0].astype(jnp.int32)
    padded = ((sizes + bgroups - 1) // bgroups) * bgroups
    pad_end = jnp.cumsum(padded)
    pad_start = pad_end - padded
    seg_dst = pad_start[None, :] + seg_before
    table = jnp.concatenate([seg_groups, seg_dst, seg_local], axis=1)[:, None, :]
    block_group = jnp.arange(n_blocks, dtype=jnp.int32) * bgroups
    block_expert = jnp.minimum(jnp.sum((pad_end[None, :] <= block_group[:, None]).astype(jnp.int32), axis=-1),
                               N_EXPERTS - 1)
    n_used = (pad_end[-1:] // bgroups).astype(jnp.int32)

    xbuf = _dispatch(table, xs, xbuf0)
    ybuf = _experts(block_expert, n_used, xbuf, wgu, wdb)
    out = _combine(table, h2.reshape(t, d), rinfo, norm_final[None, :], ybuf)
    return out.reshape(bsz, seq, d)
```

```python
import math

import jax
import jax.numpy as jnp
from jax import lax
from jax.experimental import pallas as pl
from jax.experimental.pallas import tpu as pltpu

F32 = jnp.float32
BF16 = jnp.bfloat16
EPS = 1e-6

LANES = 128
SUBLANES = 8
VMEM_LIMIT_BYTES = 56 * 1024 * 1024

D_MODEL = 1024
SSD_WIDTH = 512
SSD_HEAD_DIM = 64
SSD_HEADS = 8
SSD_GROUPS = 2
SSD_HEADS_PER_GROUP = 4
SSD_STATE = 64
SSD_CHUNK = 128
XBC_WIDTH = SSD_WIDTH + 2 * SSD_GROUPS * SSD_STATE
SC_WIDTH = 512
XA_HEADS = 4
XA_HEAD_DIM = 256
N_GROUPS_MOE = 4
EXPERTS_PER_GROUP = 8
N_EXPERTS = 32
D_EXPERT = 512

COL_Z = 0
COL_XBC = COL_Z + SSD_WIDTH
COL_GB = COL_XBC + XBC_WIDTH
COL_GC = COL_GB + SC_WIDTH
COL_V = COL_GC + SC_WIDTH
COL_DT = COL_V + SC_WIDTH
PROJ_WIDTH = COL_DT + LANES

SEQ_TILE = 512
EXPERT_BLOCK = 512
BLOCKS_PER_STEP = 4
HALO = SUBLANES
ROW_GROUP = SUBLANES
LOCAL_ROWS = 2 * SEQ_TILE + N_EXPERTS * ROW_GROUP
ROW_WORDS = D_MODEL // 2 + LANES
PERM_CHUNKS = 4

ROUTER_GROUP_ROW = 0
ROUTER_EXPERT_ROW = SUBLANES


def _rms(x, g):
    return x * lax.rsqrt(jnp.mean(x * x, axis=-1, keepdims=True) + EPS) * g


def _silu(x):
    return x * jax.nn.sigmoid(x)


def _dot(a, b):
    return jnp.dot(a, b, preferred_element_type=F32)


def _pack_halves(v):
    w = v.shape[1] // 2
    lo = lax.shift_right_logical(pltpu.bitcast(v[:, :w], jnp.uint32), jnp.uint32(16))
    hi = pltpu.bitcast(v[:, w:], jnp.uint32) & jnp.uint32(0xFFFF0000)
    return lo | hi


def _unpack_halves(words):
    lo = pltpu.bitcast(lax.shift_left(words, jnp.uint32(16)), F32).astype(BF16)
    hi = pltpu.bitcast(words & jnp.uint32(0xFFFF0000), F32).astype(BF16)
    return lo, hi


def _mixer_kernel(x_ref, gmix_ref, win_ref, cw_ref, cb_ref, dtb_ref, a_ref, dskip_ref, gate_ref,
                  scw_ref, wout_ref, tri_ref, wg_ref, wu_ref, wd_ref, h_ref, wgu_ref, wdb_ref,
                  conv_ref, cv_ref, act_ref, dt_ref, y_ref, state_ref, sinc_ref, loff_ref, elast_ref):
    ts = x_ref.shape[1]
    wgu_ref[0, :, :D_EXPERT] = wg_ref[0].astype(BF16)
    wgu_ref[0, :, D_EXPERT:] = wu_ref[0].astype(BF16)
    wdb_ref[0] = wd_ref[0].astype(BF16)
    s = pl.program_id(1)
    x = x_ref[0]
    nb = _rms(x, gmix_ref[...]).astype(BF16)

    @pl.when(s == 0)
    def _():
        conv_ref[0:HALO, :] = jnp.zeros((HALO, XBC_WIDTH), F32)
        cv_ref[0:HALO, :] = jnp.zeros((HALO, SC_WIDTH), F32)
        state_ref[...] = jnp.zeros(state_ref.shape, F32)

    @pl.when(s > 0)
    def _():
        conv_ref[0:HALO, :] = conv_ref[ts:ts + HALO, :]
        cv_ref[0:HALO, :] = cv_ref[ts:ts + HALO, :]

    xbc = _dot(nb, win_ref[:, COL_XBC:COL_XBC + XBC_WIDTH])
    conv_ref[HALO:HALO + ts, :] = xbc
    acc = cb_ref[...] + cw_ref[3:4, :] * xbc
    for i in range(3):
        acc = acc + cw_ref[i:i + 1, :] * conv_ref[HALO - 3 + i:HALO - 3 + i + ts, :]
    act_ref[...] = _silu(acc)

    dtr = _dot(nb, win_ref[:, COL_DT:COL_DT + LANES]) + dtb_ref[...]
    dt_ref[...] = jnp.maximum(dtr, 0.0) + jnp.log1p(jnp.exp(-jnp.abs(dtr)))

    row = lax.broadcasted_iota(jnp.int32, (SSD_CHUNK, SSD_CHUNK), 0)
    col = lax.broadcasted_iota(jnp.int32, (SSD_CHUNK, SSD_CHUNK), 1)
    causal = row >= col
    lo_q = col < SSD_HEAD_DIM
    lo_n = lax.broadcasted_iota(jnp.int32, (SSD_STATE, SSD_CHUNK), 1) < SSD_HEAD_DIM
    lo_e = lax.broadcasted_iota(jnp.int32, (SUBLANES, SSD_CHUNK), 1) < SSD_HEAD_DIM

    def chunk_prologue(c):
        rows = slice(c * SSD_CHUNK, (c + 1) * SSD_CHUNK)
        dt = dt_ref[rows, :]
        adt = dt * a_ref[...]
        cs = jnp.dot(tri_ref[...], adt, preferred_element_type=F32,
                     precision=lax.Precision.HIGHEST)
        bm = act_ref[rows, SSD_WIDTH:SSD_WIDTH + SSD_GROUPS * SSD_STATE]
        cm = act_ref[rows, SSD_WIDTH + SSD_GROUPS * SSD_STATE:XBC_WIDTH]
        cbs = []
        for g in range(SSD_GROUPS):
            cg = cm[:, g * SSD_STATE:(g + 1) * SSD_STATE]
            bg = bm[:, g * SSD_STATE:(g + 1) * SSD_STATE]
            cbs.append(lax.dot_general(cg.astype(BF16), bg.astype(BF16), (((1,), (1,)), ((), ())),
                                       preferred_element_type=F32))
        return cs, cs.T, dt.T, bm.T, cm, pltpu.roll(cm, SSD_STATE, axis=1), cbs

    def chunk_body(c, prologue):
        rows = slice(c * SSD_CHUNK, (c + 1) * SSD_CHUNK)
        cs, cs_t, dt_t, bm_t, cm, cm_sw, cbs = prologue
        for g in range(SSD_GROUPS):
            bg_t = bm_t[g * SSD_STATE:(g + 1) * SSD_STATE, :]
            cb = cbs[g]
            c2 = jnp.where(lo_q, cm, cm_sw) if g == 0 else jnp.where(lo_q, cm_sw, cm)
            for jj in range(SSD_HEADS_PER_GROUP // 2):
                j = g * (SSD_HEADS_PER_GROUP // 2) + jj
                idx = c * (SSD_HEADS // 2) + j
                xp = act_ref[rows, j * LANES:(j + 1) * LANES]
                rhs = jnp.concatenate(
                    [jnp.where(lo_q, xp, 0.0), jnp.where(lo_q, 0.0, xp)], axis=0).astype(BF16)
                diag, off, w_rows, e_last = [], [], [], []
                for h in (2 * j, 2 * j + 1):
                    cs_col = jnp.broadcast_to(cs[:, h:h + 1], (SSD_CHUNK, SSD_CHUNK))
                    cs_row = cs_t[h:h + 1, :]
                    dt_row = dt_t[h:h + 1, :]
                    decay = jnp.exp(jnp.where(causal, cs_col - cs_row, -jnp.inf))
                    diag.append((cb * dt_row) * decay)
                    off.append(jnp.exp(cs_col))
                    cl = cs_t[h:h + 1, SSD_CHUNK - 1:SSD_CHUNK]
                    w_rows.append(jnp.exp(cl - cs_row) * dt_row)
                    e_last.append(jnp.exp(cl))
                y_ref[rows, j * LANES:(j + 1) * LANES] = _dot(jnp.concatenate(diag, axis=1).astype(BF16), rhs)
                lhs2 = jnp.concatenate([bg_t * w_rows[0], bg_t * w_rows[1]], axis=1).astype(BF16)
                sinc_ref[idx] = _dot(lhs2, rhs)
                loff_ref[idx] = (c2 * jnp.where(lo_q, off[0], off[1])).astype(BF16)
                elast_ref[idx] = jnp.where(lo_e, e_last[0], e_last[1])

    n_chunks = ts // SSD_CHUNK
    prologues = [chunk_prologue(c) for c in range(n_chunks)]

    g_c = _dot(nb, win_ref[:, COL_GC:COL_GC + SC_WIDTH])
    v = _dot(nb, win_ref[:, COL_V:COL_V + SC_WIDTH])
    u = g_c * v
    cv_ref[HALO:HALO + ts, :] = u
    conv3 = scw_ref[2:3, :] * u
    for i in range(2):
        conv3 = conv3 + scw_ref[i:i + 1, :] * cv_ref[HALO - 2 + i:HALO - 2 + i + ts, :]
    g_b = _dot(nb, win_ref[:, COL_GB:COL_GB + SC_WIDTH])
    out_sc = _dot((g_b * conv3).astype(BF16), wout_ref[SSD_WIDTH:, :])
    gate_z = _silu(_dot(nb, win_ref[:, COL_Z:COL_Z + SSD_WIDTH]))

    for c in range(n_chunks):
        chunk_body(c, prologues[c])

    for c in range(n_chunks):
        for j in range(SSD_HEADS // 2):
            idx = c * (SSD_HEADS // 2) + j
            stp = state_ref[j]
            rhs_st = jnp.concatenate(
                [jnp.where(lo_n, stp, 0.0), jnp.where(lo_n, 0.0, stp)], axis=0).astype(BF16)
            y_ref[c * SSD_CHUNK:(c + 1) * SSD_CHUNK, j * LANES:(j + 1) * LANES] += _dot(loff_ref[idx], rhs_st)
            state_ref[j] = elast_ref[idx][0:1, :] * stp + sinc_ref[idx]

    y = y_ref[...] + act_ref[:, 0:SSD_WIDTH] * dskip_ref[...]
    gated = y * gate_z
    gw = SSD_WIDTH // SSD_GROUPS
    y_ssd = jnp.concatenate(
        [_rms(gated[:, g * gw:(g + 1) * gw], gate_ref[:, g * gw:(g + 1) * gw]) for g in range(SSD_GROUPS)],
        axis=-1).astype(BF16)
    h_ref[0] = x + out_sc + _dot(y_ssd, wout_ref[0:SSD_WIDTH, :])


def _const_spec(shape):
    nd = len(shape)
    return pl.BlockSpec(shape, lambda *_: (0,) * nd)


def _mixer(x, gmix, win, cw, cb, dtb, a_row, dskip, gate, scw, wout, tri, w_gate, w_up, w_down):
    bsz, seq, d = x.shape
    ts = SEQ_TILE
    nt = seq // ts
    consts = (gmix, win, cw, cb, dtb, a_row, dskip, gate, scw, wout, tri)
    n_exp, _, de = w_gate.shape
    per_expert = (bsz * nt) // n_exp
    assert per_expert * n_exp == bsz * nt and d % per_expert == 0 and de % per_expert == 0
    rg, rd = d // per_expert, de // per_expert

    def slice_map(b, s):
        i = b * nt + s
        return (i // per_expert, i % per_expert, 0)
    return pl.pallas_call(
        _mixer_kernel,
        grid=(bsz, nt),
        in_specs=([pl.BlockSpec((1, ts, d), lambda b, s: (b, s, 0))] + [_const_spec(c.shape) for c in consts]
                  + [pl.BlockSpec((1, rg, de), slice_map), pl.BlockSpec((1, rg, de), slice_map),
                     pl.BlockSpec((1, rd, d), slice_map)]),
        out_specs=[pl.BlockSpec((1, ts, d), lambda b, s: (b, s, 0)),
                   pl.BlockSpec((1, rg, 2 * de), slice_map), pl.BlockSpec((1, rd, d), slice_map)],
        out_shape=[jax.ShapeDtypeStruct((bsz, seq, d), F32),
                   jax.ShapeDtypeStruct((n_exp, d, 2 * de), BF16), jax.ShapeDtypeStruct((n_exp, de, d), BF16)],
        scratch_shapes=[
            pltpu.VMEM((ts + HALO, XBC_WIDTH), F32),
            pltpu.VMEM((ts + HALO, SC_WIDTH), F32),
            pltpu.VMEM((ts, XBC_WIDTH), F32),
            pltpu.VMEM((ts, LANES), F32),
            pltpu.VMEM((ts, SSD_WIDTH), F32),
            pltpu.VMEM((SSD_HEADS // 2, SSD_STATE, 2 * SSD_HEAD_DIM), F32),
            pltpu.VMEM((ts // SSD_CHUNK * SSD_HEADS // 2, SSD_STATE, 2 * SSD_HEAD_DIM), F32),
            pltpu.VMEM((ts // SSD_CHUNK * SSD_HEADS // 2, SSD_CHUNK, 2 * SSD_STATE), BF16),
            pltpu.VMEM((ts // SSD_CHUNK * SSD_HEADS // 2, SUBLANES, 2 * SSD_HEAD_DIM), F32),
        ],
        compiler_params=pltpu.CompilerParams(
            dimension_semantics=("arbitrary", "arbitrary"), vmem_limit_bytes=VMEM_LIMIT_BYTES),
        name="mixer",
    )(x, *consts, w_gate, w_up, w_down)


def _kv_kernel(mem_ref, g_ref, wkv_ref, kt_ref, v_ref):
    d = mem_ref.shape[2]
    mn = _rms(mem_ref[0], g_ref[...]).astype(BF16)
    kv = _dot(mn, wkv_ref[...])
    kt_ref[0] = (kv[:, :d] * (XA_HEAD_DIM ** -0.5)).T.astype(BF16)
    v_ref[0] = kv[:, d:].astype(BF16)


def _memory_kv(mem, g, wkv):
    bsz, m, d = mem.shape
    return pl.pallas_call(
        _kv_kernel,
        grid=(bsz,),
        in_specs=[pl.BlockSpec((1, m, d), lambda b: (b, 0, 0)), _const_spec(g.shape), _const_spec(wkv.shape)],
        out_specs=[pl.BlockSpec((1, d, m), lambda b: (b, 0, 0)), pl.BlockSpec((1, m, d), lambda b: (b, 0, 0))],
        out_shape=[jax.ShapeDtypeStruct((bsz, d, m), BF16), jax.ShapeDtypeStruct((bsz, m, d), BF16)],
        compiler_params=pltpu.CompilerParams(
            dimension_semantics=("arbitrary",), vmem_limit_bytes=VMEM_LIMIT_BYTES),
        name="memory_kv",
    )(mem, g, wkv)


def _xattn_router_kernel(h1_ref, gx_ref, wq_ref, kt_ref, v_ref, wo_ref, gm_ref, wr_ref, br_ref, upper_ref,
                         lower_ref, h2_ref, xs_ref, rinfo_ref, seg_ref, counts_ref, zero_ref, carry_ref):
    ts = h1_ref.shape[1]
    first = jnp.logical_and(pl.program_id(0) == 0, pl.program_id(1) == 0)

    @pl.when(first)
    def _():
        carry_ref[...] = jnp.zeros(carry_ref.shape, F32)

    h1 = h1_ref[0]
    q = _dot(_rms(h1, gx_ref[...]).astype(BF16), wq_ref[...]).astype(BF16)
    outs = []
    for h in range(XA_HEADS):
        hs = slice(h * XA_HEAD_DIM, (h + 1) * XA_HEAD_DIM)
        sc = _dot(q[:, hs], kt_ref[0, hs, :])
        p = jnp.exp(sc - jnp.max(sc, axis=-1, keepdims=True))
        o = _dot(p.astype(BF16), v_ref[0, :, hs]) / jnp.sum(p, axis=-1, keepdims=True)
        outs.append(o.astype(BF16))
    h2 = h1 + _dot(jnp.concatenate(outs, axis=-1), wo_ref[...])
    h2_ref[0] = h2

    n3 = _rms(h2, gm_ref[...])
    n3b = n3.astype(BF16)
    half = n3.shape[1] // 2
    n3r = n3b.astype(F32)
    zero_ref[...] = jnp.zeros(zero_ref.shape, zero_ref.dtype)

    n3_lo = (n3 - n3r).astype(BF16)
    lg2 = _dot(n3b, wr_ref[...])
    logits = lg2[:, :LANES] + lg2[:, LANES:] + _dot(n3_lo, wr_ref[:, :LANES]) + br_ref[...]
    lt = logits.T

    l0, l1, l2, l3 = (lt[ROUTER_GROUP_ROW + i:ROUTER_GROUP_ROW + i + 1, :] for i in range(N_GROUPS_MOE))
    m = jnp.maximum(jnp.maximum(l0, l1), jnp.maximum(l2, l3))
    gidx = jnp.where(l0 == m, 0, jnp.where(l1 == m, 1, jnp.where(l2 == m, 2, 3)))
    g_w = 1.0 / (jnp.exp(l0 - m) + jnp.exp(l1 - m) + jnp.exp(l2 - m) + jnp.exp(l3 - m))

    def group_rows(g):
        r = ROUTER_EXPERT_ROW + g * EXPERTS_PER_GROUP
        return lt[r:r + EXPERTS_PER_GROUP, :]

    el = jnp.where(gidx == 0, group_rows(0),
                   jnp.where(gidx == 1, group_rows(1), jnp.where(gidx == 2, group_rows(2), group_rows(3))))
    sub = lax.broadcasted_iota(jnp.int32, el.shape, 0)
    m1 = jnp.max(el, axis=0, keepdims=True)
    i1 = jnp.min(jnp.where(el == m1, sub, EXPERTS_PER_GROUP), axis=0, keepdims=True)
    el2 = jnp.where(sub == i1, -jnp.inf, el)
    m2 = jnp.max(el2, axis=0, keepdims=True)
    i2 = jnp.min(jnp.where(el2 == m2, sub, EXPERTS_PER_GROUP), axis=0, keepdims=True)
    e2 = jnp.exp(m2 - m1)
    w1 = 1.0 / (1.0 + e2)
    w2 = e2 / (1.0 + e2)
    e_a = gidx * EXPERTS_PER_GROUP + i1
    e_b = gidx * EXPERTS_PER_GROUP + i2

    eio = lax.broadcasted_iota(jnp.int32, (N_EXPERTS, ts), 0)
    oh_a = (eio == e_a).astype(F32)
    oh_b = (eio == e_b).astype(F32)
    prefix = _dot(jnp.concatenate([oh_a, oh_b], axis=0).astype(BF16), upper_ref[...])
    tot_a = jnp.sum(oh_a, axis=1, keepdims=True)
    tot_b = jnp.sum(oh_b, axis=1, keepdims=True)
    groups = jnp.broadcast_to(jnp.floor((tot_a + tot_b + (ROW_GROUP - 1)) * (1.0 / ROW_GROUP)),
                              (N_EXPERTS, LANES))
    first_group = _dot(lower_ref[...], groups.astype(BF16))
    base = ROW_GROUP * first_group[:, 0:1]
    pos_a = jnp.sum(oh_a * (base + prefix[:N_EXPERTS]), axis=0, keepdims=True)
    pos_b = jnp.sum(oh_b * (base + tot_a + prefix[N_EXPERTS:]), axis=0, keepdims=True)

    carry = carry_ref[...]
    lane = lax.broadcasted_iota(jnp.int32, (N_EXPERTS, LANES), 1)
    seg_ref[...] = jnp.where(lane == 0, groups, jnp.where(lane == 1, first_group,
                                                          jnp.where(lane == 2, carry, 0.0)))
    carry_ref[...] = carry + groups
    counts_ref[...] = carry + groups

    n_local = xs_ref.shape[0]
    chunk = n_local // PERM_CHUNKS
    ipos_a, ipos_b = pos_a.astype(jnp.int32), pos_b.astype(jnp.int32)
    c_a, c_b = g_w * w1, g_w * w2
    for k in range(PERM_CHUNKS):
        rows = slice(k * chunk, (k + 1) * chunk)
        riota = lax.broadcasted_iota(jnp.int32, (chunk, ts), 0) + k * chunk
        sel_a = riota == ipos_a
        sel_b = riota == ipos_b
        perm = jnp.where(sel_a, 1.0, jnp.where(sel_b, 1.0, 0.0)).astype(BF16)
        xs_ref[rows, :half] = _pack_halves(_dot(perm, n3b))
        w_sorted = jnp.sum(jnp.where(sel_a, c_a, jnp.where(sel_b, c_b, 0.0)),
                           axis=1, keepdims=True)
        xs_ref[rows, half:] = pltpu.bitcast(jnp.broadcast_to(w_sorted, (chunk, LANES)), jnp.uint32)

    res = jnp.concatenate([pos_a, pos_b, jnp.zeros((LANES - 2, ts), F32)], axis=0)
    rinfo_ref[...] = res.T


def _xattn_router(h1, gx, wq, kt, v, wo, gm, wr, br, upper, lower, n_rows):
    bsz, seq, d = h1.shape
    ts = SEQ_TILE
    m = v.shape[1]
    nt = seq // ts
    n_tiles = bsz * nt
    zero_rows = n_rows // n_tiles
    assert zero_rows * n_tiles == n_rows and zero_rows % SUBLANES == 0
    tok_spec = pl.BlockSpec((1, ts, d), lambda b, s: (b, s, 0))
    return pl.pallas_call(
        _xattn_router_kernel,
        grid=(bsz, nt),
        in_specs=[tok_spec, _const_spec(gx.shape), _const_spec(wq.shape),
                  pl.BlockSpec((1, d, m), lambda b, s: (b, 0, 0)),
                  pl.BlockSpec((1, m, d), lambda b, s: (b, 0, 0)),
                  _const_spec(wo.shape), _const_spec(gm.shape), _const_spec(wr.shape), _const_spec(br.shape),
                  _const_spec(upper.shape), _const_spec(lower.shape)],
        out_specs=[tok_spec,
                   pl.BlockSpec((LOCAL_ROWS, ROW_WORDS), lambda b, s: (b * nt + s, 0)),
                   pl.BlockSpec((ts, LANES), lambda b, s: (b * nt + s, 0)),
                   pl.BlockSpec((N_EXPERTS, LANES), lambda b, s: (b * nt + s, 0)),
                   pl.BlockSpec((N_EXPERTS, LANES), lambda b, s: (0, 0)),
                   pl.BlockSpec((zero_rows, ROW_WORDS), lambda b, s: (b * nt + s, 0))],
        out_shape=[jax.ShapeDtypeStruct((bsz, seq, d), F32),
                   jax.ShapeDtypeStruct((n_tiles * LOCAL_ROWS, ROW_WORDS), jnp.uint32),
                   jax.ShapeDtypeStruct((bsz * seq, LANES), F32),
                   jax.ShapeDtypeStruct((n_tiles * N_EXPERTS, LANES), F32),
                   jax.ShapeDtypeStruct((N_EXPERTS, LANES), F32),
                   jax.ShapeDtypeStruct((n_rows, ROW_WORDS), jnp.uint32)],
        scratch_shapes=[pltpu.VMEM((N_EXPERTS, LANES), F32)],
        compiler_params=pltpu.CompilerParams(
            dimension_semantics=("arbitrary", "arbitrary"), vmem_limit_bytes=VMEM_LIMIT_BYTES),
        name="xattn_router",
    )(h1, gx, wq, kt, v, wo, gm, wr, br, upper, lower)


TAB_COUNT, TAB_DST, TAB_LOCAL = (k * N_EXPERTS for k in range(3))


SEGMENT_RUN_LOG2 = 3
SEGMENT_CHUNKS = (4, 2, 1)


def _rows(first_group, n_groups):
    return pl.ds(pl.multiple_of(first_group * ROW_GROUP, ROW_GROUP), n_groups * ROW_GROUP)


def _for_each_segment_copy(tab_ref, make_copy, action):
    run = 1 << SEGMENT_RUN_LOG2

    def expert(e, c):
        n = tab_ref[0, 0, TAB_COUNT + e]
        dst0 = tab_ref[0, 0, TAB_DST + e]
        loc0 = tab_ref[0, 0, TAB_LOCAL + e]

        def full_run(k, c2):
            action(make_copy(loc0 + k * run, dst0 + k * run, run))
            return c2
        lax.fori_loop(0, lax.shift_right_logical(n, SEGMENT_RUN_LOG2), full_run, 0)
        for size in SEGMENT_CHUNKS:
            off = n & (-2 * size)

            @pl.when((n & size) != 0)
            def _():
                action(make_copy(loc0 + off, dst0 + off, size))
        return c
    lax.fori_loop(0, N_EXPERTS, expert, 0)


def _dispatch_kernel(tab_ref, xs_ref, buf_in_ref, buf_ref, sem):
    del buf_in_ref

    def make_copy(local, dst, n_groups):
        return pltpu.make_async_copy(xs_ref.at[_rows(local, n_groups)], buf_ref.at[_rows(dst, n_groups)], sem)

    _for_each_segment_copy(tab_ref, make_copy, lambda cp: cp.start())
    _for_each_segment_copy(tab_ref, make_copy, lambda cp: cp.wait())


def _dispatch(table, xs, buf0):
    n_tiles = table.shape[0]
    return pl.pallas_call(
        _dispatch_kernel,
        grid=(n_tiles,),
        in_specs=[pl.BlockSpec((1, 1, table.shape[2]), lambda i: (i, 0, 0), memory_space=pltpu.SMEM),
                  pl.BlockSpec((LOCAL_ROWS, xs.shape[1]), lambda i: (i, 0)),
                  pl.BlockSpec(memory_space=pl.ANY)],
        out_specs=pl.BlockSpec(memory_space=pl.ANY),
        out_shape=jax.ShapeDtypeStruct(buf0.shape, buf0.dtype),
        scratch_shapes=[pltpu.SemaphoreType.DMA],
        input_output_aliases={2: 0},
        compiler_params=pltpu.CompilerParams(
            dimension_semantics=("arbitrary",), vmem_limit_bytes=VMEM_LIMIT_BYTES),
        name="dispatch",
    )(table, xs, buf0)


def _expert_kernel(be_ref, nused_ref, xb_ref, *refs):
    del be_ref
    weight_refs, y_ref = refs[:-1], refs[-1]
    i = pl.program_id(0)
    bm = xb_ref.shape[0] // BLOCKS_PER_STEP
    half = xb_ref.shape[1] - LANES

    @pl.when(BLOCKS_PER_STEP * i < nused_ref[0])
    def _():
        for k in range(BLOCKS_PER_STEP):
            wgu_ref, wd_ref = weight_refs[2 * k], weight_refs[2 * k + 1]
            rows = slice(k * bm, (k + 1) * bm)
            x_lo, x_hi = _unpack_halves(xb_ref[rows, :half])
            gu = _dot(x_lo, wgu_ref[0, :half, :]) + _dot(x_hi, wgu_ref[0, half:, :])
            mid = (_silu(gu[:, :D_EXPERT]) * gu[:, D_EXPERT:]).astype(BF16)
            comb = pltpu.bitcast(xb_ref[rows, half:half + LANES], F32)[:, 0:1]
            y = _dot(mid, wd_ref[0]) * comb
            y_ref[rows, :] = _pack_halves(y.astype(BF16).astype(F32))

    @pl.when(BLOCKS_PER_STEP * i >= nused_ref[0])
    def _():
        y_ref[...] = jnp.zeros(y_ref.shape, y_ref.dtype)


def _experts(block_expert, n_used, xbuf, wgu, wd):
    w = xbuf.shape[1]
    bm = EXPERT_BLOCK
    n_blocks = block_expert.shape[0]
    assert n_blocks % BLOCKS_PER_STEP == 0
    n_rows = n_blocks * bm
    d = wd.shape[2]

    def weights_of(k):
        return lambda i, be, nu: (be[BLOCKS_PER_STEP * i + k], 0, 0)
    weight_specs = []
    for k in range(BLOCKS_PER_STEP):
        weight_specs += [pl.BlockSpec((1,) + wgu.shape[1:], weights_of(k)),
                         pl.BlockSpec((1,) + wd.shape[1:], weights_of(k))]
    grid_spec = pltpu.PrefetchScalarGridSpec(
        num_scalar_prefetch=2,
        grid=(n_blocks // BLOCKS_PER_STEP,),
        in_specs=[pl.BlockSpec((BLOCKS_PER_STEP * bm, w), lambda i, be, nu: (i, 0))] + weight_specs,
        out_specs=pl.BlockSpec((BLOCKS_PER_STEP * bm, d // 2), lambda i, be, nu: (i, 0)),
    )
    return pl.pallas_call(
        _expert_kernel,
        grid_spec=grid_spec,
        out_shape=jax.ShapeDtypeStruct((n_rows, d // 2), jnp.uint32),
        compiler_params=pltpu.CompilerParams(
            dimension_semantics=("arbitrary",), vmem_limit_bytes=VMEM_LIMIT_BYTES),
        name="experts",
    )(block_expert, n_used, xbuf, *([wgu, wd] * BLOCKS_PER_STEP))


def _combine_kernel(tab_ref, tab_next_ref, h2_ref, rinfo_ref, gf_ref, ybuf_ref, out_ref, ys_ref, sem):
    tr = h2_ref.shape[0]
    n_local = ys_ref.shape[1]
    i = pl.program_id(0)
    slot = i % 2

    def copier(s):
        def make_copy(local, src, n_groups):
            return pltpu.make_async_copy(ybuf_ref.at[_rows(src, n_groups)],
                                         ys_ref.at[s, _rows(local, n_groups)], sem.at[s])
        return make_copy

    def gather(tab, s):
        covered = tab[0, 0, TAB_LOCAL + N_EXPERTS - 1] + tab[0, 0, TAB_COUNT + N_EXPERTS - 1]

        def zero(g, c):
            ys_ref[s, _rows(g, 1), :] = jnp.zeros((ROW_GROUP, ys_ref.shape[2]), ys_ref.dtype)
            return c
        lax.fori_loop(covered, n_local // ROW_GROUP, zero, 0)
        _for_each_segment_copy(tab, copier(s), lambda cp: cp.start())

    @pl.when(i == 0)
    def _():
        gather(tab_ref, 0)

    @pl.when(i + 1 < pl.num_programs(0))
    def _():
        gather(tab_next_ref, 1 - slot)

    _for_each_segment_copy(tab_ref, copier(slot), lambda cp: cp.wait())

    info = rinfo_ref[...]
    ciota = lax.broadcasted_iota(jnp.int32, (tr, n_local), 1)
    pick = jnp.where(ciota == info[:, 0:1].astype(jnp.int32), 1.0,
                     jnp.where(ciota == info[:, 1:2].astype(jnp.int32), 1.0, 0.0)).astype(BF16)
    y_lo, y_hi = _unpack_halves(ys_ref[slot])
    h = h2_ref[...] + jnp.concatenate([_dot(pick, y_lo), _dot(pick, y_hi)], axis=-1)
    out_ref[...] = _rms(h, gf_ref[...])


def _combine(table, h2, rinfo, gf, ybuf):
    t, d = h2.shape
    tr = SEQ_TILE
    n_tiles = t // tr
    tab_spec = (1, 1, table.shape[2])
    return pl.pallas_call(
        _combine_kernel,
        grid=(n_tiles,),
        in_specs=[pl.BlockSpec(tab_spec, lambda i: (i, 0, 0), memory_space=pltpu.SMEM),
                  pl.BlockSpec(tab_spec, lambda i: (jnp.minimum(i + 1, n_tiles - 1), 0, 0),
                               memory_space=pltpu.SMEM),
                  pl.BlockSpec((tr, d), lambda i: (i, 0)),
                  pl.BlockSpec((tr, LANES), lambda i: (i, 0)),
                  _const_spec(gf.shape),
                  pl.BlockSpec(memory_space=pl.ANY)],
        out_specs=pl.BlockSpec((tr, d), lambda i: (i, 0)),
        out_shape=jax.ShapeDtypeStruct((t, d), F32),
        scratch_shapes=[pltpu.VMEM((2, LOCAL_ROWS, d // 2), jnp.uint32), pltpu.SemaphoreType.DMA((2,))],
        compiler_params=pltpu.CompilerParams(
            dimension_semantics=("arbitrary",), vmem_limit_bytes=VMEM_LIMIT_BYTES),
        name="combine",
    )(table, table, h2, rinfo, gf, ybuf)


def _pad_lanes(v):
    return jnp.pad(v, (0, LANES - v.shape[0]))[None, :]


def kernel(x, mem, norm_mem, norm_mix, w_in, conv_ssd_w, conv_ssd_b, dt_bias, a_log, d_skip, norm_ssd_gate,
           conv_short_w, w_out, norm_xattn, w_q, w_kv, w_o, norm_moe, w_router_group, b_router_group,
           w_router_expert, b_router_expert, w_gate, w_up, w_down, norm_final):
    bsz, seq, d = x.shape
    t = bsz * seq
    assert w_in.shape[0] == 1, "single-layer model"
    assert d == D_MODEL and seq % SEQ_TILE == 0

    tri = jnp.tril(jnp.ones((SSD_CHUNK, SSD_CHUNK), F32))
    upper = jnp.triu(jnp.ones((SEQ_TILE, SEQ_TILE), F32), k=1).astype(BF16)

    o2 = SSD_WIDTH + XBC_WIDTH
    o3 = o2 + SSD_HEADS
    wl = w_in[0]
    win = jnp.concatenate(
        [wl[:, :o2], wl[:, o3:], wl[:, o2:o3], jnp.zeros((d, LANES - SSD_HEADS), F32)], axis=1).astype(BF16)
    h1, wgu, wdb = _mixer(x, norm_mix, win, conv_ssd_w[0], conv_ssd_b, _pad_lanes(dt_bias[0]),
                _pad_lanes(-jnp.exp(a_log[0])), jnp.repeat(d_skip[0], SSD_HEAD_DIM)[None, :], norm_ssd_gate,
                conv_short_w[0], w_out[0].astype(BF16), tri, w_gate[0], w_up[0], w_down[0])

    kt, v = _memory_kv(mem, norm_mem[None, :], w_kv[0].astype(BF16))
    wr = jnp.zeros((d, LANES), F32)
    wr = wr.at[:, ROUTER_GROUP_ROW:ROUTER_GROUP_ROW + N_GROUPS_MOE].set(w_router_group[0])
    wr = wr.at[:, ROUTER_EXPERT_ROW:ROUTER_EXPERT_ROW + N_EXPERTS].set(w_router_expert[0])
    wr_hi = wr.astype(BF16)
    wr_lo = (wr - wr_hi.astype(F32)).astype(BF16)
    br = jnp.zeros((LANES,), F32)
    br = br.at[ROUTER_GROUP_ROW:ROUTER_GROUP_ROW + N_GROUPS_MOE].set(b_router_group[0])
    br = br.at[ROUTER_EXPERT_ROW:ROUTER_EXPERT_ROW + N_EXPERTS].set(b_router_expert[0])
    lower = jnp.tril(jnp.ones((N_EXPERTS, N_EXPERTS), F32), k=-1).astype(BF16)
    n_tiles = t // SEQ_TILE
    bgroups = EXPERT_BLOCK // ROW_GROUP
    max_groups = (2 * t + n_tiles * N_EXPERTS * (ROW_GROUP - 1)) // ROW_GROUP
    n_blocks = -(-max_groups // bgroups) + N_EXPERTS
    even = math.lcm((n_tiles * SUBLANES) // math.gcd(n_tiles * SUBLANES, EXPERT_BLOCK), BLOCKS_PER_STEP)
    n_blocks = -(-n_blocks // even) * even
    h2, xs, rinfo, seg, counts, xbuf0 = _xattn_router(
        h1, norm_xattn, w_q[0].astype(BF16), kt, v, w_o[0].astype(BF16), norm_moe,
        jnp.concatenate([wr_hi, wr_lo], axis=1), br[None, :], upper, lower, n_blocks * EXPERT_BLOCK)

    seg = seg.reshape(n_tiles, N_EXPERTS, LANES)[:, :, 0:3].astype(jnp.int32)
    seg_groups, seg_local, seg_before = seg[:, :, 0], seg[:, :, 1], seg[:, :, 2]
    sizes = counts[:, 0].astype(jnp.int32)
    padded = ((sizes + bgroups - 1) // bgroups) * bgroups
    pad_end = jnp.cumsum(padded)
    pad_start = pad_end - padded
    seg_dst = pad_start[None, :] + seg_before
    table = jnp.concatenate([seg_groups, seg_dst, seg_local], axis=1)[:, None, :]
    block_group = jnp.arange(n_blocks, dtype=jnp.int32) * bgroups
    block_expert = jnp.minimum(jnp.sum((pad_end[None, :] <= block_group[:, None]).astype(jnp.int32), axis=-1),
                               N_EXPERTS - 1)
    n_used = (pad_end[-1:] // bgroups).astype(jnp.int32)

    xbuf = _dispatch(table, xs, xbuf0)
    ybuf = _experts(block_expert, n_used, xbuf, wgu, wdb)
    out = _combine(table, h2.reshape(t, d), rinfo, norm_final[None, :], ybuf)
    return out.reshape(bsz, seq, d)
```

```python
import math

import jax
import jax.numpy as jnp
from jax import lax
from jax.experimental import pallas as pl
from jax.experimental.pallas import tpu as pltpu

F32 = jnp.float32
BF16 = jnp.bfloat16
EPS = 1e-6

LANES = 128
SUBLANES = 8
VMEM_LIMIT_BYTES = 56 * 1024 * 1024

D_MODEL = 1024
SSD_WIDTH = 512
SSD_HEAD_DIM = 64
SSD_HEADS = 8
SSD_GROUPS = 2
SSD_HEADS_PER_GROUP = 4
SSD_STATE = 64
SSD_CHUNK = 128
XBC_WIDTH = SSD_WIDTH + 2 * SSD_GROUPS * SSD_STATE
SC_WIDTH = 512
XA_HEADS = 4
XA_HEAD_DIM = 256
N_GROUPS_MOE = 4
EXPERTS_PER_GROUP = 8
N_EXPERTS = 32
D_EXPERT = 512

COL_Z = 0
COL_XBC = COL_Z + SSD_WIDTH
COL_GB = COL_XBC + XBC_WIDTH
COL_GC = COL_GB + SC_WIDTH
COL_V = COL_GC + SC_WIDTH
COL_DT = COL_V + SC_WIDTH
PROJ_WIDTH = COL_DT + LANES

SEQ_TILE = 512
EXPERT_BLOCK = 512
BLOCKS_PER_STEP = 4
HALO = SUBLANES
ROW_GROUP = SUBLANES
LOCAL_ROWS = 2 * SEQ_TILE + N_EXPERTS * ROW_GROUP
ROW_WORDS = D_MODEL // 2 + LANES
PERM_CHUNKS = 4
TILES_PER_STEP = 2

ROUTER_GROUP_ROW = 0
ROUTER_EXPERT_ROW = SUBLANES


def _rms(x, g):
    return x * lax.rsqrt(jnp.mean(x * x, axis=-1, keepdims=True) + EPS) * g


def _silu(x):
    return x * jax.nn.sigmoid(x)


def _dot(a, b):
    return jnp.dot(a, b, preferred_element_type=F32)


def _pack_halves(v):
    w = v.shape[1] // 2
    lo = lax.shift_right_logical(pltpu.bitcast(v[:, :w], jnp.uint32), jnp.uint32(16))
    hi = pltpu.bitcast(v[:, w:], jnp.uint32) & jnp.uint32(0xFFFF0000)
    return lo | hi


def _unpack_halves(words):
    lo = pltpu.bitcast(lax.shift_left(words, jnp.uint32(16)), F32).astype(BF16)
    hi = pltpu.bitcast(words & jnp.uint32(0xFFFF0000), F32).astype(BF16)
    return lo, hi


def _mixer_kernel(x_ref, gmix_ref, win_ref, cw_ref, cb_ref, dtb_ref, a_ref, dskip_ref, gate_ref,
                  scw_ref, wout_ref, tri_ref, wg_ref, wu_ref, wd_ref, h_ref, wgu_ref, wdb_ref,
                  conv_ref, cv_ref, act_ref, dt_ref, y_ref, state_ref, sinc_ref, loff_ref, elast_ref):
    ts = x_ref.shape[1]
    wgu_ref[0, :, :D_EXPERT] = wg_ref[0].astype(BF16)
    wgu_ref[0, :, D_EXPERT:] = wu_ref[0].astype(BF16)
    wdb_ref[0] = wd_ref[0].astype(BF16)
    s = pl.program_id(1)
    x = x_ref[0]
    nb = _rms(x, gmix_ref[...]).astype(BF16)

    @pl.when(s == 0)
    def _():
        conv_ref[0:HALO, :] = jnp.zeros((HALO, XBC_WIDTH), F32)
        cv_ref[0:HALO, :] = jnp.zeros((HALO, SC_WIDTH), F32)
        state_ref[...] = jnp.zeros(state_ref.shape, F32)

    @pl.when(s > 0)
    def _():
        conv_ref[0:HALO, :] = conv_ref[ts:ts + HALO, :]
        cv_ref[0:HALO, :] = cv_ref[ts:ts + HALO, :]

    xbc = _dot(nb, win_ref[:, COL_XBC:COL_XBC + XBC_WIDTH])
    conv_ref[HALO:HALO + ts, :] = xbc
    acc = cb_ref[...] + cw_ref[3:4, :] * xbc
    for i in range(3):
        acc = acc + cw_ref[i:i + 1, :] * conv_ref[HALO - 3 + i:HALO - 3 + i + ts, :]
    act_ref[...] = _silu(acc)

    dtr = _dot(nb, win_ref[:, COL_DT:COL_DT + LANES]) + dtb_ref[...]
    dt_ref[...] = jnp.maximum(dtr, 0.0) + jnp.log1p(jnp.exp(-jnp.abs(dtr)))

    row = lax.broadcasted_iota(jnp.int32, (SSD_CHUNK, SSD_CHUNK), 0)
    col = lax.broadcasted_iota(jnp.int32, (SSD_CHUNK, SSD_CHUNK), 1)
    causal = row >= col
    lo_q = col < SSD_HEAD_DIM
    lo_n = lax.broadcasted_iota(jnp.int32, (SSD_STATE, SSD_CHUNK), 1) < SSD_HEAD_DIM
    lo_e = lax.broadcasted_iota(jnp.int32, (SUBLANES, SSD_CHUNK), 1) < SSD_HEAD_DIM

    def chunk_prologue(c):
        rows = slice(c * SSD_CHUNK, (c + 1) * SSD_CHUNK)
        dt = dt_ref[rows, :]
        adt = dt * a_ref[...]
        cs = jnp.dot(tri_ref[...], adt, preferred_element_type=F32,
                     precision=lax.Precision.HIGHEST)
        bm = act_ref[rows, SSD_WIDTH:SSD_WIDTH + SSD_GROUPS * SSD_STATE]
        cm = act_ref[rows, SSD_WIDTH + SSD_GROUPS * SSD_STATE:XBC_WIDTH]
        cbs = []
        for g in range(SSD_GROUPS):
            cg = cm[:, g * SSD_STATE:(g + 1) * SSD_STATE]
            bg = bm[:, g * SSD_STATE:(g + 1) * SSD_STATE]
            cbs.append(lax.dot_general(cg.astype(BF16), bg.astype(BF16), (((1,), (1,)), ((), ())),
                                       preferred_element_type=F32))
        return cs, cs.T, dt.T, bm.T, cm, pltpu.roll(cm, SSD_STATE, axis=1), cbs

    def chunk_body(c, prologue):
        rows = slice(c * SSD_CHUNK, (c + 1) * SSD_CHUNK)
        cs, cs_t, dt_t, bm_t, cm, cm_sw, cbs = prologue
        for g in range(SSD_GROUPS):
            bg_t = bm_t[g * SSD_STATE:(g + 1) * SSD_STATE, :]
            cb = cbs[g]
            c2 = jnp.where(lo_q, cm, cm_sw) if g == 0 else jnp.where(lo_q, cm_sw, cm)
            for jj in range(SSD_HEADS_PER_GROUP // 2):
                j = g * (SSD_HEADS_PER_GROUP // 2) + jj
                idx = c * (SSD_HEADS // 2) + j
                xp = act_ref[rows, j * LANES:(j + 1) * LANES]
                rhs = jnp.concatenate(
                    [jnp.where(lo_q, xp, 0.0), jnp.where(lo_q, 0.0, xp)], axis=0).astype(BF16)
                diag, off, w_rows, e_last = [], [], [], []
                for h in (2 * j, 2 * j + 1):
                    cs_col = jnp.broadcast_to(cs[:, h:h + 1], (SSD_CHUNK, SSD_CHUNK))
                    cs_row = cs_t[h:h + 1, :]
                    dt_row = dt_t[h:h + 1, :]
                    decay = jnp.exp(jnp.where(causal, cs_col - cs_row, -jnp.inf))
                    diag.append((cb * dt_row) * decay)
                    off.append(jnp.exp(cs_col))
                    cl = cs_t[h:h + 1, SSD_CHUNK - 1:SSD_CHUNK]
                    w_rows.append(jnp.exp(cl - cs_row) * dt_row)
                    e_last.append(jnp.exp(cl))
                y_ref[rows, j * LANES:(j + 1) * LANES] = _dot(jnp.concatenate(diag, axis=1).astype(BF16), rhs)
                lhs2 = jnp.concatenate([bg_t * w_rows[0], bg_t * w_rows[1]], axis=1).astype(BF16)
                sinc_ref[idx] = _dot(lhs2, rhs)
                loff_ref[idx] = (c2 * jnp.where(lo_q, off[0], off[1])).astype(BF16)
                elast_ref[idx] = jnp.where(lo_e, e_last[0], e_last[1])

    n_chunks = ts // SSD_CHUNK
    prologues = [chunk_prologue(c) for c in range(n_chunks)]

    g_c = _dot(nb, win_ref[:, COL_GC:COL_GC + SC_WIDTH])
    v = _dot(nb, win_ref[:, COL_V:COL_V + SC_WIDTH])
    u = g_c * v
    cv_ref[HALO:HALO + ts, :] = u
    conv3 = scw_ref[2:3, :] * u
    for i in range(2):
        conv3 = conv3 + scw_ref[i:i + 1, :] * cv_ref[HALO - 2 + i:HALO - 2 + i + ts, :]
    g_b = _dot(nb, win_ref[:, COL_GB:COL_GB + SC_WIDTH])
    out_sc = _dot((g_b * conv3).astype(BF16), wout_ref[SSD_WIDTH:, :])
    gate_z = _silu(_dot(nb, win_ref[:, COL_Z:COL_Z + SSD_WIDTH]))

    for c in range(n_chunks):
        chunk_body(c, prologues[c])

    for c in range(n_chunks):
        for j in range(SSD_HEADS // 2):
            idx = c * (SSD_HEADS // 2) + j
            stp = state_ref[j]
            rhs_st = jnp.concatenate(
                [jnp.where(lo_n, stp, 0.0), jnp.where(lo_n, 0.0, stp)], axis=0).astype(BF16)
            y_ref[c * SSD_CHUNK:(c + 1) * SSD_CHUNK, j * LANES:(j + 1) * LANES] += _dot(loff_ref[idx], rhs_st)
            state_ref[j] = elast_ref[idx][0:1, :] * stp + sinc_ref[idx]

    y = y_ref[...] + act_ref[:, 0:SSD_WIDTH] * dskip_ref[...]
    gated = y * gate_z
    gw = SSD_WIDTH // SSD_GROUPS
    y_ssd = jnp.concatenate(
        [_rms(gated[:, g * gw:(g + 1) * gw], gate_ref[:, g * gw:(g + 1) * gw]) for g in range(SSD_GROUPS)],
        axis=-1).astype(BF16)
    h_ref[0] = x + out_sc + _dot(y_ssd, wout_ref[0:SSD_WIDTH, :])


def _const_spec(shape):
    nd = len(shape)
    return pl.BlockSpec(shape, lambda *_: (0,) * nd)


def _mixer(x, gmix, win, cw, cb, dtb, a_row, dskip, gate, scw, wout, tri, w_gate, w_up, w_down):
    bsz, seq, d = x.shape
    ts = SEQ_TILE
    nt = seq // ts
    consts = (gmix, win, cw, cb, dtb, a_row, dskip, gate, scw, wout, tri)
    n_exp, _, de = w_gate.shape
    per_expert = (bsz * nt) // n_exp
    assert per_expert * n_exp == bsz * nt and d % per_expert == 0 and de % per_expert == 0
    rg, rd = d // per_expert, de // per_expert

    def slice_map(b, s):
        i = b * nt + s
        return (i // per_expert, i % per_expert, 0)
    return pl.pallas_call(
        _mixer_kernel,
        grid=(bsz, nt),
        in_specs=([pl.BlockSpec((1, ts, d), lambda b, s: (b, s, 0))] + [_const_spec(c.shape) for c in consts]
                  + [pl.BlockSpec((1, rg, de), slice_map), pl.BlockSpec((1, rg, de), slice_map),
                     pl.BlockSpec((1, rd, d), slice_map)]),
        out_specs=[pl.BlockSpec((1, ts, d), lambda b, s: (b, s, 0)),
                   pl.BlockSpec((1, rg, 2 * de), slice_map), pl.BlockSpec((1, rd, d), slice_map)],
        out_shape=[jax.ShapeDtypeStruct((bsz, seq, d), F32),
                   jax.ShapeDtypeStruct((n_exp, d, 2 * de), BF16), jax.ShapeDtypeStruct((n_exp, de, d), BF16)],
        scratch_shapes=[
            pltpu.VMEM((ts + HALO, XBC_WIDTH), F32),
            pltpu.VMEM((ts + HALO, SC_WIDTH), F32),
            pltpu.VMEM((ts, XBC_WIDTH), F32),
            pltpu.VMEM((ts, LANES), F32),
            pltpu.VMEM((ts, SSD_WIDTH), F32),
            pltpu.VMEM((SSD_HEADS // 2, SSD_STATE, 2 * SSD_HEAD_DIM), F32),
            pltpu.VMEM((ts // SSD_CHUNK * SSD_HEADS // 2, SSD_STATE, 2 * SSD_HEAD_DIM), F32),
            pltpu.VMEM((ts // SSD_CHUNK * SSD_HEADS // 2, SSD_CHUNK, 2 * SSD_STATE), BF16),
            pltpu.VMEM((ts // SSD_CHUNK * SSD_HEADS // 2, SUBLANES, 2 * SSD_HEAD_DIM), F32),
        ],
        compiler_params=pltpu.CompilerParams(
            dimension_semantics=("arbitrary", "arbitrary"), vmem_limit_bytes=VMEM_LIMIT_BYTES),
        name="mixer",
    )(x, *consts, w_gate, w_up, w_down)


def _kv_kernel(mem_ref, g_ref, wkv_ref, kt_ref, v_ref):
    d = mem_ref.shape[2]
    mn = _rms(mem_ref[0], g_ref[...]).astype(BF16)
    kv = _dot(mn, wkv_ref[...])
    kt_ref[0] = (kv[:, :d] * (XA_HEAD_DIM ** -0.5)).T.astype(BF16)
    v_ref[0] = kv[:, d:].astype(BF16)


def _memory_kv(mem, g, wkv):
    bsz, m, d = mem.shape
    return pl.pallas_call(
        _kv_kernel,
        grid=(bsz,),
        in_specs=[pl.BlockSpec((1, m, d), lambda b: (b, 0, 0)), _const_spec(g.shape), _const_spec(wkv.shape)],
        out_specs=[pl.BlockSpec((1, d, m), lambda b: (b, 0, 0)), pl.BlockSpec((1, m, d), lambda b: (b, 0, 0))],
        out_shape=[jax.ShapeDtypeStruct((bsz, d, m), BF16), jax.ShapeDtypeStruct((bsz, m, d), BF16)],
        compiler_params=pltpu.CompilerParams(
            dimension_semantics=("arbitrary",), vmem_limit_bytes=VMEM_LIMIT_BYTES),
        name="memory_kv",
    )(mem, g, wkv)


def _xattn_router_kernel(h1_ref, gx_ref, wq_ref, kt_ref, v_ref, wo_ref, gm_ref, wr_ref, br_ref, upper_ref,
                         lower_ref, h2_ref, xs_ref, rinfo_ref, seg_ref, counts_ref, zero_ref, carry_ref):
    ts = h1_ref.shape[1]
    first = jnp.logical_and(pl.program_id(0) == 0, pl.program_id(1) == 0)

    @pl.when(first)
    def _():
        carry_ref[...] = jnp.zeros(carry_ref.shape, F32)

    h1 = h1_ref[0]
    q = _dot(_rms(h1, gx_ref[...]).astype(BF16), wq_ref[...]).astype(BF16)
    outs = []
    for h in range(XA_HEADS):
        hs = slice(h * XA_HEAD_DIM, (h + 1) * XA_HEAD_DIM)
        sc = _dot(q[:, hs], kt_ref[0, hs, :])
        p = jnp.exp(sc - jnp.max(sc, axis=-1, keepdims=True))
        o = _dot(p.astype(BF16), v_ref[0, :, hs]) / jnp.sum(p, axis=-1, keepdims=True)
        outs.append(o.astype(BF16))
    h2 = h1 + _dot(jnp.concatenate(outs, axis=-1), wo_ref[...])
    h2_ref[0] = h2

    n3 = _rms(h2, gm_ref[...])
    n3b = n3.astype(BF16)
    half = n3.shape[1] // 2
    n3r = n3b.astype(F32)
    zero_ref[...] = jnp.zeros(zero_ref.shape, zero_ref.dtype)

    n3_lo = (n3 - n3r).astype(BF16)
    lg2 = _dot(n3b, wr_ref[...])
    logits = lg2[:, :LANES] + lg2[:, LANES:] + _dot(n3_lo, wr_ref[:, :LANES]) + br_ref[...]
    lt = logits.T

    l0, l1, l2, l3 = (lt[ROUTER_GROUP_ROW + i:ROUTER_GROUP_ROW + i + 1, :] for i in range(N_GROUPS_MOE))
    m = jnp.maximum(jnp.maximum(l0, l1), jnp.maximum(l2, l3))
    gidx = jnp.where(l0 == m, 0, jnp.where(l1 == m, 1, jnp.where(l2 == m, 2, 3)))
    g_w = 1.0 / (jnp.exp(l0 - m) + jnp.exp(l1 - m) + jnp.exp(l2 - m) + jnp.exp(l3 - m))

    def group_rows(g):
        r = ROUTER_EXPERT_ROW + g * EXPERTS_PER_GROUP
        return lt[r:r + EXPERTS_PER_GROUP, :]

    el = jnp.where(gidx == 0, group_rows(0),
                   jnp.where(gidx == 1, group_rows(1), jnp.where(gidx == 2, group_rows(2), group_rows(3))))
    sub = lax.broadcasted_iota(jnp.int32, el.shape, 0)
    m1 = jnp.max(el, axis=0, keepdims=True)
    i1 = jnp.min(jnp.where(el == m1, sub, EXPERTS_PER_GROUP), axis=0, keepdims=True)
    el2 = jnp.where(sub == i1, -jnp.inf, el)
    m2 = jnp.max(el2, axis=0, keepdims=True)
    i2 = jnp.min(jnp.where(el2 == m2, sub, EXPERTS_PER_GROUP), axis=0, keepdims=True)
    e2 = jnp.exp(m2 - m1)
    w1 = 1.0 / (1.0 + e2)
    w2 = e2 / (1.0 + e2)
    e_a = gidx * EXPERTS_PER_GROUP + i1
    e_b = gidx * EXPERTS_PER_GROUP + i2

    eio = lax.broadcasted_iota(jnp.int32, (N_EXPERTS, ts), 0)
    oh_a = (eio == e_a).astype(F32)
    oh_b = (eio == e_b).astype(F32)
    prefix = _dot(jnp.concatenate([oh_a, oh_b], axis=0).astype(BF16), upper_ref[...])
    tot_a = jnp.sum(oh_a, axis=1, keepdims=True)
    tot_b = jnp.sum(oh_b, axis=1, keepdims=True)
    groups = jnp.broadcast_to(jnp.floor((tot_a + tot_b + (ROW_GROUP - 1)) * (1.0 / ROW_GROUP)),
                              (N_EXPERTS, LANES))
    first_group = _dot(lower_ref[...], groups.astype(BF16))
    base = ROW_GROUP * first_group[:, 0:1]
    pos_a = jnp.sum(oh_a * (base + prefix[:N_EXPERTS]), axis=0, keepdims=True)
    pos_b = jnp.sum(oh_b * (base + tot_a + prefix[N_EXPERTS:]), axis=0, keepdims=True)

    carry = carry_ref[...]
    lane = lax.broadcasted_iota(jnp.int32, (N_EXPERTS, LANES), 1)
    seg_ref[...] = jnp.where(lane == 0, groups, jnp.where(lane == 1, first_group,
                                                          jnp.where(lane == 2, carry, 0.0)))
    carry_ref[...] = carry + groups
    counts_ref[...] = carry + groups

    n_local = xs_ref.shape[0]
    chunk = n_local // PERM_CHUNKS
    ipos_a, ipos_b = pos_a.astype(jnp.int32), pos_b.astype(jnp.int32)
    c_a, c_b = g_w * w1, g_w * w2
    for k in range(PERM_CHUNKS):
        rows = slice(k * chunk, (k + 1) * chunk)
        riota = lax.broadcasted_iota(jnp.int32, (chunk, ts), 0) + k * chunk
        sel_a = riota == ipos_a
        sel_b = riota == ipos_b
        perm = jnp.where(sel_a, 1.0, jnp.where(sel_b, 1.0, 0.0)).astype(BF16)
        xs_ref[rows, :half] = _pack_halves(_dot(perm, n3b))
        w_sorted = jnp.sum(jnp.where(sel_a, c_a, jnp.where(sel_b, c_b, 0.0)),
                           axis=1, keepdims=True)
        xs_ref[rows, half:] = pltpu.bitcast(jnp.broadcast_to(w_sorted, (chunk, LANES)), jnp.uint32)

    res = jnp.concatenate([pos_a, pos_b, jnp.zeros((LANES - 2, ts), F32)], axis=0)
    rinfo_ref[...] = res.T


def _xattn_router(h1, gx, wq, kt, v, wo, gm, wr, br, upper, lower, n_rows):
    bsz, seq, d = h1.shape
    ts = SEQ_TILE
    m = v.shape[1]
    nt = seq // ts
    n_tiles = bsz * nt
    zero_rows = n_rows // n_tiles
    assert zero_rows * n_tiles == n_rows and zero_rows % SUBLANES == 0
    tok_spec = pl.BlockSpec((1, ts, d), lambda b, s: (b, s, 0))
    return pl.pallas_call(
        _xattn_router_kernel,
        grid=(bsz, nt),
        in_specs=[tok_spec, _const_spec(gx.shape), _const_spec(wq.shape),
                  pl.BlockSpec((1, d, m), lambda b, s: (b, 0, 0)),
                  pl.BlockSpec((1, m, d), lambda b, s: (b, 0, 0)),
                  _const_spec(wo.shape), _const_spec(gm.shape), _const_spec(wr.shape), _const_spec(br.shape),
                  _const_spec(upper.shape), _const_spec(lower.shape)],
        out_specs=[tok_spec,
                   pl.BlockSpec((LOCAL_ROWS, ROW_WORDS), lambda b, s: (b * nt + s, 0)),
                   pl.BlockSpec((ts, LANES), lambda b, s: (b * nt + s, 0)),
                   pl.BlockSpec((N_EXPERTS, LANES), lambda b, s: (b * nt + s, 0)),
                   pl.BlockSpec((N_EXPERTS, LANES), lambda b, s: (0, 0)),
                   pl.BlockSpec((zero_rows, ROW_WORDS), lambda b, s: (b * nt + s, 0))],
        out_shape=[jax.ShapeDtypeStruct((bsz, seq, d), F32),
                   jax.ShapeDtypeStruct((n_tiles * LOCAL_ROWS, ROW_WORDS), jnp.uint32),
                   jax.ShapeDtypeStruct((bsz * seq, LANES), F32),
                   jax.ShapeDtypeStruct((n_tiles * N_EXPERTS, LANES), F32),
                   jax.ShapeDtypeStruct((N_EXPERTS, LANES), F32),
                   jax.ShapeDtypeStruct((n_rows, ROW_WORDS), jnp.uint32)],
        scratch_shapes=[pltpu.VMEM((N_EXPERTS, LANES), F32)],
        compiler_params=pltpu.CompilerParams(
            dimension_semantics=("arbitrary", "arbitrary"), vmem_limit_bytes=VMEM_LIMIT_BYTES),
        name="xattn_router",
    )(h1, gx, wq, kt, v, wo, gm, wr, br, upper, lower)


TAB_COUNT, TAB_DST, TAB_LOCAL = (k * N_EXPERTS for k in range(3))


SEGMENT_RUN_LOG2 = 3
SEGMENT_CHUNKS = (4, 2, 1)


def _rows(first_group, n_groups):
    return pl.ds(pl.multiple_of(first_group * ROW_GROUP, ROW_GROUP), n_groups * ROW_GROUP)


def _for_each_segment_copy(entry, make_copy, action):
    run = 1 << SEGMENT_RUN_LOG2

    def expert(e, c):
        n = entry(TAB_COUNT + e)
        dst0 = entry(TAB_DST + e)
        loc0 = entry(TAB_LOCAL + e)

        def full_run(k, c2):
            action(make_copy(loc0 + k * run, dst0 + k * run, run))
            return c2
        lax.fori_loop(0, lax.shift_right_logical(n, SEGMENT_RUN_LOG2), full_run, 0)
        for size in SEGMENT_CHUNKS:
            off = n & (-2 * size)

            @pl.when((n & size) != 0)
            def _():
                action(make_copy(loc0 + off, dst0 + off, size))
        return c
    lax.fori_loop(0, N_EXPERTS, expert, 0)


def _dispatch_kernel(tab_ref, xs_ref, buf_in_ref, buf_ref, sem):
    del buf_in_ref

    def make_copy(local, dst, n_groups):
        return pltpu.make_async_copy(xs_ref.at[_rows(local, n_groups)], buf_ref.at[_rows(dst, n_groups)], sem)

    def entry(j):
        return tab_ref[0, 0, j]
    _for_each_segment_copy(entry, make_copy, lambda cp: cp.start())
    _for_each_segment_copy(entry, make_copy, lambda cp: cp.wait())


def _dispatch(table, xs, buf0):
    n_tiles = table.shape[0]
    return pl.pallas_call(
        _dispatch_kernel,
        grid=(n_tiles,),
        in_specs=[pl.BlockSpec((1, 1, table.shape[2]), lambda i: (i, 0, 0), memory_space=pltpu.SMEM),
                  pl.BlockSpec((LOCAL_ROWS, xs.shape[1]), lambda i: (i, 0)),
                  pl.BlockSpec(memory_space=pl.ANY)],
        out_specs=pl.BlockSpec(memory_space=pl.ANY),
        out_shape=jax.ShapeDtypeStruct(buf0.shape, buf0.dtype),
        scratch_shapes=[pltpu.SemaphoreType.DMA],
        input_output_aliases={2: 0},
        compiler_params=pltpu.CompilerParams(
            dimension_semantics=("arbitrary",), vmem_limit_bytes=VMEM_LIMIT_BYTES),
        name="dispatch",
    )(table, xs, buf0)


def _expert_kernel(be_ref, nused_ref, xb_ref, *refs):
    del be_ref
    weight_refs, y_ref = refs[:-1], refs[-1]
    i = pl.program_id(0)
    bm = xb_ref.shape[0] // BLOCKS_PER_STEP
    half = xb_ref.shape[1] - LANES

    @pl.when(BLOCKS_PER_STEP * i < nused_ref[0])
    def _():
        for k in range(BLOCKS_PER_STEP):
            wgu_ref, wd_ref = weight_refs[2 * k], weight_refs[2 * k + 1]
            rows = slice(k * bm, (k + 1) * bm)
            x_lo, x_hi = _unpack_halves(xb_ref[rows, :half])
            gu = _dot(x_lo, wgu_ref[0, :half, :]) + _dot(x_hi, wgu_ref[0, half:, :])
            mid = (_silu(gu[:, :D_EXPERT]) * gu[:, D_EXPERT:]).astype(BF16)
            comb = pltpu.bitcast(xb_ref[rows, half:half + LANES], F32)[:, 0:1]
            y = _dot(mid, wd_ref[0]) * comb
            y_ref[rows, :] = _pack_halves(y.astype(BF16).astype(F32))

    @pl.when(BLOCKS_PER_STEP * i >= nused_ref[0])
    def _():
        y_ref[...] = jnp.zeros(y_ref.shape, y_ref.dtype)


def _experts(block_expert, n_used, xbuf, wgu, wd):
    w = xbuf.shape[1]
    bm = EXPERT_BLOCK
    n_blocks = block_expert.shape[0]
    assert n_blocks % BLOCKS_PER_STEP == 0
    n_rows = n_blocks * bm
    d = wd.shape[2]

    def weights_of(k):
        return lambda i, be, nu: (be[BLOCKS_PER_STEP * i + k], 0, 0)
    weight_specs = []
    for k in range(BLOCKS_PER_STEP):
        weight_specs += [pl.BlockSpec((1,) + wgu.shape[1:], weights_of(k)),
                         pl.BlockSpec((1,) + wd.shape[1:], weights_of(k))]
    grid_spec = pltpu.PrefetchScalarGridSpec(
        num_scalar_prefetch=2,
        grid=(n_blocks // BLOCKS_PER_STEP,),
        in_specs=[pl.BlockSpec((BLOCKS_PER_STEP * bm, w), lambda i, be, nu: (i, 0))] + weight_specs,
        out_specs=pl.BlockSpec((BLOCKS_PER_STEP * bm, d // 2), lambda i, be, nu: (i, 0)),
    )
    return pl.pallas_call(
        _expert_kernel,
        grid_spec=grid_spec,
        out_shape=jax.ShapeDtypeStruct((n_rows, d // 2), jnp.uint32),
        compiler_params=pltpu.CompilerParams(
            dimension_semantics=("arbitrary",), vmem_limit_bytes=VMEM_LIMIT_BYTES),
        name="experts",
    )(block_expert, n_used, xbuf, *([wgu, wd] * BLOCKS_PER_STEP))


def _combine_kernel(tab_ref, tab_next_ref, h2_ref, rinfo_ref, gf_ref, ybuf_ref, out_ref, ys_ref, sem):
    tr = h2_ref.shape[0] // TILES_PER_STEP
    n_local = ys_ref.shape[2]
    i = pl.program_id(0)
    slot = i % 2

    def reader(tab, k):
        return lambda j: tab[k, 0, j]

    def copier(s, k):
        def make_copy(local, src, n_groups):
            return pltpu.make_async_copy(ybuf_ref.at[_rows(src, n_groups)],
                                         ys_ref.at[s, k, _rows(local, n_groups)], sem.at[s, k])
        return make_copy

    def gather(tab, s):
        for k in range(TILES_PER_STEP):
            entry = reader(tab, k)
            covered = entry(TAB_LOCAL + N_EXPERTS - 1) + entry(TAB_COUNT + N_EXPERTS - 1)

            def zero(g, c, k=k):
                ys_ref[s, k, _rows(g, 1), :] = jnp.zeros((ROW_GROUP, ys_ref.shape[3]), ys_ref.dtype)
                return c
            lax.fori_loop(covered, n_local // ROW_GROUP, zero, 0)
            _for_each_segment_copy(entry, copier(s, k), lambda cp: cp.start())

    @pl.when(i == 0)
    def _():
        gather(tab_ref, 0)

    @pl.when(i + 1 < pl.num_programs(0))
    def _():
        gather(tab_next_ref, 1 - slot)

    for k in range(TILES_PER_STEP):
        _for_each_segment_copy(reader(tab_ref, k), copier(slot, k), lambda cp: cp.wait())

    for k in range(TILES_PER_STEP):
        rows = slice(k * tr, (k + 1) * tr)
        info = rinfo_ref[rows, :]
        ciota = lax.broadcasted_iota(jnp.int32, (tr, n_local), 1)
        pick = jnp.where(ciota == info[:, 0:1].astype(jnp.int32), 1.0,
                         jnp.where(ciota == info[:, 1:2].astype(jnp.int32), 1.0, 0.0)).astype(BF16)
        y_lo, y_hi = _unpack_halves(ys_ref[slot, k])
        h = h2_ref[rows, :] + jnp.concatenate([_dot(pick, y_lo), _dot(pick, y_hi)], axis=-1)
        out_ref[rows, :] = _rms(h, gf_ref[...])


def _combine(table, h2, rinfo, gf, ybuf):
    t, d = h2.shape
    tr = TILES_PER_STEP * SEQ_TILE
    n_steps = t // tr
    assert n_steps * tr == t
    tab_spec = (TILES_PER_STEP, 1, table.shape[2])
    return pl.pallas_call(
        _combine_kernel,
        grid=(n_steps,),
        in_specs=[pl.BlockSpec(tab_spec, lambda i: (i, 0, 0), memory_space=pltpu.SMEM),
                  pl.BlockSpec(tab_spec, lambda i: (jnp.minimum(i + 1, n_steps - 1), 0, 0),
                               memory_space=pltpu.SMEM),
                  pl.BlockSpec((tr, d), lambda i: (i, 0)),
                  pl.BlockSpec((tr, LANES), lambda i: (i, 0)),
                  _const_spec(gf.shape),
                  pl.BlockSpec(memory_space=pl.ANY)],
        out_specs=pl.BlockSpec((tr, d), lambda i: (i, 0)),
        out_shape=jax.ShapeDtypeStruct((t, d), F32),
        scratch_shapes=[pltpu.VMEM((2, TILES_PER_STEP, LOCAL_ROWS, d // 2), jnp.uint32),
                        pltpu.SemaphoreType.DMA((2, TILES_PER_STEP))],
        compiler_params=pltpu.CompilerParams(
            dimension_semantics=("arbitrary",), vmem_limit_bytes=VMEM_LIMIT_BYTES),
        name="combine",
    )(table, table, h2, rinfo, gf, ybuf)


def _pad_lanes(v):
    return jnp.pad(v, (0, LANES - v.shape[0]))[None, :]


def kernel(x, mem, norm_mem, norm_mix, w_in, conv_ssd_w, conv_ssd_b, dt_bias, a_log, d_skip, norm_ssd_gate,
           conv_short_w, w_out, norm_xattn, w_q, w_kv, w_o, norm_moe, w_router_group, b_router_group,
           w_router_expert, b_router_expert, w_gate, w_up, w_down, norm_final):
    bsz, seq, d = x.shape
    t = bsz * seq
    assert w_in.shape[0] == 1, "single-layer model"
    assert d == D_MODEL and seq % SEQ_TILE == 0

    tri = jnp.tril(jnp.ones((SSD_CHUNK, SSD_CHUNK), F32))
    upper = jnp.triu(jnp.ones((SEQ_TILE, SEQ_TILE), F32), k=1).astype(BF16)

    o2 = SSD_WIDTH + XBC_WIDTH
    o3 = o2 + SSD_HEADS
    wl = w_in[0]
    win = jnp.concatenate(
        [wl[:, :o2], wl[:, o3:], wl[:, o2:o3], jnp.zeros((d, LANES - SSD_HEADS), F32)], axis=1).astype(BF16)
    h1, wgu, wdb = _mixer(x, norm_mix, win, conv_ssd_w[0], conv_ssd_b, _pad_lanes(dt_bias[0]),
                _pad_lanes(-jnp.exp(a_log[0])), jnp.repeat(d_skip[0], SSD_HEAD_DIM)[None, :], norm_ssd_gate,
                conv_short_w[0], w_out[0].astype(BF16), tri, w_gate[0], w_up[0], w_down[0])

    kt, v = _memory_kv(mem, norm_mem[None, :], w_kv[0].astype(BF16))
    wr = jnp.zeros((d, LANES), F32)
    wr = wr.at[:, ROUTER_GROUP_ROW:ROUTER_GROUP_ROW + N_GROUPS_MOE].set(w_router_group[0])
    wr = wr.at[:, ROUTER_EXPERT_ROW:ROUTER_EXPERT_ROW + N_EXPERTS].set(w_router_expert[0])
    wr_hi = wr.astype(BF16)
    wr_lo = (wr - wr_hi.astype(F32)).astype(BF16)
    br = jnp.zeros((LANES,), F32)
    br = br.at[ROUTER_GROUP_ROW:ROUTER_GROUP_ROW + N_GROUPS_MOE].set(b_router_group[0])
    br = br.at[ROUTER_EXPERT_ROW:ROUTER_EXPERT_ROW + N_EXPERTS].set(b_router_expert[0])
    lower = jnp.tril(jnp.ones((N_EXPERTS, N_EXPERTS), F32), k=-1).astype(BF16)
    n_tiles = t // SEQ_TILE
    bgroups = EXPERT_BLOCK // ROW_GROUP
    max_groups = (2 * t + n_tiles * N_EXPERTS * (ROW_GROUP - 1)) // ROW_GROUP
    n_blocks = -(-max_groups // bgroups) + N_EXPERTS
    even = math.lcm((n_tiles * SUBLANES) // math.gcd(n_tiles * SUBLANES, EXPERT_BLOCK), BLOCKS_PER_STEP)
    n_blocks = -(-n_blocks // even) * even
    h2, xs, rinfo, seg, counts, xbuf0 = _xattn_router(
        h1, norm_xattn, w_q[0].astype(BF16), kt, v, w_o[0].astype(BF16), norm_moe,
        jnp.concatenate([wr_hi, wr_lo], axis=1), br[None, :], upper, lower, n_blocks * EXPERT_BLOCK)

    seg = seg.reshape(n_tiles, N_EXPERTS, LANES)[:, :, 0:3].astype(jnp.int32)
    seg_groups, seg_local, seg_before = seg[:, :, 0], seg[:, :, 1], seg[:, :, 2]
    sizes = counts[:, 0].astype(jnp.int32)
    padded = ((sizes + bgroups - 1) // bgroups) * bgroups
    pad_end = jnp.cumsum(padded)
    pad_start = pad_end - padded
    seg_dst = pad_start[None, :] + seg_before
    table = jnp.concatenate([seg_groups, seg_dst, seg_local], axis=1)[:, None, :]
    block_group = jnp.arange(n_blocks, dtype=jnp.int32) * bgroups
    block_expert = jnp.minimum(jnp.sum((pad_end[None, :] <= block_group[:, None]).astype(jnp.int32), axis=-1),
                               N_EXPERTS - 1)
    n_used = (pad_end[-1:] // bgroups).astype(jnp.int32)

    xbuf = _dispatch(table, xs, xbuf0)
    ybuf = _experts(block_expert, n_used, xbuf, wgu, wdb)
    out = _combine(table, h2.reshape(t, d), rinfo, norm_final[None, :], ybuf)
    return out.reshape(bsz, seq, d)
```
